```python
import jax, jax.numpy as jnp
from jax import lax
import numpy as np

D_MODEL = 1024
BATCH = 4
SEQ = 4096
DEPTH = 1

CHUNK = 64
Q_BLOCK = 128
SB_HEADS = 8
SB_HEAD_DIM = 64
SB_WIDTH = SB_HEADS * SB_HEAD_DIM
HG_HEADS = 4
HG_HEAD_DIM = 128
HG_WIDTH = HG_HEADS * HG_HEAD_DIM
N_BRANCH = 2
D_FF = 4 * D_MODEL
IN_COLS = 3 * SB_WIDTH + 4 * HG_WIDTH + N_BRANCH * D_MODEL
EPS = 1e-6

kernel_name = "hybrid_stickbreak_hgrn2_gated_merge"


def rmsnorm(x, g):
    xf = x.astype(jnp.float32)
    y = xf * lax.rsqrt(jnp.mean(xf * xf, axis=-1, keepdims=True) + EPS)
    return y * g.astype(jnp.float32)


def split_heads(t, n_heads):
    b, s, w = t.shape
    return t.reshape(b, s, n_heads, w // n_heads).transpose(0, 2, 1, 3)


def merge_heads(t):
    b, h, s, d = t.shape
    return t.transpose(0, 2, 1, 3).reshape(b, s, h * d)


def stick_breaking_attention(q, k, v):
    b, h, s, dh = q.shape
    nb = s // Q_BLOCK
    scale = dh ** -0.5
    qb = q.reshape(b, h, nb, Q_BLOCK, dh).transpose(2, 0, 1, 3, 4)
    kpos = jnp.arange(s)

    def block(args):
        qi, i = args
        qpos = i * Q_BLOCK + jnp.arange(Q_BLOCK)
        z = jnp.einsum('bhqd,bhkd->bhqk', qi, k) * scale
        mask = kpos[None, :] < qpos[:, None]
        log_beta = jax.nn.log_sigmoid(z)
        log_rem = jnp.where(mask, log_beta - z, 0.0)
        between = lax.cumsum(log_rem, axis=3, reverse=True) - log_rem
        w = jnp.where(mask, jnp.exp(log_beta + between), 0.0)
        return jnp.einsum('bhqk,bhkd->bhqd', w, v)

    out = lax.map(block, (qb, jnp.arange(nb)))
    return out.transpose(1, 2, 0, 3, 4).reshape(b, h, s, dh)


def hgrn2_chunkwise(q, k, v, log_f):
    b, h, s, dk = q.shape
    dv = v.shape[-1]
    nc = s // CHUNK

    def to_chunks(t):
        return t.reshape(b, h, nc, CHUNK, t.shape[-1]).transpose(2, 0, 1, 3, 4)

    qc, kc, vc, gc = to_chunks(q), to_chunks(k), to_chunks(v), to_chunks(log_f)
    causal = jnp.arange(CHUNK)[:, None] >= jnp.arange(CHUNK)[None, :]

    def step(state, inp):
        qi, ki, vi, gi = inp
        bcum = jnp.cumsum(gi, axis=2)
        diff = bcum[:, :, :, None, :] - bcum[:, :, None, :, :]
        decay = jnp.exp(jnp.where(causal[:, :, None], diff, -jnp.inf))
        scores = jnp.einsum('bhtd,bhsd,bhtsd->bhts', qi, ki, decay)
        o = jnp.einsum('bhts,bhse->bhte', scores, vi) \
            + jnp.einsum('bhtd,bhde->bhte', qi * jnp.exp(bcum), state)
        b_last = bcum[:, :, -1:, :]
        k_dec = ki * jnp.exp(b_last - bcum)
        state = jnp.exp(b_last[:, :, 0, :])[..., None] * state \
            + jnp.einsum('bhsd,bhse->bhde', k_dec, vi)
        return state, o

    s0 = jnp.zeros((b, h, dk, dv), jnp.float32)
    _, o = lax.scan(step, s0, (qc, kc, vc, gc))
    return o.transpose(1, 2, 0, 3, 4).reshape(b, h, s, dv)


def hybrid_layer(x, layer, norm1_g, w_in, b_gate, lb_logits, hg_norm_g, w_o_sb, w_o_hg, w_out,
                 norm2_g, w_ff1, w_ff2):
    f32 = jnp.float32
    xn = rmsnorm(x, norm1_g[layer])
    u = xn @ w_in[layer].astype(f32)
    c = np.cumsum([SB_WIDTH, SB_WIDTH, SB_WIDTH, HG_WIDTH, HG_WIDTH, HG_WIDTH, HG_WIDTH, D_MODEL])
    q_sb, k_sb, v_sb, f_raw, i_hg, q_hg, g_hg, gate_sb, gate_hg = jnp.split(u, list(c), axis=-1)

    o_sb = stick_breaking_attention(split_heads(q_sb, SB_HEADS), split_heads(k_sb, SB_HEADS),
                                    split_heads(v_sb, SB_HEADS))
    o_sb = merge_heads(o_sb)

    lb_all = jnp.cumsum(jax.nn.softmax(lb_logits.astype(f32), axis=0), axis=0)
    lb = lb_all[layer]
    sig = jax.nn.sigmoid(f_raw)
    f = lb + (1.0 - lb) * sig
    log_f = jnp.log(f)
    k_hg = (1.0 - lb) * jax.nn.sigmoid(-f_raw)
    o_hg = hgrn2_chunkwise(split_heads(jax.nn.silu(q_hg), HG_HEADS), split_heads(k_hg, HG_HEADS),
                           split_heads(i_hg, HG_HEADS), split_heads(log_f, HG_HEADS))
    o_hg = rmsnorm(o_hg, hg_norm_g[layer].reshape(HG_HEADS, 1, HG_HEAD_DIM))
    o_hg = merge_heads(o_hg) * jax.nn.silu(g_hg)

    bg = b_gate[layer].astype(f32)
    g_a = jax.nn.sigmoid(gate_sb + bg[:D_MODEL])
    g_b = jax.nn.sigmoid(gate_hg + bg[D_MODEL:])
    merged = g_a * (o_sb @ w_o_sb[layer].astype(f32)) + g_b * (o_hg @ w_o_hg[layer].astype(f32))
    h = x.astype(f32) + merged @ w_out[layer].astype(f32)

    hn = rmsnorm(h, norm2_g[layer])
    a = jnp.square(jax.nn.relu(hn @ w_ff1[layer].astype(f32)))
    h = h + a @ w_ff2[layer].astype(f32)
    return h


def setup_inputs(seed: int = 0) -> dict:
    key = jax.random.key(seed)
    ks = jax.random.split(key, 14)
    f32 = jnp.float32

    def nrm(k, shape, fan_in):
        return jax.random.normal(k, shape, f32) * (fan_in ** -0.5)

    return {
        "x": jax.random.normal(ks[0], (BATCH, SEQ, D_MODEL), f32),
        "norm1_g": 1.0 + 0.05 * jax.random.normal(ks[1], (DEPTH, D_MODEL), f32),
        "w_in": nrm(ks[2], (DEPTH, D_MODEL, IN_COLS), D_MODEL),
        "b_gate": 0.05 * jax.random.normal(ks[3], (DEPTH, N_BRANCH * D_MODEL), f32),
        "lb_logits": 0.5 * jax.random.normal(ks[4], (DEPTH + 1, HG_WIDTH), f32),
        "hg_norm_g": 1.0 + 0.05 * jax.random.normal(ks[5], (DEPTH, HG_WIDTH), f32),
        "w_o_sb": nrm(ks[6], (DEPTH, SB_WIDTH, D_MODEL), SB_WIDTH),
        "w_o_hg": nrm(ks[7], (DEPTH, HG_WIDTH, D_MODEL), HG_WIDTH),
        "w_out": nrm(ks[8], (DEPTH, D_MODEL, D_MODEL), D_MODEL),
        "norm2_g": 1.0 + 0.05 * jax.random.normal(ks[9], (DEPTH, D_MODEL), f32),
        "w_ff1": nrm(ks[10], (DEPTH, D_MODEL, D_FF), D_MODEL),
        "w_ff2": nrm(ks[11], (DEPTH, D_FF, D_MODEL), D_FF),
        "final_g": 1.0 + 0.05 * jax.random.normal(ks[12], (D_MODEL,), f32),
    }


def reference(x, norm1_g, w_in, b_gate, lb_logits, hg_norm_g, w_o_sb, w_o_hg, w_out,
              norm2_g, w_ff1, w_ff2, final_g):
    h = x.astype(jnp.float32)
    for layer in range(DEPTH):
        h = hybrid_layer(h, layer, norm1_g, w_in, b_gate, lb_logits, hg_norm_g, w_o_sb, w_o_hg,
                         w_out, norm2_g, w_ff1, w_ff2)
    return rmsnorm(h, final_g).astype(x.dtype)
```

```python
import functools

import jax
import jax.numpy as jnp
import numpy as np
from jax import lax
from jax.experimental import pallas as pl
from jax.experimental.pallas import tpu as pltpu

F32 = jnp.float32
BF16 = jnp.bfloat16

SB_HEADS = 8
SB_HEAD_DIM = 64
HG_HEADS = 4
HG_HEAD_DIM = 128
EPS = 1e-6

V7X_LANES = 128
V7X_MXU_DIM = 256
V7X_VMEM_BYTES = 64 * 1024 * 1024

TOKEN_TILE = 256
ATTN_TILE = V7X_MXU_DIM
HG_BLOCK = V7X_MXU_DIM


def _vmem_limit(pipelined_bytes, resident_bytes, temp_bytes):
    need = 2 * pipelined_bytes + resident_bytes + temp_bytes
    return int(min(need + need // 4, V7X_VMEM_BYTES - 8 * 1024 * 1024))


def _resident(shape):
    return pl.BlockSpec(shape, lambda *_: (0,) * len(shape), pipeline_mode=pl.Buffered(1))


def _rms(x, g):
    ms = jnp.mean(x * x, axis=-1, keepdims=True)
    return x * lax.rsqrt(ms + EPS) * g


def _dot(a, b):
    return jnp.dot(a, b, preferred_element_type=F32)


def _dot_nt(a, b):
    return lax.dot_general(a, b, (((1,), (1,)), ((), ())), preferred_element_type=F32)


def _dot_tn(a, b):
    return lax.dot_general(a, b, (((0,), (0,)), ((), ())), preferred_element_type=F32)


def _in_proj_kernel(x_ref, g_ref, w_ref, qkv_ref, f_ref, iqg_ref, gate_ref):
    xn = _rms(x_ref[...], g_ref[...]).astype(BF16)
    c0 = qkv_ref.shape[1]
    c1 = c0 + f_ref.shape[1]
    c2 = c1 + iqg_ref.shape[1]
    c3 = c2 + gate_ref.shape[1]
    qkv_ref[...] = _dot(xn, w_ref[:, 0:c0]).astype(qkv_ref.dtype)
    f_ref[...] = _dot(xn, w_ref[:, c0:c1])
    iqg_ref[...] = _dot(xn, w_ref[:, c1:c2]).astype(iqg_ref.dtype)
    gate_ref[...] = _dot(xn, w_ref[:, c2:c3])


def _in_proj(x2, g1, w_in, sb_w, hg_w, d):
    n = x2.shape[0]
    tm = TOKEN_TILE
    cols = w_in.shape[1]
    widths = (3 * sb_w, hg_w, 3 * hg_w, 2 * d)
    dtypes = (BF16, F32, BF16, F32)
    row = lambda i: (i, 0)
    out_bytes = sum(tm * w * jnp.dtype(t).itemsize for w, t in zip(widths, dtypes))
    return pl.pallas_call(
        _in_proj_kernel,
        grid=(n // tm,),
        in_specs=[pl.BlockSpec((tm, d), row), _resident((1, d)), _resident((d, cols))],
        out_specs=[pl.BlockSpec((tm, w), row) for w in widths],
        out_shape=[jax.ShapeDtypeStruct((n, w), t) for w, t in zip(widths, dtypes)],
        compiler_params=pltpu.CompilerParams(
            dimension_semantics=("parallel",),
            vmem_limit_bytes=_vmem_limit(tm * d * 4 + out_bytes, d * 4 + d * cols * 2,
                                         tm * (d * 6 + cols * 4))),
        name="in_proj",
    )(x2, g1, w_in)


def _sb_weights(z, tri, carry, mask):
    l1p = jnp.log(1.0 + jnp.exp(-jnp.abs(z)))
    log_beta = jnp.minimum(z, 0.0) - l1p
    log_rem = jnp.minimum(-z, 0.0) - l1p
    if mask is not None:
        log_rem = jnp.where(mask, log_rem, 0.0)
    hi = log_rem.astype(BF16)
    lo = (log_rem - hi.astype(F32)).astype(BF16)
    between = _dot(hi, tri) + _dot(lo, tri) + carry
    w = jnp.exp(log_beta + between)
    if mask is not None:
        w = jnp.where(mask, w, 0.0)
    return w, carry + jnp.sum(log_rem, axis=1, keepdims=True)


def _sb_attn_kernel(q_ref, k_ref, v_ref, o_ref):
    t = q_ref.shape[1]
    qi = pl.program_id(2)
    row = lax.broadcasted_iota(jnp.int32, (t, t), 0)
    col = lax.broadcasted_iota(jnp.int32, (t, t), 1)
    tri = (row > col).astype(BF16)
    causal = col < row
    lane = lax.broadcasted_iota(jnp.int32, (t, 2 * SB_HEAD_DIM), 1)
    head0 = lane < SB_HEAD_DIM
    zero = jnp.zeros((), BF16)
    q = q_ref[0] * jnp.asarray(SB_HEAD_DIM ** -0.5, BF16)
    q0 = jnp.where(head0, q, zero)
    q1 = jnp.where(head0, zero, q)

    def tile(j, carry0, carry1, acc, mask):
        start = pl.multiple_of(j * t, t)
        k = k_ref[0, pl.ds(start, t), :]
        v = v_ref[0, pl.ds(start, t), :]
        w0, carry0 = _sb_weights(_dot_nt(q0, k), tri, carry0, mask)
        w1, carry1 = _sb_weights(_dot_nt(q1, k), tri, carry1, mask)
        w = jnp.concatenate([w0, w1], axis=1).astype(BF16)
        v2 = jnp.concatenate([jnp.where(head0, v, zero), jnp.where(head0, zero, v)], axis=0)
        return carry0, carry1, acc + _dot(w, v2)

    zc = jnp.zeros((t, 1), F32)
    state = tile(qi, zc, zc, jnp.zeros((t, 2 * SB_HEAD_DIM), F32), causal)

    def body(i, st):
        return tile(qi - 1 - i, *st, None)

    _, _, acc = lax.fori_loop(0, qi, body, state)
    o_ref[0] = acc.astype(o_ref.dtype)


def _sb_attn(qkv3):
    b, s, _ = qkv3.shape
    t = ATTN_TILE
    pairs = SB_HEADS // 2
    w = 2 * SB_HEAD_DIM
    assert w == V7X_LANES and s % t == 0
    block_bytes = (2 * t * w + 2 * s * w) * 2
    return pl.pallas_call(
        _sb_attn_kernel,
        grid=(b, pairs, s // t),
        in_specs=[pl.BlockSpec((1, t, w), lambda bi, p, i: (bi, i, p)),
                  pl.BlockSpec((1, s, w), lambda bi, p, i: (bi, 0, pairs + p)),
                  pl.BlockSpec((1, s, w), lambda bi, p, i: (bi, 0, 2 * pairs + p))],
        out_specs=pl.BlockSpec((1, t, w), lambda bi, p, i: (bi, i, p)),
        out_shape=jax.ShapeDtypeStruct((b, s, pairs * w), BF16),
        compiler_params=pltpu.CompilerParams(
            dimension_semantics=("parallel", "parallel", "arbitrary"),
            vmem_limit_bytes=_vmem_limit(block_bytes, 0, 24 * t * t * 4)),
        name="sb_attn",
    )(qkv3, qkv3, qkv3)


def _split3(x):
    a = x.astype(BF16)
    r = x - a.astype(F32)
    b = r.astype(BF16)
    c = (r - b.astype(F32)).astype(BF16)
    return a, b, c


def _rows_from_group(b, group, r):
    n, c = b.shape
    if group == n:
        return jnp.broadcast_to(b[r:r + 1, :], (n, c))
    b3 = b.reshape(n // group, group, c)
    return jnp.broadcast_to(b3[:, r:r + 1, :], b3.shape).reshape(n, c)


def _midpoint_rows(b, group, pos):
    n = b.shape[0]
    half = group // 2
    if group >= 16:
        return _rows_from_group(b, group, half - 1)
    if group == 8:
        return _rows_from_group(b, 8, 3)
    up1 = pltpu.roll(b, n - 1, 0)
    dn1 = pltpu.roll(b, 1, 0)
    if group == 2:
        return jnp.where((pos & 1) == 0, b, dn1)
    assert group == 4
    dn2 = pltpu.roll(b, 2, 0)
    r4 = pos & 3
    return jnp.where(r4 == 0, up1, jnp.where(r4 == 1, b, jnp.where(r4 == 2, dn1, dn2)))


def _hgrn2_kernel(f_ref, i_ref, q_ref, g_ref, lbl_ref, ng_ref, lvl_ref, o_ref, st_ref):
    t = f_ref.shape[1]
    dk = HG_HEAD_DIM

    @pl.when(pl.program_id(1) == 0)
    def _():
        st_ref[...] = jnp.zeros_like(st_ref)

    row = lax.broadcasted_iota(jnp.int32, (t, t), 0)
    col = lax.broadcasted_iota(jnp.int32, (t, t), 1)
    tril = (row >= col).astype(BF16)
    pos = lax.broadcasted_iota(jnp.int32, (t, dk), 0)
    lvl = lvl_ref[...]
    n_levels = t.bit_length() - 1

    lbl = lbl_ref[...]
    ex = jnp.exp(lbl - jnp.max(lbl, axis=0, keepdims=True))
    lb_all = ex[0:1, :] / jnp.sum(ex, axis=0, keepdims=True)

    for h in range(HG_HEADS):
        sl = slice(h * dk, (h + 1) * dk)
        lb = lb_all[:, sl]
        fr = f_ref[0, :, sl]
        e = jnp.exp(-jnp.abs(fr))
        r = 1.0 / (1.0 + e)
        er = e * r
        sig = jnp.where(fr >= 0, r, er)
        nsig = jnp.where(fr >= 0, er, r)
        logf = jnp.log(lb + (1.0 - lb) * sig)
        k = (1.0 - lb) * nsig
        qr = q_ref[0, :, sl].astype(F32)
        q = qr / (1.0 + jnp.exp(-qr))
        v = i_ref[0, :, sl]

        g1, g2, g3 = _split3(logf)
        b = _dot(tril, g1) + _dot(tril, g2) + _dot(tril, g3)

        scores = jnp.zeros((t, t), F32)
        for level in range(1, n_levels + 1):
            group = 1 << level
            d = b - _midpoint_rows(b, group, pos)
            later = (pos & (group - 1)) >= (group // 2)
            fac = jnp.exp(-jnp.abs(d))
            ql = jnp.where(later, q * fac, 0.0).astype(BF16)
            kl = jnp.where(later, 0.0, k * fac).astype(BF16)
            scores = jnp.where(lvl == level, _dot_nt(ql, kl), scores)

        st = st_ref[h]
        b_last = b[t - 1:t, :]
        o = _dot(scores.astype(BF16), v)
        o = o + _dot_nt((q * jnp.exp(b)).astype(BF16), st.astype(BF16))
        o = o + jnp.sum(q * k, axis=1, keepdims=True) * v.astype(F32)
        k_dec = (k * jnp.exp(b_last - b)).astype(BF16)
        st_ref[h] = jnp.exp(b_last) * st + _dot_tn(v, k_dec)

        gr = g_ref[0, :, sl].astype(F32)
        y = _rms(o, ng_ref[:, sl]) * (gr / (1.0 + jnp.exp(-gr)))
        o_ref[0, :, sl] = y.astype(o_ref.dtype)


def _pair_levels(t):
    idx = np.arange(t)
    x = idx[:, None] ^ idx[None, :]
    lev = np.where(x > 0, np.floor(np.log2(np.maximum(x, 1))).astype(np.int64) + 1, 0)
    return np.where(idx[:, None] > idx[None, :], lev, 0).astype(np.int32)


def _hgrn2(f3, iqg3, lb_logits, ng):
    b, s, hw = f3.shape
    t = HG_BLOCK
    assert hw == HG_HEADS * HG_HEAD_DIM and s % t == 0
    lvl = jnp.asarray(_pair_levels(t))
    blk = lambda c: pl.BlockSpec((1, t, hw), lambda bi, i: (bi, i, c))
    block_bytes = t * hw * (4 + 4 * 2)
    resident_bytes = t * t * 4 + 3 * hw * 4
    return pl.pallas_call(
        _hgrn2_kernel,
        grid=(b, s // t),
        in_specs=[blk(0), blk(0), blk(1), blk(2),
                  _resident(lb_logits.shape), _resident(ng.shape), _resident((t, t))],
        out_specs=blk(0),
        out_shape=jax.ShapeDtypeStruct((b, s, hw), BF16),
        scratch_shapes=[pltpu.VMEM((HG_HEADS, HG_HEAD_DIM, HG_HEAD_DIM), F32)],
        compiler_params=pltpu.CompilerParams(
            dimension_semantics=("parallel", "arbitrary"),
            vmem_limit_bytes=_vmem_limit(block_bytes, resident_bytes, 64 * t * t * 4)),
        name="hgrn2",
    )(f3, iqg3, iqg3, iqg3, lb_logits, ng, lvl)


def _merge_ffn_kernel(x_ref, osb_ref, ohg_ref, gate_ref, bg_ref, wsb_ref, whg_ref, wout_ref,
                      g2_ref, w1_ref, w2_ref, gf_ref, o_ref):
    d = x_ref.shape[1]
    gates = 1.0 / (1.0 + jnp.exp(-(gate_ref[...] + bg_ref[...])))
    merged = gates[:, :d] * _dot(osb_ref[...], wsb_ref[...]) \
        + gates[:, d:] * _dot(ohg_ref[...], whg_ref[...])
    h = x_ref[...] + _dot(merged.astype(BF16), wout_ref[...])
    hn = _rms(h, g2_ref[...]).astype(BF16)
    a = jnp.maximum(_dot(hn, w1_ref[...]), 0.0)
    h = h + _dot((a * a).astype(BF16), w2_ref[...])
    o_ref[...] = _rms(h, gf_ref[...]).astype(o_ref.dtype)


def _merge_ffn(x2, osb, ohg, gates, bg, wsb, whg, wout, g2, w1, w2, gf):
    n, d = x2.shape
    tm = TOKEN_TILE
    dff = w1.shape[1]
    row = lambda i: (i, 0)
    full = lambda a: _resident(a.shape)
    resident_bytes = sum(a.size * a.dtype.itemsize for a in (bg, wsb, whg, wout, g2, w1, w2, gf))
    block_bytes = tm * (d * 4 + osb.shape[1] * 2 + ohg.shape[1] * 2 + 2 * d * 4 + d * 4)
    return pl.pallas_call(
        _merge_ffn_kernel,
        grid=(n // tm,),
        in_specs=[pl.BlockSpec((tm, d), row),
                  pl.BlockSpec((tm, osb.shape[1]), row),
                  pl.BlockSpec((tm, ohg.shape[1]), row),
                  pl.BlockSpec((tm, 2 * d), row),
                  full(bg), full(wsb), full(whg), full(wout), full(g2), full(w1), full(w2),
                  full(gf)],
        out_specs=pl.BlockSpec((tm, d), row),
        out_shape=jax.ShapeDtypeStruct((n, d), x2.dtype),
        compiler_params=pltpu.CompilerParams(
            dimension_semantics=("parallel",),
            vmem_limit_bytes=_vmem_limit(block_bytes, resident_bytes, tm * (dff * 6 + d * 24))),
        name="merge_ffn",
    )(x2, osb, ohg, gates, bg, wsb, whg, wout, g2, w1, w2, gf)


def kernel(x, norm1_g, w_in, b_gate, lb_logits, hg_norm_g, w_o_sb, w_o_hg, w_out, norm2_g,
           w_ff1, w_ff2, final_g):
    b, s, d = x.shape
    assert w_in.shape[0] == 1, "single-layer block"
    sb_w = SB_HEADS * SB_HEAD_DIM
    hg_w = HG_HEADS * HG_HEAD_DIM
    x2 = x.reshape(b * s, d)
    qkv, f_raw, iqg, gates = _in_proj(x2, norm1_g, w_in[0].astype(BF16), sb_w, hg_w, d)
    o_sb = _sb_attn(qkv.reshape(b, s, 3 * sb_w))
    o_hg = _hgrn2(f_raw.reshape(b, s, hg_w), iqg.reshape(b, s, 3 * hg_w), lb_logits, hg_norm_g)
    out = _merge_ffn(x2, o_sb.reshape(b * s, sb_w), o_hg.reshape(b * s, hg_w), gates, b_gate,
                     w_o_sb[0].astype(BF16), w_o_hg[0].astype(BF16), w_out[0].astype(BF16),
                     norm2_g, w_ff1[0].astype(BF16), w_ff2[0].astype(BF16),
                     final_g.reshape(1, d))
    return out.reshape(b, s, d)
```

```python
import functools

import jax
import jax.numpy as jnp
import numpy as np
from jax import lax
from jax.experimental import pallas as pl
from jax.experimental.pallas import tpu as pltpu

F32 = jnp.float32
BF16 = jnp.bfloat16

SB_HEADS = 8
SB_HEAD_DIM = 64
HG_HEADS = 4
HG_HEAD_DIM = 128
EPS = 1e-6
LOG2E = 1.4426950408889634

V7X_LANES = 128
V7X_MXU_DIM = 256
V7X_VMEM_BYTES = 64 * 1024 * 1024

TOKEN_TILE = 256
ATTN_TILE = V7X_MXU_DIM
ATTN_HEADS = 8
HG_BLOCK = V7X_MXU_DIM


def _vmem_limit(pipelined_bytes, resident_bytes, temp_bytes):
    need = 2 * pipelined_bytes + resident_bytes + temp_bytes
    return int(min(need + need // 4, V7X_VMEM_BYTES - 8 * 1024 * 1024))


def _resident(shape):
    return pl.BlockSpec(shape, lambda *_: (0,) * len(shape), pipeline_mode=pl.Buffered(1))


def _rms(x, g):
    ms = jnp.mean(x * x, axis=-1, keepdims=True)
    return x * lax.rsqrt(ms + EPS) * g


def _dot(a, b):
    return jnp.dot(a, b, preferred_element_type=F32)


def _dot_nt(a, b):
    return lax.dot_general(a, b, (((1,), (1,)), ((), ())), preferred_element_type=F32)


def _dot_tn(a, b):
    return lax.dot_general(a, b, (((0,), (0,)), ((), ())), preferred_element_type=F32)


def _in_proj_kernel(x_ref, g_ref, w_ref, wvt_ref, q_ref, k_ref, vt_ref, f_ref, iqg_ref,
                    gate_ref):
    xn = _rms(x_ref[...], g_ref[...]).astype(BF16)
    sb_w = q_ref.shape[1]
    c0 = 3 * sb_w
    c1 = c0 + f_ref.shape[1]
    c2 = c1 + iqg_ref.shape[1]
    c3 = c2 + gate_ref.shape[1]
    q_ref[...] = (_dot(xn, w_ref[:, 0:sb_w]) * (SB_HEAD_DIM ** -0.5 * LOG2E)).astype(q_ref.dtype)
    k_ref[...] = _dot(xn, w_ref[:, sb_w:2 * sb_w]).astype(k_ref.dtype)
    vt_ref[0] = _dot_nt(wvt_ref[...], xn).astype(vt_ref.dtype)
    f_ref[...] = _dot(xn, w_ref[:, c0:c1])
    iqg_ref[...] = _dot(xn, w_ref[:, c1:c2]).astype(iqg_ref.dtype)
    gate_ref[...] = _dot(xn, w_ref[:, c2:c3])


def _in_proj(x2, g1, w_in, w_vt, batch, sb_w, hg_w, d):
    n = x2.shape[0]
    tm = TOKEN_TILE
    cols = w_in.shape[1]
    tiles_per_seq = n // batch // tm
    widths = (sb_w, sb_w, hg_w, 3 * hg_w, 2 * d)
    dtypes = (BF16, BF16, F32, BF16, F32)
    row = lambda i: (i, 0)
    rows = lambda w: pl.BlockSpec((tm, w), row)
    vt_spec = pl.BlockSpec((1, sb_w, tm), lambda i: (i // tiles_per_seq, 0, i % tiles_per_seq))
    sds = lambda w, t: jax.ShapeDtypeStruct((n, w), t)
    out_bytes = sum(tm * w * jnp.dtype(t).itemsize for w, t in zip(widths, dtypes)) \
        + tm * sb_w * 2
    return pl.pallas_call(
        _in_proj_kernel,
        grid=(n // tm,),
        in_specs=[rows(d), _resident((1, d)), _resident((d, cols)), _resident((sb_w, d))],
        out_specs=[rows(sb_w), rows(sb_w), vt_spec, rows(hg_w), rows(3 * hg_w), rows(2 * d)],
        out_shape=[sds(sb_w, BF16), sds(sb_w, BF16),
                   jax.ShapeDtypeStruct((batch, sb_w, n // batch), BF16),
                   sds(hg_w, F32), sds(3 * hg_w, BF16), sds(2 * d, F32)],
        compiler_params=pltpu.CompilerParams(
            dimension_semantics=("parallel",),
            vmem_limit_bytes=_vmem_limit(tm * d * 4 + out_bytes,
                                         d * 4 + d * cols * 2 + sb_w * d * 2,
                                         tm * (d * 6 + cols * 4))),
        name="in_proj",
    )(x2, g1, w_in, w_vt)


def _softplus2(z, mask):
    sp = jnp.maximum(z, 0.0) + jnp.log(1.0 + jnp.exp2(-jnp.abs(z))) * LOG2E
    return sp if mask is None else jnp.where(mask, sp, 0.0)


def _sb_weights(z, sp, later, mask):
    w = jnp.exp2(z - sp - later)
    return w if mask is None else jnp.where(mask, w, 0.0)


def _sb_attn_kernel(q_ref, k_ref, vt_ref, o_ref):
    t = q_ref.shape[1]
    n_heads = q_ref.shape[2] // SB_HEAD_DIM
    qi = pl.program_id(2)
    row = lax.broadcasted_iota(jnp.int32, (t, t), 0)
    col = lax.broadcasted_iota(jnp.int32, (t, t), 1)
    tri = (col > row).astype(BF16)
    causal = row < col
    lane = lax.broadcasted_iota(jnp.int32, (t, V7X_LANES), 1)
    zero = jnp.zeros((), BF16)
    q = q_ref[0]
    qm = []
    for h in range(n_heads):
        grp = q[:, (h // 2) * V7X_LANES:(h // 2 + 1) * V7X_LANES]
        qm.append(jnp.where((lane // SB_HEAD_DIM) == (h % 2), grp, zero))

    def tile(j, carries, accs, mask):
        start = pl.multiple_of(j * t, t)
        k = k_ref[0, pl.ds(start, t), :]
        heads = range(n_heads)
        z = [_dot_nt(k[:, (h // 2) * V7X_LANES:(h // 2 + 1) * V7X_LANES], qm[h]) for h in heads]
        sp = [_softplus2(z[h], mask) for h in heads]
        later = [_dot(tri, sp[h].astype(BF16)) for h in heads]
        w = [_sb_weights(z[h], sp[h], later[h], mask).astype(BF16) for h in heads]
        new_a = tuple(
            accs[h] + jnp.exp2(-carries[h])
            * _dot(vt_ref[0, h * SB_HEAD_DIM:(h + 1) * SB_HEAD_DIM, pl.ds(start, t)], w[h])
            for h in heads)
        new_c = tuple(carries[h] + later[h][0:1, :] + sp[h][0:1, :] for h in heads)
        return new_c, new_a

    zc = tuple(jnp.zeros((1, t), F32) for _ in range(n_heads))
    za = tuple(jnp.zeros((SB_HEAD_DIM, t), F32) for _ in range(n_heads))
    state = tile(qi, zc, za, causal)

    def body(i, st):
        return tile(qi - 1 - i, st[0], st[1], None)

    _, accs = lax.fori_loop(0, qi, body, state)
    o_ref[0] = jnp.concatenate(accs, axis=0).T.astype(o_ref.dtype)


def _sb_attn(q3, k3, vt3):
    b, s, sb_w = q3.shape
    t = ATTN_TILE
    w = ATTN_HEADS * SB_HEAD_DIM
    groups = sb_w // w
    assert w % V7X_LANES == 0 and s % t == 0
    block_bytes = (2 * t * w + 2 * s * w) * 2
    return pl.pallas_call(
        _sb_attn_kernel,
        grid=(b, groups, s // t),
        in_specs=[pl.BlockSpec((1, t, w), lambda bi, p, i: (bi, i, p)),
                  pl.BlockSpec((1, s, w), lambda bi, p, i: (bi, 0, p)),
                  pl.BlockSpec((1, w, s), lambda bi, p, i: (bi, p, 0))],
        out_specs=pl.BlockSpec((1, t, w), lambda bi, p, i: (bi, i, p)),
        out_shape=jax.ShapeDtypeStruct((b, s, sb_w), BF16),
        compiler_params=pltpu.CompilerParams(
            dimension_semantics=("parallel", "parallel", "arbitrary"),
            vmem_limit_bytes=_vmem_limit(block_bytes, 0, 12 * ATTN_HEADS * t * t * 4)),
        name="sb_attn",
    )(q3, k3, vt3)


def _split3(x):
    a = x.astype(BF16)
    r = x - a.astype(F32)
    b = r.astype(BF16)
    c = (r - b.astype(F32)).astype(BF16)
    return a, b, c


def _rows_from_group(b, group, r):
    n, c = b.shape
    if group == n:
        return jnp.broadcast_to(b[r:r + 1, :], (n, c))
    b3 = b.reshape(n // group, group, c)
    return jnp.broadcast_to(b3[:, r:r + 1, :], b3.shape).reshape(n, c)


def _midpoint_rows(b, group, pos):
    n = b.shape[0]
    half = group // 2
    if group >= 16:
        return _rows_from_group(b, group, half - 1)
    if group == 8:
        return _rows_from_group(b, 8, 3)
    up1 = pltpu.roll(b, n - 1, 0)
    dn1 = pltpu.roll(b, 1, 0)
    if group == 2:
        return jnp.where((pos & 1) == 0, b, dn1)
    assert group == 4
    dn2 = pltpu.roll(b, 2, 0)
    r4 = pos & 3
    return jnp.where(r4 == 0, up1, jnp.where(r4 == 1, b, jnp.where(r4 == 2, dn1, dn2)))


def _hgrn2_kernel(f_ref, i_ref, q_ref, g_ref, lbl_ref, ng_ref, lvl_ref, o_ref, st_ref):
    t = f_ref.shape[1]
    dk = HG_HEAD_DIM

    @pl.when(pl.program_id(1) == 0)
    def _():
        st_ref[...] = jnp.zeros_like(st_ref)

    row = lax.broadcasted_iota(jnp.int32, (t, t), 0)
    col = lax.broadcasted_iota(jnp.int32, (t, t), 1)
    tril = (row >= col).astype(BF16)
    pos = lax.broadcasted_iota(jnp.int32, (t, dk), 0)
    lvl = lvl_ref[...]
    n_levels = t.bit_length() - 1

    lbl = lbl_ref[...]
    ex = jnp.exp(lbl - jnp.max(lbl, axis=0, keepdims=True))
    lb_all = ex[0:1, :] / jnp.sum(ex, axis=0, keepdims=True)

    for h in range(HG_HEADS):
        sl = slice(h * dk, (h + 1) * dk)
        lb = lb_all[:, sl]
        fr = f_ref[0, :, sl]
        e = jnp.exp(-jnp.abs(fr))
        r = 1.0 / (1.0 + e)
        er = e * r
        sig = jnp.where(fr >= 0, r, er)
        nsig = jnp.where(fr >= 0, er, r)
        logf = jnp.log(lb + (1.0 - lb) * sig)
        k = (1.0 - lb) * nsig
        qr = q_ref[0, :, sl].astype(F32)
        q = qr / (1.0 + jnp.exp(-qr))
        v = i_ref[0, :, sl]

        g1, g2, g3 = _split3(logf)
        b = _dot(tril, g1) + _dot(tril, g2) + _dot(tril, g3)

        scores = jnp.zeros((t, t), F32)
        for level in range(1, n_levels + 1):
            group = 1 << level
            d = b - _midpoint_rows(b, group, pos)
            later = (pos & (group - 1)) >= (group // 2)
            fac = jnp.exp(-jnp.abs(d))
            ql = jnp.where(later, q * fac, 0.0).astype(BF16)
            kl = jnp.where(later, 0.0, k * fac).astype(BF16)
            scores = jnp.where(lvl == level, _dot_nt(ql, kl), scores)

        st = st_ref[h]
        b_last = b[t - 1:t, :]
        o = _dot(scores.astype(BF16), v)
        o = o + _dot_nt((q * jnp.exp(b)).astype(BF16), st.astype(BF16))
        o = o + jnp.sum(q * k, axis=1, keepdims=True) * v.astype(F32)
        k_dec = (k * jnp.exp(b_last - b)).astype(BF16)
        st_ref[h] = jnp.exp(b_last) * st + _dot_tn(v, k_dec)

        gr = g_ref[0, :, sl].astype(F32)
        y = _rms(o, ng_ref[:, sl]) * (gr / (1.0 + jnp.exp(-gr)))
        o_ref[0, :, sl] = y.astype(o_ref.dtype)


def _pair_levels(t):
    idx = np.arange(t)
    x = idx[:, None] ^ idx[None, :]
    lev = np.where(x > 0, np.floor(np.log2(np.maximum(x, 1))).astype(np.int64) + 1, 0)
    return np.where(idx[:, None] > idx[None, :], lev, 0).astype(np.int32)


def _hgrn2(f3, iqg3, lb_logits, ng):
    b, s, hw = f3.shape
    t = HG_BLOCK
    assert hw == HG_HEADS * HG_HEAD_DIM and s % t == 0
    lvl = jnp.asarray(_pair_levels(t))
    blk = lambda c: pl.BlockSpec((1, t, hw), lambda bi, i: (bi, i, c))
    block_bytes = t * hw * (4 + 4 * 2)
    resident_bytes = t * t * 4 + 3 * hw * 4
    return pl.pallas_call(
        _hgrn2_kernel,
        grid=(b, s // t),
        in_specs=[blk(0), blk(0), blk(1), blk(2),
                  _resident(lb_logits.shape), _resident(ng.shape), _resident((t, t))],
        out_specs=blk(0),
        out_shape=jax.ShapeDtypeStruct((b, s, hw), BF16),
        scratch_shapes=[pltpu.VMEM((HG_HEADS, HG_HEAD_DIM, HG_HEAD_DIM), F32)],
        compiler_params=pltpu.CompilerParams(
            dimension_semantics=("parallel", "arbitrary"),
            vmem_limit_bytes=_vmem_limit(block_bytes, resident_bytes, 64 * t * t * 4)),
        name="hgrn2",
    )(f3, iqg3, iqg3, iqg3, lb_logits, ng, lvl)


def _merge_ffn_kernel(x_ref, osb_ref, ohg_ref, gate_ref, bg_ref, wsb_ref, whg_ref, wout_ref,
                      g2_ref, w1_ref, w2_ref, gf_ref, o_ref):
    d = x_ref.shape[1]
    gates = 1.0 / (1.0 + jnp.exp(-(gate_ref[...] + bg_ref[...])))
    merged = gates[:, :d] * _dot(osb_ref[...], wsb_ref[...]) \
        + gates[:, d:] * _dot(ohg_ref[...], whg_ref[...])
    h = x_ref[...] + _dot(merged.astype(BF16), wout_ref[...])
    hn = _rms(h, g2_ref[...]).astype(BF16)
    a = jnp.maximum(_dot(hn, w1_ref[...]), 0.0)
    h = h + _dot((a * a).astype(BF16), w2_ref[...])
    o_ref[...] = _rms(h, gf_ref[...]).astype(o_ref.dtype)


def _merge_ffn(x2, osb, ohg, gates, bg, wsb, whg, wout, g2, w1, w2, gf):
    n, d = x2.shape
    tm = TOKEN_TILE
    dff = w1.shape[1]
    row = lambda i: (i, 0)
    full = lambda a: _resident(a.shape)
    resident_bytes = sum(a.size * a.dtype.itemsize for a in (bg, wsb, whg, wout, g2, w1, w2, gf))
    block_bytes = tm * (d * 4 + osb.shape[1] * 2 + ohg.shape[1] * 2 + 2 * d * 4 + d * 4)
    return pl.pallas_call(
        _merge_ffn_kernel,
        grid=(n // tm,),
        in_specs=[pl.BlockSpec((tm, d), row),
                  pl.BlockSpec((tm, osb.shape[1]), row),
                  pl.BlockSpec((tm, ohg.shape[1]), row),
                  pl.BlockSpec((tm, 2 * d), row),
                  full(bg), full(wsb), full(whg), full(wout), full(g2), full(w1), full(w2),
                  full(gf)],
        out_specs=pl.BlockSpec((tm, d), row),
        out_shape=jax.ShapeDtypeStruct((n, d), x2.dtype),
        compiler_params=pltpu.CompilerParams(
            dimension_semantics=("parallel",),
            vmem_limit_bytes=_vmem_limit(block_bytes, resident_bytes, tm * (dff * 6 + d * 24))),
        name="merge_ffn",
    )(x2, osb, ohg, gates, bg, wsb, whg, wout, g2, w1, w2, gf)


def kernel(x, norm1_g, w_in, b_gate, lb_logits, hg_norm_g, w_o_sb, w_o_hg, w_out, norm2_g,
           w_ff1, w_ff2, final_g):
    b, s, d = x.shape
    assert w_in.shape[0] == 1, "single-layer block"
    sb_w = SB_HEADS * SB_HEAD_DIM
    hg_w = HG_HEADS * HG_HEAD_DIM
    x2 = x.reshape(b * s, d)
    w_in_bf = w_in[0].astype(BF16)
    w_vt = w_in_bf[:, 2 * sb_w:3 * sb_w].T
    q, k, vt, f_raw, iqg, gates = _in_proj(x2, norm1_g, w_in_bf, w_vt, b, sb_w, hg_w, d)
    o_sb = _sb_attn(q.reshape(b, s, sb_w), k.reshape(b, s, sb_w), vt)
    o_hg = _hgrn2(f_raw.reshape(b, s, hg_w), iqg.reshape(b, s, 3 * hg_w), lb_logits, hg_norm_g)
    out = _merge_ffn(x2, o_sb.reshape(b * s, sb_w), o_hg.reshape(b * s, hg_w), gates, b_gate,
                     w_o_sb[0].astype(BF16), w_o_hg[0].astype(BF16), w_out[0].astype(BF16),
                     norm2_g, w_ff1[0].astype(BF16), w_ff2[0].astype(BF16),
                     final_g.reshape(1, d))
    return out.reshape(b, s, d)
```

```python
import functools

import jax
import jax.numpy as jnp
import numpy as np
from jax import lax
from jax.experimental import pallas as pl
from jax.experimental.pallas import tpu as pltpu

F32 = jnp.float32
BF16 = jnp.bfloat16

SB_HEADS = 8
SB_HEAD_DIM = 64
HG_HEADS = 4
HG_HEAD_DIM = 128
EPS = 1e-6
LOG2E = 1.4426950408889634
SB_DEAD_CARRY = 151.0

V7X_LANES = 128
V7X_MXU_DIM = 256
V7X_VMEM_BYTES = 64 * 1024 * 1024

TOKEN_TILE = 256
ATTN_TILE = V7X_MXU_DIM
ATTN_HEADS = 8
HG_BLOCK = V7X_MXU_DIM


def _vmem_limit(pipelined_bytes, resident_bytes, temp_bytes):
    need = 2 * pipelined_bytes + resident_bytes + temp_bytes
    return int(min(need + need // 4, V7X_VMEM_BYTES - 8 * 1024 * 1024))


def _resident(shape):
    return pl.BlockSpec(shape, lambda *_: (0,) * len(shape), pipeline_mode=pl.Buffered(1))


def _rms(x, g):
    ms = jnp.mean(x * x, axis=-1, keepdims=True)
    return x * lax.rsqrt(ms + EPS) * g


def _dot(a, b):
    return jnp.dot(a, b, preferred_element_type=F32)


def _dot_nt(a, b):
    return lax.dot_general(a, b, (((1,), (1,)), ((), ())), preferred_element_type=F32)


def _dot_tn(a, b):
    return lax.dot_general(a, b, (((0,), (0,)), ((), ())), preferred_element_type=F32)


def _in_proj_kernel(x_ref, g_ref, w_ref, wvt_ref, q_ref, k_ref, vt_ref, f_ref, iqg_ref,
                    gate_ref):
    xn = _rms(x_ref[...], g_ref[...]).astype(BF16)
    sb_w = q_ref.shape[1]
    c0 = 3 * sb_w
    c1 = c0 + f_ref.shape[1]
    c2 = c1 + iqg_ref.shape[1]
    c3 = c2 + gate_ref.shape[1]
    q_ref[...] = (_dot(xn, w_ref[:, 0:sb_w]) * (SB_HEAD_DIM ** -0.5 * LOG2E)).astype(q_ref.dtype)
    k_ref[...] = _dot(xn, w_ref[:, sb_w:2 * sb_w]).astype(k_ref.dtype)
    vt_ref[0] = _dot_nt(wvt_ref[...], xn).astype(vt_ref.dtype)
    f_ref[...] = _dot(xn, w_ref[:, c0:c1])
    iqg_ref[...] = _dot(xn, w_ref[:, c1:c2]).astype(iqg_ref.dtype)
    gate_ref[...] = _dot(xn, w_ref[:, c2:c3])


def _in_proj(x2, g1, w_in, w_vt, batch, sb_w, hg_w, d):
    n = x2.shape[0]
    tm = TOKEN_TILE
    cols = w_in.shape[1]
    tiles_per_seq = n // batch // tm
    widths = (sb_w, sb_w, hg_w, 3 * hg_w, 2 * d)
    dtypes = (BF16, BF16, F32, BF16, F32)
    row = lambda i: (i, 0)
    rows = lambda w: pl.BlockSpec((tm, w), row)
    vt_spec = pl.BlockSpec((1, sb_w, tm), lambda i: (i // tiles_per_seq, 0, i % tiles_per_seq))
    sds = lambda w, t: jax.ShapeDtypeStruct((n, w), t)
    out_bytes = sum(tm * w * jnp.dtype(t).itemsize for w, t in zip(widths, dtypes)) \
        + tm * sb_w * 2
    return pl.pallas_call(
        _in_proj_kernel,
        grid=(n // tm,),
        in_specs=[rows(d), _resident((1, d)), _resident((d, cols)), _resident((sb_w, d))],
        out_specs=[rows(sb_w), rows(sb_w), vt_spec, rows(hg_w), rows(3 * hg_w), rows(2 * d)],
        out_shape=[sds(sb_w, BF16), sds(sb_w, BF16),
                   jax.ShapeDtypeStruct((batch, sb_w, n // batch), BF16),
                   sds(hg_w, F32), sds(3 * hg_w, BF16), sds(2 * d, F32)],
        compiler_params=pltpu.CompilerParams(
            dimension_semantics=("parallel",),
            vmem_limit_bytes=_vmem_limit(tm * d * 4 + out_bytes,
                                         d * 4 + d * cols * 2 + sb_w * d * 2,
                                         tm * (d * 6 + cols * 4))),
        name="in_proj",
    )(x2, g1, w_in, w_vt)


def _softplus2(z, mask):
    sp = jnp.maximum(z, 0.0) + jnp.log(1.0 + jnp.exp2(-jnp.abs(z))) * LOG2E
    return sp if mask is None else jnp.where(mask, sp, 0.0)


def _sb_attn_kernel(q_ref, k_ref, vt_ref, o_ref, qm_ref, acc_ref, carry_ref):
    t = q_ref.shape[1]
    n_heads = q_ref.shape[2] // SB_HEAD_DIM
    heads = range(n_heads)
    qi = pl.program_id(2)
    row = lax.broadcasted_iota(jnp.int32, (t, t), 0)
    col = lax.broadcasted_iota(jnp.int32, (t, t), 1)
    tri = (col > row).astype(BF16)
    causal = row < col
    lane = lax.broadcasted_iota(jnp.int32, (t, V7X_LANES), 1)
    zero = jnp.zeros((), BF16)
    q = q_ref[0]
    for h in heads:
        grp = q[:, (h // 2) * V7X_LANES:(h // 2 + 1) * V7X_LANES]
        qm_ref[h] = jnp.where((lane // SB_HEAD_DIM) == (h % 2), grp, zero)

    def logits(j):
        start = pl.multiple_of(j * t, t)
        k = k_ref[0, pl.ds(start, t), :]
        return [_dot_nt(k[:, (h // 2) * V7X_LANES:(h // 2 + 1) * V7X_LANES], qm_ref[h])
                for h in heads]

    def values_t(j, h):
        return vt_ref[0, h * SB_HEAD_DIM:(h + 1) * SB_HEAD_DIM, pl.ds(pl.multiple_of(j * t, t), t)]

    def tile_products(j, mask):
        z = logits(j)
        sp = [_softplus2(z[h], mask) for h in heads]
        spb = [sp[h].astype(BF16) for h in heads]
        later = [_dot(tri, spb[h]) for h in heads]
        w = [jnp.exp2(z[h] - sp[h] - later[h]) for h in heads]
        if mask is not None:
            w = [jnp.where(mask, w[h], 0.0) for h in heads]
        pv = [_dot(values_t(j, h), w[h].astype(BF16)) for h in heads]
        total = [later[h][0:1, :] + spb[h][0:1, :].astype(F32) for h in heads]
        return pv, total

    has_prev = qi > 0
    pv0, tot0 = tile_products(qi, causal)
    pv1, tot1 = tile_products(jnp.maximum(qi - 1, 0), None)
    for h in heads:
        scale = jnp.where(has_prev, jnp.exp2(-tot0[h]), 0.0)
        acc_ref[h] = pv0[h] + scale * pv1[h]
        carry_ref[h:h + 1, :] = tot0[h] + jnp.where(has_prev, tot1[h], 0.0)

    def more(state):
        i, live = state
        return jnp.logical_and(i < qi, live)

    def body(state):
        i, _ = state
        pv, tot = tile_products(qi - 1 - i, None)
        for h in heads:
            c = carry_ref[h:h + 1, :]
            acc_ref[h] += jnp.exp2(-c) * pv[h]
            carry_ref[h:h + 1, :] = c + tot[h]
        return i + 1, jnp.min(carry_ref[...]) < SB_DEAD_CARRY

    lax.while_loop(more, body, (jnp.int32(1), jnp.min(carry_ref[...]) < SB_DEAD_CARRY))
    o_ref[0] = acc_ref[...].reshape(n_heads * SB_HEAD_DIM, t).T.astype(o_ref.dtype)


def _sb_attn(q3, k3, vt3):
    b, s, sb_w = q3.shape
    t = ATTN_TILE
    w = ATTN_HEADS * SB_HEAD_DIM
    groups = sb_w // w
    assert w % V7X_LANES == 0 and s % t == 0
    block_bytes = (2 * t * w + 2 * s * w) * 2
    return pl.pallas_call(
        _sb_attn_kernel,
        grid=(b, groups, s // t),
        in_specs=[pl.BlockSpec((1, t, w), lambda bi, p, i: (bi, i, p)),
                  pl.BlockSpec((1, s, w), lambda bi, p, i: (bi, 0, p)),
                  pl.BlockSpec((1, w, s), lambda bi, p, i: (bi, p, 0))],
        out_specs=pl.BlockSpec((1, t, w), lambda bi, p, i: (bi, i, p)),
        out_shape=jax.ShapeDtypeStruct((b, s, sb_w), BF16),
        scratch_shapes=[pltpu.VMEM((ATTN_HEADS, t, V7X_LANES), BF16),
                        pltpu.VMEM((ATTN_HEADS, SB_HEAD_DIM, t), F32),
                        pltpu.VMEM((ATTN_HEADS, t), F32)],
        compiler_params=pltpu.CompilerParams(
            dimension_semantics=("parallel", "parallel", "arbitrary"),
            vmem_limit_bytes=_vmem_limit(block_bytes, 0, 24 * ATTN_HEADS * t * t * 4)),
        name="sb_attn",
    )(q3, k3, vt3)


def _split3(x):
    a = x.astype(BF16)
    r = x - a.astype(F32)
    b = r.astype(BF16)
    c = (r - b.astype(F32)).astype(BF16)
    return a, b, c


def _rows_from_group(b, group, r):
    n, c = b.shape
    if group == n:
        return jnp.broadcast_to(b[r:r + 1, :], (n, c))
    b3 = b.reshape(n // group, group, c)
    return jnp.broadcast_to(b3[:, r:r + 1, :], b3.shape).reshape(n, c)


def _midpoint_rows(b, group, pos):
    n = b.shape[0]
    half = group // 2
    if group >= 16:
        return _rows_from_group(b, group, half - 1)
    if group == 8:
        return _rows_from_group(b, 8, 3)
    up1 = pltpu.roll(b, n - 1, 0)
    dn1 = pltpu.roll(b, 1, 0)
    if group == 2:
        return jnp.where((pos & 1) == 0, b, dn1)
    assert group == 4
    dn2 = pltpu.roll(b, 2, 0)
    r4 = pos & 3
    return jnp.where(r4 == 0, up1, jnp.where(r4 == 1, b, jnp.where(r4 == 2, dn1, dn2)))


def _hgrn2_kernel(f_ref, i_ref, q_ref, g_ref, lbl_ref, ng_ref, lvl_ref, o_ref, st_ref):
    t = f_ref.shape[1]
    dk = HG_HEAD_DIM

    @pl.when(pl.program_id(1) == 0)
    def _():
        st_ref[...] = jnp.zeros_like(st_ref)

    row = lax.broadcasted_iota(jnp.int32, (t, t), 0)
    col = lax.broadcasted_iota(jnp.int32, (t, t), 1)
    tril = (row >= col).astype(BF16)
    pos = lax.broadcasted_iota(jnp.int32, (t, dk), 0)
    lvl = lvl_ref[...]
    n_levels = t.bit_length() - 1

    lbl = lbl_ref[...]
    ex = jnp.exp(lbl - jnp.max(lbl, axis=0, keepdims=True))
    lb_all = ex[0:1, :] / jnp.sum(ex, axis=0, keepdims=True)

    for h in range(HG_HEADS):
        sl = slice(h * dk, (h + 1) * dk)
        lb = lb_all[:, sl]
        fr = f_ref[0, :, sl]
        e = jnp.exp(-jnp.abs(fr))
        r = 1.0 / (1.0 + e)
        er = e * r
        sig = jnp.where(fr >= 0, r, er)
        nsig = jnp.where(fr >= 0, er, r)
        logf = jnp.log(lb + (1.0 - lb) * sig)
        k = (1.0 - lb) * nsig
        qr = q_ref[0, :, sl].astype(F32)
        q = qr / (1.0 + jnp.exp(-qr))
        v = i_ref[0, :, sl]

        g1, g2, g3 = _split3(logf)
        b = _dot(tril, g1) + _dot(tril, g2) + _dot(tril, g3)

        scores = jnp.zeros((t, t), F32)
        for level in range(1, n_levels + 1):
            group = 1 << level
            d = b - _midpoint_rows(b, group, pos)
            later = (pos & (group - 1)) >= (group // 2)
            fac = jnp.exp(-jnp.abs(d))
            ql = jnp.where(later, q * fac, 0.0).astype(BF16)
            kl = jnp.where(later, 0.0, k * fac).astype(BF16)
            scores = jnp.where(lvl == level, _dot_nt(ql, kl), scores)

        st = st_ref[h]
        b_last = b[t - 1:t, :]
        o = _dot(scores.astype(BF16), v)
        o = o + _dot_nt((q * jnp.exp(b)).astype(BF16), st.astype(BF16))
        o = o + jnp.sum(q * k, axis=1, keepdims=True) * v.astype(F32)
        k_dec = (k * jnp.exp(b_last - b)).astype(BF16)
        st_ref[h] = jnp.exp(b_last) * st + _dot_tn(v, k_dec)

        gr = g_ref[0, :, sl].astype(F32)
        y = _rms(o, ng_ref[:, sl]) * (gr / (1.0 + jnp.exp(-gr)))
        o_ref[0, :, sl] = y.astype(o_ref.dtype)


def _pair_levels(t):
    idx = np.arange(t)
    x = idx[:, None] ^ idx[None, :]
    lev = np.where(x > 0, np.floor(np.log2(np.maximum(x, 1))).astype(np.int64) + 1, 0)
    return np.where(idx[:, None] > idx[None, :], lev, 0).astype(np.int32)


def _hgrn2(f3, iqg3, lb_logits, ng):
    b, s, hw = f3.shape
    t = HG_BLOCK
    assert hw == HG_HEADS * HG_HEAD_DIM and s % t == 0
    lvl = jnp.asarray(_pair_levels(t))
    blk = lambda c: pl.BlockSpec((1, t, hw), lambda bi, i: (bi, i, c))
    block_bytes = t * hw * (4 + 4 * 2)
    resident_bytes = t * t * 4 + 3 * hw * 4
    return pl.pallas_call(
        _hgrn2_kernel,
        grid=(b, s // t),
        in_specs=[blk(0), blk(0), blk(1), blk(2),
                  _resident(lb_logits.shape), _resident(ng.shape), _resident((t, t))],
        out_specs=blk(0),
        out_shape=jax.ShapeDtypeStruct((b, s, hw), BF16),
        scratch_shapes=[pltpu.VMEM((HG_HEADS, HG_HEAD_DIM, HG_HEAD_DIM), F32)],
        compiler_params=pltpu.CompilerParams(
            dimension_semantics=("parallel", "arbitrary"),
            vmem_limit_bytes=_vmem_limit(block_bytes, resident_bytes, 64 * t * t * 4)),
        name="hgrn2",
    )(f3, iqg3, iqg3, iqg3, lb_logits, ng, lvl)


def _merge_ffn_kernel(x_ref, osb_ref, ohg_ref, gate_ref, bg_ref, wsb_ref, whg_ref, wout_ref,
                      g2_ref, w1_ref, w2_ref, gf_ref, o_ref):
    d = x_ref.shape[1]
    gates = 1.0 / (1.0 + jnp.exp(-(gate_ref[...] + bg_ref[...])))
    merged = gates[:, :d] * _dot(osb_ref[...], wsb_ref[...]) \
        + gates[:, d:] * _dot(ohg_ref[...], whg_ref[...])
    h = x_ref[...] + _dot(merged.astype(BF16), wout_ref[...])
    hn = _rms(h, g2_ref[...]).astype(BF16)
    a = jnp.maximum(_dot(hn, w1_ref[...]), 0.0)
    h = h + _dot((a * a).astype(BF16), w2_ref[...])
    o_ref[...] = _rms(h, gf_ref[...]).astype(o_ref.dtype)


def _merge_ffn(x2, osb, ohg, gates, bg, wsb, whg, wout, g2, w1, w2, gf):
    n, d = x2.shape
    tm = TOKEN_TILE
    dff = w1.shape[1]
    row = lambda i: (i, 0)
    full = lambda a: _resident(a.shape)
    resident_bytes = sum(a.size * a.dtype.itemsize for a in (bg, wsb, whg, wout, g2, w1, w2, gf))
    block_bytes = tm * (d * 4 + osb.shape[1] * 2 + ohg.shape[1] * 2 + 2 * d * 4 + d * 4)
    return pl.pallas_call(
        _merge_ffn_kernel,
        grid=(n // tm,),
        in_specs=[pl.BlockSpec((tm, d), row),
                  pl.BlockSpec((tm, osb.shape[1]), row),
                  pl.BlockSpec((tm, ohg.shape[1]), row),
                  pl.BlockSpec((tm, 2 * d), row),
                  full(bg), full(wsb), full(whg), full(wout), full(g2), full(w1), full(w2),
                  full(gf)],
        out_specs=pl.BlockSpec((tm, d), row),
        out_shape=jax.ShapeDtypeStruct((n, d), x2.dtype),
        compiler_params=pltpu.CompilerParams(
            dimension_semantics=("parallel",),
            vmem_limit_bytes=_vmem_limit(block_bytes, resident_bytes, tm * (dff * 6 + d * 24))),
        name="merge_ffn",
    )(x2, osb, ohg, gates, bg, wsb, whg, wout, g2, w1, w2, gf)


def kernel(x, norm1_g, w_in, b_gate, lb_logits, hg_norm_g, w_o_sb, w_o_hg, w_out, norm2_g,
           w_ff1, w_ff2, final_g):
    b, s, d = x.shape
    assert w_in.shape[0] == 1, "single-layer block"
    sb_w = SB_HEADS * SB_HEAD_DIM
    hg_w = HG_HEADS * HG_HEAD_DIM
    x2 = x.reshape(b * s, d)
    w_in_bf = w_in[0].astype(BF16)
    w_vt = w_in_bf[:, 2 * sb_w:3 * sb_w].T
    q, k, vt, f_raw, iqg, gates = _in_proj(x2, norm1_g, w_in_bf, w_vt, b, sb_w, hg_w, d)
    o_sb = _sb_attn(q.reshape(b, s, sb_w), k.reshape(b, s, sb_w), vt)
    o_hg = _hgrn2(f_raw.reshape(b, s, hg_w), iqg.reshape(b, s, 3 * hg_w), lb_logits, hg_norm_g)
    out = _merge_ffn(x2, o_sb.reshape(b * s, sb_w), o_hg.reshape(b * s, hg_w), gates, b_gate,
                     w_o_sb[0].astype(BF16), w_o_hg[0].astype(BF16), w_out[0].astype(BF16),
                     norm2_g, w_ff1[0].astype(BF16), w_ff2[0].astype(BF16),
                     final_g.reshape(1, d))
    return out.reshape(b, s, d)
```

```python
import functools

import jax
import jax.numpy as jnp
import numpy as np
from jax import lax
from jax.experimental import pallas as pl
from jax.experimental.pallas import tpu as pltpu

F32 = jnp.float32
BF16 = jnp.bfloat16

SB_HEADS = 8
SB_HEAD_DIM = 64
HG_HEADS = 4
HG_HEAD_DIM = 128
EPS = 1e-6
LOG2E = 1.4426950408889634
SB_DEAD_CARRY = 151.0

V7X_LANES = 128
V7X_MXU_DIM = 256
V7X_VMEM_BYTES = 64 * 1024 * 1024

TOKEN_TILE = 256
ATTN_TILE = V7X_MXU_DIM
ATTN_HEADS = 8
HG_BLOCK = V7X_MXU_DIM


def _vmem_limit(pipelined_bytes, resident_bytes, temp_bytes):
    need = 2 * pipelined_bytes + resident_bytes + temp_bytes
    return int(min(need + need // 4, V7X_VMEM_BYTES - 8 * 1024 * 1024))


def _resident(shape):
    return pl.BlockSpec(shape, lambda *_: (0,) * len(shape), pipeline_mode=pl.Buffered(1))


def _rms(x, g):
    ms = jnp.mean(x * x, axis=-1, keepdims=True)
    return x * lax.rsqrt(ms + EPS) * g


def _dot(a, b):
    return jnp.dot(a, b, preferred_element_type=F32)


def _dot_nt(a, b):
    return lax.dot_general(a, b, (((1,), (1,)), ((), ())), preferred_element_type=F32)


def _dot_tn(a, b):
    return lax.dot_general(a, b, (((0,), (0,)), ((), ())), preferred_element_type=F32)


def _softplus2(z, mask):
    sp = jnp.maximum(z, 0.0) + jnp.log(1.0 + jnp.exp2(-jnp.abs(z))) * LOG2E
    return sp if mask is None else jnp.where(mask, sp, 0.0)


def _sb_attn_kernel(q_ref, k_ref, vt_ref, o_ref, qm_ref, acc_ref, carry_ref):
    t = q_ref.shape[1]
    n_heads = q_ref.shape[2] // SB_HEAD_DIM
    heads = range(n_heads)
    qi = pl.program_id(2)
    row = lax.broadcasted_iota(jnp.int32, (t, t), 0)
    col = lax.broadcasted_iota(jnp.int32, (t, t), 1)
    tri = (col > row).astype(BF16)
    causal = row < col
    lane = lax.broadcasted_iota(jnp.int32, (t, V7X_LANES), 1)
    zero = jnp.zeros((), BF16)
    q = q_ref[0]
    for h in heads:
        grp = q[:, (h // 2) * V7X_LANES:(h // 2 + 1) * V7X_LANES]
        qm_ref[h] = jnp.where((lane // SB_HEAD_DIM) == (h % 2), grp, zero)

    def logits(j):
        start = pl.multiple_of(j * t, t)
        k = k_ref[0, pl.ds(start, t), :]
        return [_dot_nt(k[:, (h // 2) * V7X_LANES:(h // 2 + 1) * V7X_LANES], qm_ref[h])
                for h in heads]

    def values_t(j, h):
        return vt_ref[0, h * SB_HEAD_DIM:(h + 1) * SB_HEAD_DIM, pl.ds(pl.multiple_of(j * t, t), t)]

    def tile_products(j, mask):
        z = logits(j)
        sp = [_softplus2(z[h], mask) for h in heads]
        spb = [sp[h].astype(BF16) for h in heads]
        later = [_dot(tri, spb[h]) for h in heads]
        w = [jnp.exp2(z[h] - sp[h] - later[h]) for h in heads]
        if mask is not None:
            w = [jnp.where(mask, w[h], 0.0) for h in heads]
        pv = [_dot(values_t(j, h), w[h].astype(BF16)) for h in heads]
        total = [later[h][0:1, :] + spb[h][0:1, :].astype(F32) for h in heads]
        return pv, total

    has_prev = qi > 0
    pv0, tot0 = tile_products(qi, causal)
    pv1, tot1 = tile_products(jnp.maximum(qi - 1, 0), None)
    for h in heads:
        scale = jnp.where(has_prev, jnp.exp2(-tot0[h]), 0.0)
        acc_ref[h] = pv0[h] + scale * pv1[h]
        carry_ref[h:h + 1, :] = tot0[h] + jnp.where(has_prev, tot1[h], 0.0)

    def more(state):
        i, live = state
        return jnp.logical_and(i < qi, live)

    def body(state):
        i, _ = state
        pv, tot = tile_products(qi - 1 - i, None)
        for h in heads:
            c = carry_ref[h:h + 1, :]
            acc_ref[h] += jnp.exp2(-c) * pv[h]
            carry_ref[h:h + 1, :] = c + tot[h]
        return i + 1, jnp.min(carry_ref[...]) < SB_DEAD_CARRY

    lax.while_loop(more, body, (jnp.int32(1), jnp.min(carry_ref[...]) < SB_DEAD_CARRY))
    o_ref[0] = acc_ref[...].reshape(n_heads * SB_HEAD_DIM, t).T.astype(o_ref.dtype)


def _sb_attn(q3, k3, vt3):
    b, s, sb_w = q3.shape
    t = ATTN_TILE
    w = ATTN_HEADS * SB_HEAD_DIM
    groups = sb_w // w
    assert w % V7X_LANES == 0 and s % t == 0
    block_bytes = (2 * t * w + 2 * s * w) * 2
    return pl.pallas_call(
        _sb_attn_kernel,
        grid=(b, groups, s // t),
        in_specs=[pl.BlockSpec((1, t, w), lambda bi, p, i: (bi, i, p)),
                  pl.BlockSpec((1, s, w), lambda bi, p, i: (bi, 0, p)),
                  pl.BlockSpec((1, w, s), lambda bi, p, i: (bi, p, 0))],
        out_specs=pl.BlockSpec((1, t, w), lambda bi, p, i: (bi, i, p)),
        out_shape=jax.ShapeDtypeStruct((b, s, sb_w), BF16),
        scratch_shapes=[pltpu.VMEM((ATTN_HEADS, t, V7X_LANES), BF16),
                        pltpu.VMEM((ATTN_HEADS, SB_HEAD_DIM, t), F32),
                        pltpu.VMEM((ATTN_HEADS, t), F32)],
        compiler_params=pltpu.CompilerParams(
            dimension_semantics=("parallel", "parallel", "arbitrary"),
            vmem_limit_bytes=_vmem_limit(block_bytes, 0, 24 * ATTN_HEADS * t * t * 4)),
        name="sb_attn",
    )(q3, k3, vt3)


def _split3(x):
    a = x.astype(BF16)
    r = x - a.astype(F32)
    b = r.astype(BF16)
    c = (r - b.astype(F32)).astype(BF16)
    return a, b, c


def _rows_from_group(b, group, r):
    n, c = b.shape
    if group == n:
        return jnp.broadcast_to(b[r:r + 1, :], (n, c))
    b3 = b.reshape(n // group, group, c)
    return jnp.broadcast_to(b3[:, r:r + 1, :], b3.shape).reshape(n, c)


def _midpoint_rows(b, group, pos):
    n = b.shape[0]
    half = group // 2
    if group >= 16:
        return _rows_from_group(b, group, half - 1)
    if group == 8:
        return _rows_from_group(b, 8, 3)
    up1 = pltpu.roll(b, n - 1, 0)
    dn1 = pltpu.roll(b, 1, 0)
    if group == 2:
        return jnp.where((pos & 1) == 0, b, dn1)
    assert group == 4
    dn2 = pltpu.roll(b, 2, 0)
    r4 = pos & 3
    return jnp.where(r4 == 0, up1, jnp.where(r4 == 1, b, jnp.where(r4 == 2, dn1, dn2)))


def _hgrn2_gates(fr, qr, lb):
    e = jnp.exp(-jnp.abs(fr))
    r = 1.0 / (1.0 + e)
    er = e * r
    sig = jnp.where(fr >= 0, r, er)
    nsig = jnp.where(fr >= 0, er, r)
    logf2 = jnp.log(lb + (1.0 - lb) * sig) * LOG2E
    k = (1.0 - lb) * nsig
    q = qr / (1.0 + jnp.exp(-qr))
    return logf2, k, q


def _hgrn2_head(b, q, k, v, gr, ng, lvl, pos, st_ref, h, keep):
    t = b.shape[0]
    n_levels = t.bit_length() - 1
    scores = jnp.zeros((t, t), F32)
    for level in range(1, n_levels + 1):
        group = 1 << level
        d = b - _midpoint_rows(b, group, pos)
        later = (pos & (group - 1)) >= (group // 2)
        fac = jnp.exp2(-jnp.abs(d))
        ql = jnp.where(later, q * fac, 0.0).astype(BF16)
        kl = jnp.where(later, 0.0, k * fac).astype(BF16)
        scores = jnp.where(lvl == level, _dot_nt(ql, kl), scores)

    st = st_ref[h] * keep
    b_last = b[t - 1:t, :]
    o = _dot(scores.astype(BF16), v)
    o = o + _dot_nt((q * jnp.exp2(b)).astype(BF16), st.astype(BF16))
    o = o + jnp.sum(q * k, axis=1, keepdims=True) * v.astype(F32)
    k_dec = (k * jnp.exp2(b_last - b)).astype(BF16)
    st_ref[h] = jnp.exp2(b_last) * st + _dot_tn(v, k_dec)
    return _rms(o, ng) * (gr / (1.0 + jnp.exp(-gr)))


def _proj_hgrn2_kernel(x_ref, g_ref, w_ref, wvt_ref, lbl_ref, ng_ref, lvl_ref,
                       q_ref, k_ref, vt_ref, gate_ref, ohg_ref, f_buf, iqg_buf, st_ref,
                       *, tiles_per_seq):
    t = x_ref.shape[0]
    d = x_ref.shape[1]
    dk = HG_HEAD_DIM
    hg_w = HG_HEADS * dk
    sb_w = q_ref.shape[1]
    i = pl.program_id(0)
    slot = i % 2
    prev = 1 - slot

    @pl.when(i == 0)
    def _():
        f_buf[1] = jnp.zeros(f_buf.shape[1:], f_buf.dtype)
        iqg_buf[1] = jnp.zeros(iqg_buf.shape[1:], iqg_buf.dtype)
        st_ref[...] = jnp.zeros_like(st_ref)

    xn = _rms(x_ref[...], g_ref[...]).astype(BF16)

    row = lax.broadcasted_iota(jnp.int32, (t, t), 0)
    col = lax.broadcasted_iota(jnp.int32, (t, t), 1)
    tril = (row >= col).astype(BF16)
    pos = lax.broadcasted_iota(jnp.int32, (t, dk), 0)
    lvl = lvl_ref[...]
    keep = jnp.where(i % tiles_per_seq == 1, 0.0, 1.0)

    lbl = lbl_ref[...]
    ex = jnp.exp(lbl - jnp.max(lbl, axis=0, keepdims=True))
    lb_all = ex[0:1, :] / jnp.sum(ex, axis=0, keepdims=True)

    def project(lo, hi):
        return _dot(xn, w_ref[:, lo:hi])

    def finish_head(h):
        sl = slice(h * dk, (h + 1) * dk)
        y = _hgrn2_head(b_all[:, sl], q_all[:, sl], k_all[:, sl], iqg_buf[prev, :, sl],
                        iqg_buf[prev, :, 2 * hg_w + h * dk:2 * hg_w + (h + 1) * dk].astype(F32),
                        ng_ref[:, sl], lvl, pos, st_ref, h, keep)
        ohg_ref[:, sl] = y.astype(ohg_ref.dtype)

    c0 = 3 * sb_w
    c1 = c0 + hg_w
    c2 = c1 + 3 * hg_w
    q_ref[...] = (project(0, sb_w) * (SB_HEAD_DIM ** -0.5 * LOG2E)).astype(q_ref.dtype)
    k_ref[...] = project(sb_w, 2 * sb_w).astype(k_ref.dtype)
    logf2, k_all, q_all = _hgrn2_gates(f_buf[prev], iqg_buf[prev, :, hg_w:2 * hg_w].astype(F32),
                                       lb_all)
    g1, g2, g3 = _split3(logf2)
    b_all = _dot(tril, g1) + _dot(tril, g2) + _dot(tril, g3)
    vt_ref[0] = _dot_nt(wvt_ref[...], xn).astype(vt_ref.dtype)
    f_buf[slot] = project(c0, c1)
    finish_head(0)
    iqg_buf[slot] = project(c1, c2).astype(iqg_buf.dtype)
    finish_head(1)
    gate_ref[:, 0:d] = project(c2, c2 + d)
    finish_head(2)
    gate_ref[:, d:d + d // 2] = project(c2 + d, c2 + d + d // 2)
    finish_head(3)
    gate_ref[:, d + d // 2:2 * d] = project(c2 + d + d // 2, c2 + 2 * d)


def _pair_levels(t):
    idx = np.arange(t)
    x = idx[:, None] ^ idx[None, :]
    lev = np.where(x > 0, np.floor(np.log2(np.maximum(x, 1))).astype(np.int64) + 1, 0)
    return np.where(idx[:, None] > idx[None, :], lev, 0).astype(np.int32)


def _proj_hgrn2(x2, g1, w_in, w_vt, lb_logits, ng, batch, sb_w, hg_w):
    n, d = x2.shape
    t = HG_BLOCK
    cols = w_in.shape[1]
    n_tiles = n // t
    tiles_per_seq = n_tiles // batch
    assert hg_w == HG_HEADS * HG_HEAD_DIM and n_tiles * t == n and tiles_per_seq * batch == n_tiles
    lvl = jnp.asarray(_pair_levels(t))
    cur = lambda i: jnp.minimum(i, n_tiles - 1)
    rows = lambda w: pl.BlockSpec((t, w), lambda i: (cur(i), 0))
    vt_spec = pl.BlockSpec((1, sb_w, t),
                           lambda i: (cur(i) // tiles_per_seq, 0, cur(i) % tiles_per_seq))
    sds = lambda w, dt: jax.ShapeDtypeStruct((n, w), dt)
    moving = t * (d * 4 + 2 * sb_w * 2 + sb_w * 2 + 2 * d * 4 + hg_w * 2)
    resident = d * 4 + d * cols * 2 + sb_w * d * 2 + 3 * hg_w * 4 + t * t * 4 \
        + 2 * t * hg_w * (4 + 3 * 2) + HG_HEADS * HG_HEAD_DIM * HG_HEAD_DIM * 4
    return pl.pallas_call(
        functools.partial(_proj_hgrn2_kernel, tiles_per_seq=tiles_per_seq),
        grid=(n_tiles + 1,),
        in_specs=[rows(d), _resident((1, d)), _resident((d, cols)), _resident((sb_w, d)),
                  _resident(lb_logits.shape), _resident(ng.shape), _resident((t, t))],
        out_specs=[rows(sb_w), rows(sb_w), vt_spec, rows(2 * d),
                   pl.BlockSpec((t, hg_w), lambda i: (jnp.maximum(i - 1, 0), 0))],
        out_shape=[sds(sb_w, BF16), sds(sb_w, BF16),
                   jax.ShapeDtypeStruct((batch, sb_w, n // batch), BF16),
                   sds(2 * d, F32), sds(hg_w, BF16)],
        scratch_shapes=[pltpu.VMEM((2, t, hg_w), F32),
                        pltpu.VMEM((2, t, 3 * hg_w), BF16),
                        pltpu.VMEM((HG_HEADS, HG_HEAD_DIM, HG_HEAD_DIM), F32)],
        compiler_params=pltpu.CompilerParams(
            dimension_semantics=("arbitrary",),
            vmem_limit_bytes=_vmem_limit(moving, resident,
                                         t * (d * 6 + cols * 4) + 64 * t * t * 4)),
        name="proj_hgrn2",
    )(x2, g1, w_in, w_vt, lb_logits, ng, lvl)


def _merge_ffn_kernel(x_ref, osb_ref, ohg_ref, gate_ref, bg_ref, wsb_ref, whg_ref, wout_ref,
                      g2_ref, w1_ref, w2_ref, gf_ref, o_ref):
    d = x_ref.shape[1]
    gates = 1.0 / (1.0 + jnp.exp(-(gate_ref[...] + bg_ref[...])))
    merged = gates[:, :d] * _dot(osb_ref[...], wsb_ref[...]) \
        + gates[:, d:] * _dot(ohg_ref[...], whg_ref[...])
    h = x_ref[...] + _dot(merged.astype(BF16), wout_ref[...])
    hn = _rms(h, g2_ref[...]).astype(BF16)
    a = jnp.maximum(_dot(hn, w1_ref[...]), 0.0)
    h = h + _dot((a * a).astype(BF16), w2_ref[...])
    o_ref[...] = _rms(h, gf_ref[...]).astype(o_ref.dtype)


def _merge_ffn(x2, osb, ohg, gates, bg, wsb, whg, wout, g2, w1, w2, gf):
    n, d = x2.shape
    tm = TOKEN_TILE
    dff = w1.shape[1]
    row = lambda i: (i, 0)
    full = lambda a: _resident(a.shape)
    resident_bytes = sum(a.size * a.dtype.itemsize for a in (bg, wsb, whg, wout, g2, w1, w2, gf))
    block_bytes = tm * (d * 4 + osb.shape[1] * 2 + ohg.shape[1] * 2 + 2 * d * 4 + d * 4)
    return pl.pallas_call(
        _merge_ffn_kernel,
        grid=(n // tm,),
        in_specs=[pl.BlockSpec((tm, d), row),
                  pl.BlockSpec((tm, osb.shape[1]), row),
                  pl.BlockSpec((tm, ohg.shape[1]), row),
                  pl.BlockSpec((tm, 2 * d), row),
                  full(bg), full(wsb), full(whg), full(wout), full(g2), full(w1), full(w2),
                  full(gf)],
        out_specs=pl.BlockSpec((tm, d), row),
        out_shape=jax.ShapeDtypeStruct((n, d), x2.dtype),
        compiler_params=pltpu.CompilerParams(
            dimension_semantics=("parallel",),
            vmem_limit_bytes=_vmem_limit(block_bytes, resident_bytes, tm * (dff * 6 + d * 24))),
        name="merge_ffn",
    )(x2, osb, ohg, gates, bg, wsb, whg, wout, g2, w1, w2, gf)


def kernel(x, norm1_g, w_in, b_gate, lb_logits, hg_norm_g, w_o_sb, w_o_hg, w_out, norm2_g,
           w_ff1, w_ff2, final_g):
    b, s, d = x.shape
    assert w_in.shape[0] == 1, "single-layer block"
    sb_w = SB_HEADS * SB_HEAD_DIM
    hg_w = HG_HEADS * HG_HEAD_DIM
    x2 = x.reshape(b * s, d)
    w_in_bf = w_in[0].astype(BF16)
    w_vt = w_in[0][:, 2 * sb_w:3 * sb_w].T.astype(BF16)
    q, k, vt, gates, o_hg = _proj_hgrn2(x2, norm1_g, w_in_bf, w_vt, lb_logits, hg_norm_g, b,
                                        sb_w, hg_w)
    o_sb = _sb_attn(q.reshape(b, s, sb_w), k.reshape(b, s, sb_w), vt)
    out = _merge_ffn(x2, o_sb.reshape(b * s, sb_w), o_hg, gates, b_gate,
                     w_o_sb[0].astype(BF16), w_o_hg[0].astype(BF16), w_out[0].astype(BF16),
                     norm2_g, w_ff1[0].astype(BF16), w_ff2[0].astype(BF16),
                     final_g.reshape(1, d))
    return out.reshape(b, s, d)
```

```python
import functools

import jax
import jax.numpy as jnp
import numpy as np
from jax import lax
from jax.experimental import pallas as pl
from jax.experimental.pallas import tpu as pltpu

F32 = jnp.float32
BF16 = jnp.bfloat16

SB_HEADS = 8
SB_HEAD_DIM = 64
HG_HEADS = 4
HG_HEAD_DIM = 128
EPS = 1e-6
LOG2E = 1.4426950408889634
SB_DEAD_CARRY = 151.0

V7X_LANES = 128
V7X_MXU_DIM = 256
V7X_VMEM_BYTES = 64 * 1024 * 1024

TOKEN_TILE = 256
MERGE_STREAMS = 2
ATTN_TILE = V7X_MXU_DIM
ATTN_HEADS = 8
HG_BLOCK = V7X_MXU_DIM


def _vmem_limit(pipelined_bytes, resident_bytes, temp_bytes):
    need = 2 * pipelined_bytes + resident_bytes + temp_bytes
    return int(min(need + need // 4, V7X_VMEM_BYTES - 8 * 1024 * 1024))


def _resident(shape):
    return pl.BlockSpec(shape, lambda *_: (0,) * len(shape), pipeline_mode=pl.Buffered(1))


def _rms(x, g):
    ms = jnp.mean(x * x, axis=-1, keepdims=True)
    return x * lax.rsqrt(ms + EPS) * g


def _dot(a, b):
    return jnp.dot(a, b, preferred_element_type=F32)


def _dot_nt(a, b):
    return lax.dot_general(a, b, (((1,), (1,)), ((), ())), preferred_element_type=F32)


def _dot_tn(a, b):
    return lax.dot_general(a, b, (((0,), (0,)), ((), ())), preferred_element_type=F32)


def _softplus2(z, mask):
    sp = jnp.maximum(z, 0.0) + jnp.log(1.0 + jnp.exp2(-jnp.abs(z))) * LOG2E
    return sp if mask is None else jnp.where(mask, sp, 0.0)


def _sb_attn_kernel(q_ref, k_ref, vt_ref, o_ref, qm_ref, acc_ref, carry_ref):
    t = q_ref.shape[1]
    n_heads = q_ref.shape[2] // SB_HEAD_DIM
    heads = range(n_heads)
    qi = pl.program_id(2)
    row = lax.broadcasted_iota(jnp.int32, (t, t), 0)
    col = lax.broadcasted_iota(jnp.int32, (t, t), 1)
    tri = (col > row).astype(BF16)
    causal = row < col
    lane = lax.broadcasted_iota(jnp.int32, (t, V7X_LANES), 1)
    zero = jnp.zeros((), BF16)
    q = q_ref[0]
    for h in heads:
        grp = q[:, (h // 2) * V7X_LANES:(h // 2 + 1) * V7X_LANES]
        qm_ref[h] = jnp.where((lane // SB_HEAD_DIM) == (h % 2), grp, zero)

    def logits(j):
        start = pl.multiple_of(j * t, t)
        k = k_ref[0, pl.ds(start, t), :]
        return [_dot_nt(k[:, (h // 2) * V7X_LANES:(h // 2 + 1) * V7X_LANES], qm_ref[h])
                for h in heads]

    def values_t(j, h):
        return vt_ref[0, h * SB_HEAD_DIM:(h + 1) * SB_HEAD_DIM, pl.ds(pl.multiple_of(j * t, t), t)]

    def tile_products(j, mask):
        z = logits(j)
        sp = [_softplus2(z[h], mask) for h in heads]
        spb = [sp[h].astype(BF16) for h in heads]
        later = [_dot(tri, spb[h]) for h in heads]
        w = [jnp.exp2(z[h] - sp[h] - later[h]) for h in heads]
        if mask is not None:
            w = [jnp.where(mask, w[h], 0.0) for h in heads]
        pv = [_dot(values_t(j, h), w[h].astype(BF16)) for h in heads]
        total = [later[h][0:1, :] + spb[h][0:1, :].astype(F32) for h in heads]
        return pv, total

    has_prev = qi > 0
    pv0, tot0 = tile_products(qi, causal)
    pv1, tot1 = tile_products(jnp.maximum(qi - 1, 0), None)
    for h in heads:
        scale = jnp.where(has_prev, jnp.exp2(-tot0[h]), 0.0)
        acc_ref[h] = pv0[h] + scale * pv1[h]
        carry_ref[h:h + 1, :] = tot0[h] + jnp.where(has_prev, tot1[h], 0.0)

    def more(state):
        i, live = state
        return jnp.logical_and(i < qi, live)

    def body(state):
        i, _ = state
        pv, tot = tile_products(qi - 1 - i, None)
        for h in heads:
            c = carry_ref[h:h + 1, :]
            acc_ref[h] += jnp.exp2(-c) * pv[h]
            carry_ref[h:h + 1, :] = c + tot[h]
        return i + 1, jnp.min(carry_ref[...]) < SB_DEAD_CARRY

    lax.while_loop(more, body, (jnp.int32(1), jnp.min(carry_ref[...]) < SB_DEAD_CARRY))
    o_ref[0] = acc_ref[...].reshape(n_heads * SB_HEAD_DIM, t).T.astype(o_ref.dtype)


def _sb_attn(q3, k3, vt3):
    b, s, sb_w = q3.shape
    t = ATTN_TILE
    w = ATTN_HEADS * SB_HEAD_DIM
    groups = sb_w // w
    assert w % V7X_LANES == 0 and s % t == 0
    block_bytes = (2 * t * w + 2 * s * w) * 2
    return pl.pallas_call(
        _sb_attn_kernel,
        grid=(b, groups, s // t),
        in_specs=[pl.BlockSpec((1, t, w), lambda bi, p, i: (bi, i, p)),
                  pl.BlockSpec((1, s, w), lambda bi, p, i: (bi, 0, p)),
                  pl.BlockSpec((1, w, s), lambda bi, p, i: (bi, p, 0))],
        out_specs=pl.BlockSpec((1, t, w), lambda bi, p, i: (bi, i, p)),
        out_shape=jax.ShapeDtypeStruct((b, s, sb_w), BF16),
        scratch_shapes=[pltpu.VMEM((ATTN_HEADS, t, V7X_LANES), BF16),
                        pltpu.VMEM((ATTN_HEADS, SB_HEAD_DIM, t), F32),
                        pltpu.VMEM((ATTN_HEADS, t), F32)],
        compiler_params=pltpu.CompilerParams(
            dimension_semantics=("parallel", "parallel", "arbitrary"),
            vmem_limit_bytes=_vmem_limit(block_bytes, 0, 24 * ATTN_HEADS * t * t * 4)),
        name="sb_attn",
    )(q3, k3, vt3)


def _split3(x):
    a = x.astype(BF16)
    r = x - a.astype(F32)
    b = r.astype(BF16)
    c = (r - b.astype(F32)).astype(BF16)
    return a, b, c


def _rows_from_group(b, group, r):
    n, c = b.shape
    if group == n:
        return jnp.broadcast_to(b[r:r + 1, :], (n, c))
    b3 = b.reshape(n // group, group, c)
    return jnp.broadcast_to(b3[:, r:r + 1, :], b3.shape).reshape(n, c)


def _midpoint_rows(b, group, pos):
    n = b.shape[0]
    half = group // 2
    if group >= 16:
        return _rows_from_group(b, group, half - 1)
    if group == 8:
        return _rows_from_group(b, 8, 3)
    up1 = pltpu.roll(b, n - 1, 0)
    dn1 = pltpu.roll(b, 1, 0)
    if group == 2:
        return jnp.where((pos & 1) == 0, b, dn1)
    assert group == 4
    dn2 = pltpu.roll(b, 2, 0)
    r4 = pos & 3
    return jnp.where(r4 == 0, up1, jnp.where(r4 == 1, b, jnp.where(r4 == 2, dn1, dn2)))


def _hgrn2_gates(fr, qr, lb):
    e = jnp.exp(-jnp.abs(fr))
    r = 1.0 / (1.0 + e)
    er = e * r
    sig = jnp.where(fr >= 0, r, er)
    nsig = jnp.where(fr >= 0, er, r)
    logf2 = jnp.log(lb + (1.0 - lb) * sig) * LOG2E
    k = (1.0 - lb) * nsig
    q = qr / (1.0 + jnp.exp(-qr))
    return logf2, k, q


def _hgrn2_head(b, q, k, v, gr, ng, lvl, pos, st_ref, h, keep):
    t = b.shape[0]
    n_levels = t.bit_length() - 1
    scores = jnp.zeros((t, t), F32)
    for level in range(1, n_levels + 1):
        group = 1 << level
        d = b - _midpoint_rows(b, group, pos)
        later = (pos & (group - 1)) >= (group // 2)
        fac = jnp.exp2(-jnp.abs(d))
        ql = jnp.where(later, q * fac, 0.0).astype(BF16)
        kl = jnp.where(later, 0.0, k * fac).astype(BF16)
        scores = jnp.where(lvl == level, _dot_nt(ql, kl), scores)

    st = st_ref[h] * keep
    b_last = b[t - 1:t, :]
    o = _dot(scores.astype(BF16), v)
    o = o + _dot_nt((q * jnp.exp2(b)).astype(BF16), st.astype(BF16))
    o = o + jnp.sum(q * k, axis=1, keepdims=True) * v.astype(F32)
    k_dec = (k * jnp.exp2(b_last - b)).astype(BF16)
    st_ref[h] = jnp.exp2(b_last) * st + _dot_tn(v, k_dec)
    return _rms(o, ng) * (gr / (1.0 + jnp.exp(-gr)))


def _proj_hgrn2_kernel(x_ref, g_ref, w_ref, lbl_ref, ng_ref, lvl_ref,
                       q_ref, k_ref, vt_ref, gate_ref, ohg_ref, f_buf, iqg_buf, st_ref,
                       *, tiles_per_seq):
    t = x_ref.shape[0]
    d = x_ref.shape[1]
    dk = HG_HEAD_DIM
    hg_w = HG_HEADS * dk
    sb_w = q_ref.shape[1]
    i = pl.program_id(0)
    slot = i % 2
    prev = 1 - slot

    @pl.when(i == 0)
    def _():
        f_buf[1] = jnp.zeros(f_buf.shape[1:], f_buf.dtype)
        iqg_buf[1] = jnp.zeros(iqg_buf.shape[1:], iqg_buf.dtype)
        st_ref[...] = jnp.zeros_like(st_ref)

    xn = _rms(x_ref[...], g_ref[...]).astype(BF16)

    row = lax.broadcasted_iota(jnp.int32, (t, t), 0)
    col = lax.broadcasted_iota(jnp.int32, (t, t), 1)
    tril = (row >= col).astype(BF16)
    pos = lax.broadcasted_iota(jnp.int32, (t, dk), 0)
    lvl = lvl_ref[...]
    keep = jnp.where(i % tiles_per_seq == 1, 0.0, 1.0)

    lbl = lbl_ref[...]
    ex = jnp.exp(lbl - jnp.max(lbl, axis=0, keepdims=True))
    lb_all = ex[0:1, :] / jnp.sum(ex, axis=0, keepdims=True)

    def project(lo, hi):
        return _dot(xn, w_ref[:, lo:hi])

    def finish_head(h):
        sl = slice(h * dk, (h + 1) * dk)
        y = _hgrn2_head(b_all[:, sl], q_all[:, sl], k_all[:, sl], iqg_buf[prev, :, sl],
                        iqg_buf[prev, :, 2 * hg_w + h * dk:2 * hg_w + (h + 1) * dk].astype(F32),
                        ng_ref[:, sl], lvl, pos, st_ref, h, keep)
        ohg_ref[:, sl] = y.astype(ohg_ref.dtype)

    c0 = 3 * sb_w
    c1 = c0 + hg_w
    c2 = c1 + 3 * hg_w
    q_ref[...] = (project(0, sb_w) * (SB_HEAD_DIM ** -0.5 * LOG2E)).astype(q_ref.dtype)
    k_ref[...] = project(sb_w, 2 * sb_w).astype(k_ref.dtype)
    logf2, k_all, q_all = _hgrn2_gates(f_buf[prev], iqg_buf[prev, :, hg_w:2 * hg_w].astype(F32),
                                       lb_all)
    g1, g2, g3 = _split3(logf2)
    b_all = _dot(tril, g1) + _dot(tril, g2) + _dot(tril, g3)
    vt_ref[0] = project(2 * sb_w, 3 * sb_w).T.astype(vt_ref.dtype)
    f_buf[slot] = project(c0, c1)
    finish_head(0)
    iqg_buf[slot] = project(c1, c2).astype(iqg_buf.dtype)
    finish_head(1)
    gate_ref[:, 0:d] = project(c2, c2 + d)
    finish_head(2)
    gate_ref[:, d:d + d // 2] = project(c2 + d, c2 + d + d // 2)
    finish_head(3)
    gate_ref[:, d + d // 2:2 * d] = project(c2 + d + d // 2, c2 + 2 * d)


def _pair_levels(t):
    idx = np.arange(t)
    x = idx[:, None] ^ idx[None, :]
    lev = np.where(x > 0, np.floor(np.log2(np.maximum(x, 1))).astype(np.int64) + 1, 0)
    return np.where(idx[:, None] > idx[None, :], lev, 0).astype(np.int32)


def _proj_hgrn2(x2, g1, w_in, lb_logits, ng, batch, sb_w, hg_w):
    n, d = x2.shape
    t = HG_BLOCK
    cols = w_in.shape[1]
    n_tiles = n // t
    tiles_per_seq = n_tiles // batch
    assert hg_w == HG_HEADS * HG_HEAD_DIM and n_tiles * t == n and tiles_per_seq * batch == n_tiles
    lvl = jnp.asarray(_pair_levels(t))
    cur = lambda i: jnp.minimum(i, n_tiles - 1)
    rows = lambda w: pl.BlockSpec((t, w), lambda i: (cur(i), 0))
    vt_spec = pl.BlockSpec((1, sb_w, t),
                           lambda i: (cur(i) // tiles_per_seq, 0, cur(i) % tiles_per_seq))
    sds = lambda w, dt: jax.ShapeDtypeStruct((n, w), dt)
    moving = t * (d * 4 + 2 * sb_w * 2 + sb_w * 2 + 2 * d * 4 + hg_w * 2)
    resident = d * 4 + d * cols * 2 + 3 * hg_w * 4 + t * t * 4 \
        + 2 * t * hg_w * (4 + 3 * 2) + HG_HEADS * HG_HEAD_DIM * HG_HEAD_DIM * 4
    return pl.pallas_call(
        functools.partial(_proj_hgrn2_kernel, tiles_per_seq=tiles_per_seq),
        grid=(n_tiles + 1,),
        in_specs=[rows(d), _resident((1, d)), _resident((d, cols)),
                  _resident(lb_logits.shape), _resident(ng.shape), _resident((t, t))],
        out_specs=[rows(sb_w), rows(sb_w), vt_spec, rows(2 * d),
                   pl.BlockSpec((t, hg_w), lambda i: (jnp.maximum(i - 1, 0), 0))],
        out_shape=[sds(sb_w, BF16), sds(sb_w, BF16),
                   jax.ShapeDtypeStruct((batch, sb_w, n // batch), BF16),
                   sds(2 * d, F32), sds(hg_w, BF16)],
        scratch_shapes=[pltpu.VMEM((2, t, hg_w), F32),
                        pltpu.VMEM((2, t, 3 * hg_w), BF16),
                        pltpu.VMEM((HG_HEADS, HG_HEAD_DIM, HG_HEAD_DIM), F32)],
        compiler_params=pltpu.CompilerParams(
            dimension_semantics=("arbitrary",),
            vmem_limit_bytes=_vmem_limit(moving, resident,
                                         t * (d * 6 + cols * 4) + 64 * t * t * 4)),
        name="proj_hgrn2",
    )(x2, g1, w_in, lb_logits, ng, lvl)


def _merge_ffn_kernel(x_ref, osb_ref, ohg_ref, gate_ref, bg_ref, wsb_ref, whg_ref, wout_ref,
                      g2_ref, w1_ref, w2_ref, gf_ref, o_ref):
    d = x_ref.shape[1]
    sub = x_ref.shape[0] // MERGE_STREAMS
    rows = [slice(s * sub, (s + 1) * sub) for s in range(MERGE_STREAMS)]
    a_sb = [_dot(osb_ref[r, :], wsb_ref[...]) for r in rows]
    a_hg = [_dot(ohg_ref[r, :], whg_ref[...]) for r in rows]
    merged = []
    for s, r in enumerate(rows):
        gates = 1.0 / (1.0 + jnp.exp(-(gate_ref[r, :] + bg_ref[...])))
        merged.append((gates[:, :d] * a_sb[s] + gates[:, d:] * a_hg[s]).astype(BF16))
    h = [x_ref[r, :] + _dot(merged[s], wout_ref[...]) for s, r in enumerate(rows)]
    hn = [_rms(h[s], g2_ref[...]).astype(BF16) for s in range(MERGE_STREAMS)]
    a = [jnp.maximum(_dot(hn[s], w1_ref[...]), 0.0) for s in range(MERGE_STREAMS)]
    h = [h[s] + _dot((a[s] * a[s]).astype(BF16), w2_ref[...]) for s in range(MERGE_STREAMS)]
    for s, r in enumerate(rows):
        o_ref[r, :] = _rms(h[s], gf_ref[...]).astype(o_ref.dtype)


def _merge_ffn(x2, osb, ohg, gates, bg, wsb, whg, wout, g2, w1, w2, gf):
    n, d = x2.shape
    tm = MERGE_STREAMS * TOKEN_TILE
    dff = w1.shape[1]
    row = lambda i: (i, 0)
    full = lambda a: _resident(a.shape)
    resident_bytes = sum(a.size * a.dtype.itemsize for a in (bg, wsb, whg, wout, g2, w1, w2, gf))
    block_bytes = tm * (d * 4 + osb.shape[1] * 2 + ohg.shape[1] * 2 + 2 * d * 4 + d * 4)
    return pl.pallas_call(
        _merge_ffn_kernel,
        grid=(n // tm,),
        in_specs=[pl.BlockSpec((tm, d), row),
                  pl.BlockSpec((tm, osb.shape[1]), row),
                  pl.BlockSpec((tm, ohg.shape[1]), row),
                  pl.BlockSpec((tm, 2 * d), row),
                  full(bg), full(wsb), full(whg), full(wout), full(g2), full(w1), full(w2),
                  full(gf)],
        out_specs=pl.BlockSpec((tm, d), row),
        out_shape=jax.ShapeDtypeStruct((n, d), x2.dtype),
        compiler_params=pltpu.CompilerParams(
            dimension_semantics=("parallel",),
            vmem_limit_bytes=_vmem_limit(block_bytes, resident_bytes, tm * (dff * 6 + d * 24))),
        name="merge_ffn",
    )(x2, osb, ohg, gates, bg, wsb, whg, wout, g2, w1, w2, gf)


def kernel(x, norm1_g, w_in, b_gate, lb_logits, hg_norm_g, w_o_sb, w_o_hg, w_out, norm2_g,
           w_ff1, w_ff2, final_g):
    b, s, d = x.shape
    assert w_in.shape[0] == 1, "single-layer block"
    sb_w = SB_HEADS * SB_HEAD_DIM
    hg_w = HG_HEADS * HG_HEAD_DIM
    x2 = x.reshape(b * s, d)
    q, k, vt, gates, o_hg = _proj_hgrn2(x2, norm1_g, w_in[0].astype(BF16), lb_logits,
                                        hg_norm_g, b, sb_w, hg_w)
    o_sb = _sb_attn(q.reshape(b, s, sb_w), k.reshape(b, s, sb_w), vt)
    out = _merge_ffn(x2, o_sb.reshape(b * s, sb_w), o_hg, gates, b_gate,
                     w_o_sb[0].astype(BF16), w_o_hg[0].astype(BF16), w_out[0].astype(BF16),
                     norm2_g, w_ff1[0].astype(BF16), w_ff2[0].astype(BF16),
                     final_g.reshape(1, d))
    return out.reshape(b, s, d)
```

```python
import functools

import jax
import jax.numpy as jnp
import numpy as np
from jax import lax
from jax.experimental import pallas as pl
from jax.experimental.pallas import tpu as pltpu

F32 = jnp.float32
BF16 = jnp.bfloat16

SB_HEADS = 8
SB_HEAD_DIM = 64
HG_HEADS = 4
HG_HEAD_DIM = 128
EPS = 1e-6
LOG2E = 1.4426950408889634
SB_DEAD_CARRY = 151.0

V7X_LANES = 128
V7X_MXU_DIM = 256
V7X_VMEM_BYTES = 64 * 1024 * 1024

ATTN_TILE = V7X_MXU_DIM
HG_BLOCK = V7X_MXU_DIM


def _vmem_limit(pipelined_bytes, resident_bytes, temp_bytes):
    need = 2 * pipelined_bytes + resident_bytes + temp_bytes
    return int(min(need + need // 4, V7X_VMEM_BYTES - 8 * 1024 * 1024))


def _resident(shape):
    return pl.BlockSpec(shape, lambda *_: (0,) * len(shape), pipeline_mode=pl.Buffered(1))


def _rms(x, g):
    ms = jnp.mean(x * x, axis=-1, keepdims=True)
    return x * lax.rsqrt(ms + EPS) * g


def _dot(a, b):
    return jnp.dot(a, b, preferred_element_type=F32)


def _dot_nt(a, b):
    return lax.dot_general(a, b, (((1,), (1,)), ((), ())), preferred_element_type=F32)


def _dot_tn(a, b):
    return lax.dot_general(a, b, (((0,), (0,)), ((), ())), preferred_element_type=F32)


def _softplus2(z, mask):
    sp = jnp.maximum(z, 0.0) + jnp.log(1.0 + jnp.exp2(-jnp.abs(z))) * LOG2E
    return sp if mask is None else jnp.where(mask, sp, 0.0)


def _split3(x):
    a = x.astype(BF16)
    r = x - a.astype(F32)
    b = r.astype(BF16)
    c = (r - b.astype(F32)).astype(BF16)
    return a, b, c


def _rows_from_group(b, group, r):
    n, c = b.shape
    if group == n:
        return jnp.broadcast_to(b[r:r + 1, :], (n, c))
    b3 = b.reshape(n // group, group, c)
    return jnp.broadcast_to(b3[:, r:r + 1, :], b3.shape).reshape(n, c)


def _midpoint_rows(b, group, pos):
    n = b.shape[0]
    half = group // 2
    if group >= 16:
        return _rows_from_group(b, group, half - 1)
    if group == 8:
        return _rows_from_group(b, 8, 3)
    up1 = pltpu.roll(b, n - 1, 0)
    dn1 = pltpu.roll(b, 1, 0)
    if group == 2:
        return jnp.where((pos & 1) == 0, b, dn1)
    assert group == 4
    dn2 = pltpu.roll(b, 2, 0)
    r4 = pos & 3
    return jnp.where(r4 == 0, up1, jnp.where(r4 == 1, b, jnp.where(r4 == 2, dn1, dn2)))


def _hgrn2_gates(fr, qr, lb):
    e = jnp.exp(-jnp.abs(fr))
    r = 1.0 / (1.0 + e)
    er = e * r
    sig = jnp.where(fr >= 0, r, er)
    nsig = jnp.where(fr >= 0, er, r)
    logf2 = jnp.log(lb + (1.0 - lb) * sig) * LOG2E
    k = (1.0 - lb) * nsig
    q = qr / (1.0 + jnp.exp(-qr))
    return logf2, k, q


def _hgrn2_head(b, q, k, v, gr, ng, lvl, pos, st_ref, h, keep):
    t = b.shape[0]
    n_levels = t.bit_length() - 1
    scores = jnp.zeros((t, t), F32)
    for level in range(1, n_levels + 1):
        group = 1 << level
        d = b - _midpoint_rows(b, group, pos)
        later = (pos & (group - 1)) >= (group // 2)
        fac = jnp.exp2(-jnp.abs(d))
        ql = jnp.where(later, q * fac, 0.0).astype(BF16)
        kl = jnp.where(later, 0.0, k * fac).astype(BF16)
        scores = jnp.where(lvl == level, _dot_nt(ql, kl), scores)

    st = st_ref[h] * keep
    b_last = b[t - 1:t, :]
    o = _dot(scores.astype(BF16), v)
    o = o + _dot_nt((q * jnp.exp2(b)).astype(BF16), st.astype(BF16))
    o = o + jnp.sum(q * k, axis=1, keepdims=True) * v.astype(F32)
    k_dec = (k * jnp.exp2(b_last - b)).astype(BF16)
    st_ref[h] = jnp.exp2(b_last) * st + _dot_tn(v, k_dec)
    return _rms(o, ng) * (gr / (1.0 + jnp.exp(-gr)))


def _proj_hgrn2_kernel(x_ref, g_ref, w_ref, lbl_ref, ng_ref, lvl_ref,
                       q_ref, k_ref, vt_ref, gate_ref, ohg_ref, f_buf, iqg_buf, st_ref,
                       *, tiles_per_seq):
    t = x_ref.shape[0]
    d = x_ref.shape[1]
    dk = HG_HEAD_DIM
    hg_w = HG_HEADS * dk
    sb_w = q_ref.shape[1]
    i = pl.program_id(0)
    slot = i % 2
    prev = 1 - slot

    @pl.when(i == 0)
    def _():
        f_buf[1] = jnp.zeros(f_buf.shape[1:], f_buf.dtype)
        iqg_buf[1] = jnp.zeros(iqg_buf.shape[1:], iqg_buf.dtype)
        st_ref[...] = jnp.zeros_like(st_ref)

    xn = _rms(x_ref[...], g_ref[...]).astype(BF16)

    row = lax.broadcasted_iota(jnp.int32, (t, t), 0)
    col = lax.broadcasted_iota(jnp.int32, (t, t), 1)
    tril = (row >= col).astype(BF16)
    pos = lax.broadcasted_iota(jnp.int32, (t, dk), 0)
    lvl = lvl_ref[...]
    keep = jnp.where(i % tiles_per_seq == 1, 0.0, 1.0)

    lbl = lbl_ref[...]
    ex = jnp.exp(lbl - jnp.max(lbl, axis=0, keepdims=True))
    lb_all = ex[0:1, :] / jnp.sum(ex, axis=0, keepdims=True)

    def project(lo, hi):
        return _dot(xn, w_ref[:, lo:hi])

    def finish_head(h):
        sl = slice(h * dk, (h + 1) * dk)
        y = _hgrn2_head(b_all[:, sl], q_all[:, sl], k_all[:, sl], iqg_buf[prev, :, sl],
                        iqg_buf[prev, :, 2 * hg_w + h * dk:2 * hg_w + (h + 1) * dk].astype(F32),
                        ng_ref[:, sl], lvl, pos, st_ref, h, keep)
        ohg_ref[:, sl] = y.astype(ohg_ref.dtype)

    c0 = 3 * sb_w
    c1 = c0 + hg_w
    c2 = c1 + 3 * hg_w
    q_ref[...] = (project(0, sb_w) * (SB_HEAD_DIM ** -0.5 * LOG2E)).astype(q_ref.dtype)
    k_ref[...] = project(sb_w, 2 * sb_w).astype(k_ref.dtype)
    logf2, k_all, q_all = _hgrn2_gates(f_buf[prev], iqg_buf[prev, :, hg_w:2 * hg_w].astype(F32),
                                       lb_all)
    g1, g2, g3 = _split3(logf2)
    b_all = _dot(tril, g1) + _dot(tril, g2) + _dot(tril, g3)
    vt_ref[0] = project(2 * sb_w, 3 * sb_w).T.astype(vt_ref.dtype)
    f_buf[slot] = project(c0, c1)
    finish_head(0)
    iqg_buf[slot] = project(c1, c2).astype(iqg_buf.dtype)
    finish_head(1)
    gate_ref[:, 0:d] = project(c2, c2 + d)
    finish_head(2)
    gate_ref[:, d:d + d // 2] = project(c2 + d, c2 + d + d // 2)
    finish_head(3)
    gate_ref[:, d + d // 2:2 * d] = project(c2 + d + d // 2, c2 + 2 * d)


def _pair_levels(t):
    idx = np.arange(t)
    x = idx[:, None] ^ idx[None, :]
    lev = np.where(x > 0, np.floor(np.log2(np.maximum(x, 1))).astype(np.int64) + 1, 0)
    return np.where(idx[:, None] > idx[None, :], lev, 0).astype(np.int32)


def _proj_hgrn2(x2, g1, w_in, lb_logits, ng, batch, sb_w, hg_w):
    n, d = x2.shape
    t = HG_BLOCK
    cols = w_in.shape[1]
    n_tiles = n // t
    tiles_per_seq = n_tiles // batch
    assert hg_w == HG_HEADS * HG_HEAD_DIM and n_tiles * t == n and tiles_per_seq * batch == n_tiles
    lvl = jnp.asarray(_pair_levels(t))
    cur = lambda i: jnp.minimum(i, n_tiles - 1)
    rows = lambda w: pl.BlockSpec((t, w), lambda i: (cur(i), 0))
    vt_spec = pl.BlockSpec((1, sb_w, t),
                           lambda i: (cur(i) // tiles_per_seq, 0, cur(i) % tiles_per_seq))
    sds = lambda w, dt: jax.ShapeDtypeStruct((n, w), dt)
    moving = t * (d * 4 + 2 * sb_w * 2 + sb_w * 2 + 2 * d * 4 + hg_w * 2)
    resident = d * 4 + d * cols * 2 + 3 * hg_w * 4 + t * t * 4 \
        + 2 * t * hg_w * (4 + 3 * 2) + HG_HEADS * HG_HEAD_DIM * HG_HEAD_DIM * 4
    return pl.pallas_call(
        functools.partial(_proj_hgrn2_kernel, tiles_per_seq=tiles_per_seq),
        grid=(n_tiles + 1,),
        in_specs=[rows(d), _resident((1, d)), _resident((d, cols)),
                  _resident(lb_logits.shape), _resident(ng.shape), _resident((t, t))],
        out_specs=[rows(sb_w), rows(sb_w), vt_spec, rows(2 * d),
                   pl.BlockSpec((t, hg_w), lambda i: (jnp.maximum(i - 1, 0), 0))],
        out_shape=[sds(sb_w, BF16), sds(sb_w, BF16),
                   jax.ShapeDtypeStruct((batch, sb_w, n // batch), BF16),
                   sds(2 * d, F32), sds(hg_w, BF16)],
        scratch_shapes=[pltpu.VMEM((2, t, hg_w), F32),
                        pltpu.VMEM((2, t, 3 * hg_w), BF16),
                        pltpu.VMEM((HG_HEADS, HG_HEAD_DIM, HG_HEAD_DIM), F32)],
        compiler_params=pltpu.CompilerParams(
            dimension_semantics=("arbitrary",),
            vmem_limit_bytes=_vmem_limit(moving, resident,
                                         t * (d * 6 + cols * 4) + 64 * t * t * 4)),
        name="proj_hgrn2",
    )(x2, g1, w_in, lb_logits, ng, lvl)


def _attn_ffn_kernel(q_ref, k_ref, vt_ref, x_ref, ohg_ref, gate_ref, bg_ref, wsb_ref, whg_ref,
                     wout_ref, g2_ref, w1_ref, w2_ref, gf_ref, o_ref,
                     osb_buf, qm_ref, acc_ref, carry_ref, *, tiles_per_seq, n_tiles):
    t = q_ref.shape[1]
    d = x_ref.shape[1]
    heads = range(SB_HEADS)
    i = pl.program_id(0)
    slot = i % 2
    qi = jnp.minimum(i, n_tiles - 1) % tiles_per_seq

    @pl.when(i == 0)
    def _():
        osb_buf[1] = jnp.zeros(osb_buf.shape[1:], osb_buf.dtype)

    row = lax.broadcasted_iota(jnp.int32, (t, t), 0)
    col = lax.broadcasted_iota(jnp.int32, (t, t), 1)
    tri = (col > row).astype(BF16)
    causal = row < col
    lane = lax.broadcasted_iota(jnp.int32, (t, V7X_LANES), 1)
    zero = jnp.zeros((), BF16)
    q = q_ref[0]
    for h in heads:
        grp = q[:, (h // 2) * V7X_LANES:(h // 2 + 1) * V7X_LANES]
        qm_ref[h] = jnp.where((lane // SB_HEAD_DIM) == (h % 2), grp, zero)

    def logits(j):
        start = pl.multiple_of(j * t, t)
        k = k_ref[0, pl.ds(start, t), :]
        return [_dot_nt(k[:, (h // 2) * V7X_LANES:(h // 2 + 1) * V7X_LANES], qm_ref[h])
                for h in heads]

    def values_t(j, h):
        return vt_ref[0, h * SB_HEAD_DIM:(h + 1) * SB_HEAD_DIM, pl.ds(pl.multiple_of(j * t, t), t)]

    def softplus_phase(z, mask):
        sp = [_softplus2(z[h], mask) for h in heads]
        return sp, [sp[h].astype(BF16) for h in heads]

    def cumsum_phase(spb):
        return [_dot(tri, spb[h]) for h in heads]

    def weight_phase(z, sp, later, mask):
        w = [jnp.exp2(z[h] - sp[h] - later[h]) for h in heads]
        if mask is not None:
            w = [jnp.where(mask, w[h], 0.0) for h in heads]
        return [w[h].astype(BF16) for h in heads]

    def value_phase(j, w, later, spb):
        pv = [_dot(values_t(j, h), w[h]) for h in heads]
        return pv, [later[h][0:1, :] + spb[h][0:1, :].astype(F32) for h in heads]

    has_prev = qi > 0
    jp = jnp.maximum(qi - 1, 0)
    z0 = logits(qi)
    a_sb = _dot(osb_buf[1 - slot], wsb_ref[...])
    a_hg = _dot(ohg_ref[...], whg_ref[...])
    gates = 1.0 / (1.0 + jnp.exp(-(gate_ref[...] + bg_ref[...])))
    merged = (gates[:, :d] * a_sb + gates[:, d:] * a_hg).astype(BF16)
    hres = x_ref[...] + _dot(merged, wout_ref[...])
    z1 = logits(jp)
    hn = _rms(hres, g2_ref[...]).astype(BF16)
    half = w1_ref.shape[1] // 2

    def mlp_up(lo):
        act = jnp.maximum(_dot(hn, w1_ref[:, lo:lo + half]), 0.0)
        return (act * act).astype(BF16)

    act_a = mlp_up(0)
    sp0, spb0 = softplus_phase(z0, causal)
    lat0 = cumsum_phase(spb0)
    act_b = mlp_up(half)
    sp1, spb1 = softplus_phase(z1, None)
    lat1 = cumsum_phase(spb1)
    hres = hres + _dot(act_a, w2_ref[0:half, :])
    w0 = weight_phase(z0, sp0, lat0, causal)
    w1 = weight_phase(z1, sp1, lat1, None)
    pv0, tot0 = value_phase(qi, w0, lat0, spb0)
    pv1, tot1 = value_phase(jp, w1, lat1, spb1)
    hres = hres + _dot(act_b, w2_ref[half:2 * half, :])
    for h in heads:
        scale = jnp.where(has_prev, jnp.exp2(-tot0[h]), 0.0)
        acc_ref[h] = pv0[h] + scale * pv1[h]
        carry_ref[h:h + 1, :] = tot0[h] + jnp.where(has_prev, tot1[h], 0.0)
    o_ref[...] = _rms(hres, gf_ref[...]).astype(o_ref.dtype)

    def more(state):
        n, live = state
        return jnp.logical_and(n < qi, live)

    def body(state):
        n, _ = state
        j = qi - 1 - n
        z = logits(j)
        sp, spb = softplus_phase(z, None)
        later = cumsum_phase(spb)
        pv, tot = value_phase(j, weight_phase(z, sp, later, None), later, spb)
        for h in heads:
            c = carry_ref[h:h + 1, :]
            acc_ref[h] += jnp.exp2(-c) * pv[h]
            carry_ref[h:h + 1, :] = c + tot[h]
        return n + 1, jnp.min(carry_ref[...]) < SB_DEAD_CARRY

    lax.while_loop(more, body, (jnp.int32(1), jnp.min(carry_ref[...]) < SB_DEAD_CARRY))
    osb_buf[slot] = acc_ref[...].reshape(SB_HEADS * SB_HEAD_DIM, t).T.astype(osb_buf.dtype)


def _attn_ffn(q3, k3, vt3, x2, ohg, gates, bg, wsb, whg, wout, g2, w1, w2, gf):
    b, s, sb_w = q3.shape
    n, d = x2.shape
    t = ATTN_TILE
    dff = w1.shape[1]
    tiles_per_seq = s // t
    n_tiles = n // t
    assert sb_w == SB_HEADS * SB_HEAD_DIM and tiles_per_seq * t == s and n_tiles == b * tiles_per_seq
    cur = lambda i: jnp.minimum(i, n_tiles - 1)
    prv = lambda i: (jnp.maximum(i - 1, 0), 0)
    seq = lambda i: cur(i) // tiles_per_seq
    whole_seq = lambda shape: pl.BlockSpec(shape, lambda i: (seq(i), 0, 0),
                                           pipeline_mode=pl.Buffered(1))
    full = lambda a: _resident(a.shape)
    resident_bytes = sum(a.size * a.dtype.itemsize for a in (bg, wsb, whg, wout, g2, w1, w2, gf)) \
        + 2 * s * sb_w * 2 + 2 * t * sb_w * 2 + SB_HEADS * t * (V7X_LANES * 2 + SB_HEAD_DIM * 4 + 4)
    moving = t * (sb_w * 2 + d * 4 + ohg.shape[1] * 2 + 2 * d * 4 + d * 4)
    return pl.pallas_call(
        functools.partial(_attn_ffn_kernel, tiles_per_seq=tiles_per_seq, n_tiles=n_tiles),
        grid=(n_tiles + 1,),
        in_specs=[pl.BlockSpec((1, t, sb_w), lambda i: (seq(i), cur(i) % tiles_per_seq, 0)),
                  whole_seq((1, s, sb_w)), whole_seq((1, sb_w, s)),
                  pl.BlockSpec((t, d), prv),
                  pl.BlockSpec((t, ohg.shape[1]), prv),
                  pl.BlockSpec((t, 2 * d), prv),
                  full(bg), full(wsb), full(whg), full(wout), full(g2), full(w1), full(w2),
                  full(gf)],
        out_specs=pl.BlockSpec((t, d), prv),
        out_shape=jax.ShapeDtypeStruct((n, d), x2.dtype),
        scratch_shapes=[pltpu.VMEM((2, t, sb_w), BF16),
                        pltpu.VMEM((SB_HEADS, t, V7X_LANES), BF16),
                        pltpu.VMEM((SB_HEADS, SB_HEAD_DIM, t), F32),
                        pltpu.VMEM((SB_HEADS, t), F32)],
        compiler_params=pltpu.CompilerParams(
            dimension_semantics=("arbitrary",),
            vmem_limit_bytes=_vmem_limit(moving, resident_bytes,
                                         t * (dff * 6 + d * 24) + 8 * SB_HEADS * t * t * 4)),
        name="attn_ffn",
    )(q3, k3, vt3, x2, ohg, gates, bg, wsb, whg, wout, g2, w1, w2, gf)


def kernel(x, norm1_g, w_in, b_gate, lb_logits, hg_norm_g, w_o_sb, w_o_hg, w_out, norm2_g,
           w_ff1, w_ff2, final_g):
    b, s, d = x.shape
    assert w_in.shape[0] == 1, "single-layer block"
    sb_w = SB_HEADS * SB_HEAD_DIM
    hg_w = HG_HEADS * HG_HEAD_DIM
    x2 = x.reshape(b * s, d)
    q, k, vt, gates, o_hg = _proj_hgrn2(x2, norm1_g, w_in[0].astype(BF16), lb_logits,
                                        hg_norm_g, b, sb_w, hg_w)
    out = _attn_ffn(q.reshape(b, s, sb_w), k.reshape(b, s, sb_w), vt, x2, o_hg, gates, b_gate,
                    w_o_sb[0].astype(BF16), w_o_hg[0].astype(BF16), w_out[0].astype(BF16),
                    norm2_g, w_ff1[0].astype(BF16), w_ff2[0].astype(BF16), final_g.reshape(1, d))
    return out.reshape(b, s, d)
```

```python
import functools

import jax
import jax.numpy as jnp
import numpy as np
from jax import lax
from jax.experimental import pallas as pl
from jax.experimental.pallas import tpu as pltpu

F32 = jnp.float32
BF16 = jnp.bfloat16

SB_HEADS = 8
SB_HEAD_DIM = 64
HG_HEADS = 4
HG_HEAD_DIM = 128
EPS = 1e-6
LOG2E = 1.4426950408889634
SB_DEAD_CARRY = 151.0

V7X_LANES = 128
V7X_MXU_DIM = 256
V7X_VMEM_BYTES = 64 * 1024 * 1024

ATTN_TILE = V7X_MXU_DIM
HG_BLOCK = V7X_MXU_DIM


def _vmem_limit(pipelined_bytes, resident_bytes, temp_bytes):
    need = 2 * pipelined_bytes + resident_bytes + temp_bytes
    return int(min(need + need // 4, V7X_VMEM_BYTES - 8 * 1024 * 1024))


def _resident(shape):
    return pl.BlockSpec(shape, lambda *_: (0,) * len(shape), pipeline_mode=pl.Buffered(1))


def _rms(x, g):
    ms = jnp.mean(x * x, axis=-1, keepdims=True)
    return x * lax.rsqrt(ms + EPS) * g


def _dot(a, b):
    return jnp.dot(a, b, preferred_element_type=F32)


def _dot_nt(a, b):
    return lax.dot_general(a, b, (((1,), (1,)), ((), ())), preferred_element_type=F32)


def _dot_tn(a, b):
    return lax.dot_general(a, b, (((0,), (0,)), ((), ())), preferred_element_type=F32)


def _softplus2(z, mask):
    sp = jnp.maximum(z, 0.0) + jnp.log(1.0 + jnp.exp2(-jnp.abs(z))) * LOG2E
    return sp if mask is None else jnp.where(mask, sp, 0.0)


def _split3(x):
    a = x.astype(BF16)
    r = x - a.astype(F32)
    b = r.astype(BF16)
    c = (r - b.astype(F32)).astype(BF16)
    return a, b, c


def _rows_from_group(b, group, r):
    n, c = b.shape
    if group == n:
        return jnp.broadcast_to(b[r:r + 1, :], (n, c))
    b3 = b.reshape(n // group, group, c)
    return jnp.broadcast_to(b3[:, r:r + 1, :], b3.shape).reshape(n, c)


def _midpoint_rows(b, group, pos):
    n = b.shape[0]
    half = group // 2
    if group >= 16:
        return _rows_from_group(b, group, half - 1)
    if group == 8:
        return _rows_from_group(b, 8, 3)
    up1 = pltpu.roll(b, n - 1, 0)
    dn1 = pltpu.roll(b, 1, 0)
    if group == 2:
        return jnp.where((pos & 1) == 0, b, dn1)
    assert group == 4
    dn2 = pltpu.roll(b, 2, 0)
    r4 = pos & 3
    return jnp.where(r4 == 0, up1, jnp.where(r4 == 1, b, jnp.where(r4 == 2, dn1, dn2)))


def _hgrn2_gates(fr, qr, lb):
    e = jnp.exp(-jnp.abs(fr))
    r = 1.0 / (1.0 + e)
    er = e * r
    sig = jnp.where(fr >= 0, r, er)
    nsig = jnp.where(fr >= 0, er, r)
    logf2 = jnp.log(lb + (1.0 - lb) * sig) * LOG2E
    k = (1.0 - lb) * nsig
    q = qr / (1.0 + jnp.exp(-qr))
    return logf2, k, q


def _hgrn2_head(b, q, k, v, gr, ng, lvl, pos, st_ref, h, keep):
    t, dk = b.shape
    n_levels = t.bit_length() - 1
    hb = t // 2
    zeros = jnp.zeros((hb, dk), BF16)
    lvl_d = jnp.concatenate([lvl[0:hb, 0:hb], lvl[hb:t, hb:t]], axis=1)
    diag = jnp.zeros((hb, t), F32)
    for level in range(1, n_levels):
        group = 1 << level
        d = b - _midpoint_rows(b, group, pos)
        later = (pos & (group - 1)) >= (group // 2)
        fac = jnp.exp2(-jnp.abs(d))
        ql = jnp.where(later, q * fac, 0.0).astype(BF16)
        kl = jnp.where(later, 0.0, k * fac).astype(BF16)
        lhs = jnp.concatenate([ql[0:hb], ql[hb:t]], axis=1)
        rhs = jnp.concatenate([jnp.concatenate([kl[0:hb], zeros], axis=1),
                               jnp.concatenate([zeros, kl[hb:t]], axis=1)], axis=0)
        diag = jnp.where(lvl_d == level, _dot_nt(lhs, rhs), diag)
    b_mid = b[hb - 1:hb, :]
    q_top = (q[hb:t] * jnp.exp2(b[hb:t] - b_mid)).astype(BF16)
    k_top = (k[0:hb] * jnp.exp2(b_mid - b[0:hb])).astype(BF16)
    top = _dot_nt(q_top, k_top)
    scores = jnp.concatenate(
        [jnp.concatenate([diag[:, 0:hb], jnp.zeros((hb, hb), F32)], axis=1),
         jnp.concatenate([top, diag[:, hb:t]], axis=1)], axis=0)

    st = st_ref[h] * keep
    b_last = b[t - 1:t, :]
    o = _dot(scores.astype(BF16), v)
    o = o + _dot_nt((q * jnp.exp2(b)).astype(BF16), st.astype(BF16))
    o = o + jnp.sum(q * k, axis=1, keepdims=True) * v.astype(F32)
    k_dec = (k * jnp.exp2(b_last - b)).astype(BF16)
    st_ref[h] = jnp.exp2(b_last) * st + _dot_tn(v, k_dec)
    return _rms(o, ng) * (gr / (1.0 + jnp.exp(-gr)))


def _proj_hgrn2_kernel(x_ref, g_ref, w_ref, lbl_ref, ng_ref, lvl_ref,
                       q_ref, k_ref, vt_ref, gate_ref, ohg_ref, f_buf, iqg_buf, st_ref,
                       *, tiles_per_seq):
    t = x_ref.shape[0]
    d = x_ref.shape[1]
    dk = HG_HEAD_DIM
    hg_w = HG_HEADS * dk
    sb_w = q_ref.shape[1]
    i = pl.program_id(0)
    slot = i % 2
    prev = 1 - slot

    @pl.when(i == 0)
    def _():
        f_buf[1] = jnp.zeros(f_buf.shape[1:], f_buf.dtype)
        iqg_buf[1] = jnp.zeros(iqg_buf.shape[1:], iqg_buf.dtype)
        st_ref[...] = jnp.zeros_like(st_ref)

    xn = _rms(x_ref[...], g_ref[...]).astype(BF16)

    row = lax.broadcasted_iota(jnp.int32, (t, t), 0)
    col = lax.broadcasted_iota(jnp.int32, (t, t), 1)
    tril = (row >= col).astype(BF16)
    pos = lax.broadcasted_iota(jnp.int32, (t, dk), 0)
    lvl = lvl_ref[...]
    keep = jnp.where(i % tiles_per_seq == 1, 0.0, 1.0)

    lbl = lbl_ref[...]
    ex = jnp.exp(lbl - jnp.max(lbl, axis=0, keepdims=True))
    lb_all = ex[0:1, :] / jnp.sum(ex, axis=0, keepdims=True)

    def project(lo, hi):
        return _dot(xn, w_ref[:, lo:hi])

    def finish_head(h):
        sl = slice(h * dk, (h + 1) * dk)
        y = _hgrn2_head(b_all[:, sl], q_all[:, sl], k_all[:, sl], iqg_buf[prev, :, sl],
                        iqg_buf[prev, :, 2 * hg_w + h * dk:2 * hg_w + (h + 1) * dk].astype(F32),
                        ng_ref[:, sl], lvl, pos, st_ref, h, keep)
        ohg_ref[:, sl] = y.astype(ohg_ref.dtype)

    c0 = 3 * sb_w
    c1 = c0 + hg_w
    c2 = c1 + 3 * hg_w
    q_ref[...] = (project(0, sb_w) * (SB_HEAD_DIM ** -0.5 * LOG2E)).astype(q_ref.dtype)
    k_ref[...] = project(sb_w, 2 * sb_w).astype(k_ref.dtype)
    logf2, k_all, q_all = _hgrn2_gates(f_buf[prev], iqg_buf[prev, :, hg_w:2 * hg_w].astype(F32),
                                       lb_all)
    g1, g2, g3 = _split3(logf2)
    b_all = _dot(tril, g1) + _dot(tril, g2) + _dot(tril, g3)
    vt_ref[0] = project(2 * sb_w, 3 * sb_w).T.astype(vt_ref.dtype)
    f_buf[slot] = project(c0, c1)
    finish_head(0)
    iqg_buf[slot] = project(c1, c2).astype(iqg_buf.dtype)
    finish_head(1)
    gate_ref[:, 0:d] = project(c2, c2 + d)
    finish_head(2)
    gate_ref[:, d:d + d // 2] = project(c2 + d, c2 + d + d // 2)
    finish_head(3)
    gate_ref[:, d + d // 2:2 * d] = project(c2 + d + d // 2, c2 + 2 * d)


def _pair_levels(t):
    idx = np.arange(t)
    x = idx[:, None] ^ idx[None, :]
    lev = np.where(x > 0, np.floor(np.log2(np.maximum(x, 1))).astype(np.int64) + 1, 0)
    return np.where(idx[:, None] > idx[None, :], lev, 0).astype(np.int32)


def _proj_hgrn2(x2, g1, w_in, lb_logits, ng, batch, sb_w, hg_w):
    n, d = x2.shape
    t = HG_BLOCK
    cols = w_in.shape[1]
    n_tiles = n // t
    tiles_per_seq = n_tiles // batch
    assert hg_w == HG_HEADS * HG_HEAD_DIM and n_tiles * t == n and tiles_per_seq * batch == n_tiles
    lvl = jnp.asarray(_pair_levels(t))
    cur = lambda i: jnp.minimum(i, n_tiles - 1)
    rows = lambda w: pl.BlockSpec((t, w), lambda i: (cur(i), 0))
    vt_spec = pl.BlockSpec((1, sb_w, t),
                           lambda i: (cur(i) // tiles_per_seq, 0, cur(i) % tiles_per_seq))
    sds = lambda w, dt: jax.ShapeDtypeStruct((n, w), dt)
    moving = t * (d * 4 + 2 * sb_w * 2 + sb_w * 2 + 2 * d * 4 + hg_w * 2)
    resident = d * 4 + d * cols * 2 + 3 * hg_w * 4 + t * t * 4 \
        + 2 * t * hg_w * (4 + 3 * 2) + HG_HEADS * HG_HEAD_DIM * HG_HEAD_DIM * 4
    return pl.pallas_call(
        functools.partial(_proj_hgrn2_kernel, tiles_per_seq=tiles_per_seq),
        grid=(n_tiles + 1,),
        in_specs=[rows(d), _resident((1, d)), _resident((d, cols)),
                  _resident(lb_logits.shape), _resident(ng.shape), _resident((t, t))],
        out_specs=[rows(sb_w), rows(sb_w), vt_spec, rows(2 * d),
                   pl.BlockSpec((t, hg_w), lambda i: (jnp.maximum(i - 1, 0), 0))],
        out_shape=[sds(sb_w, BF16), sds(sb_w, BF16),
                   jax.ShapeDtypeStruct((batch, sb_w, n // batch), BF16),
                   sds(2 * d, F32), sds(hg_w, BF16)],
        scratch_shapes=[pltpu.VMEM((2, t, hg_w), F32),
                        pltpu.VMEM((2, t, 3 * hg_w), BF16),
                        pltpu.VMEM((HG_HEADS, HG_HEAD_DIM, HG_HEAD_DIM), F32)],
        compiler_params=pltpu.CompilerParams(
            dimension_semantics=("arbitrary",),
            vmem_limit_bytes=_vmem_limit(moving, resident,
                                         t * (d * 6 + cols * 4) + 64 * t * t * 4)),
        name="proj_hgrn2",
    )(x2, g1, w_in, lb_logits, ng, lvl)


def _attn_ffn_kernel(q_ref, k_ref, vt_ref, x_ref, ohg_ref, gate_ref, bg_ref, wsb_ref, whg_ref,
                     wout_ref, g2_ref, w1_ref, w2_ref, gf_ref, o_ref,
                     osb_buf, qm_ref, acc_ref, carry_ref, *, tiles_per_seq, n_tiles):
    t = q_ref.shape[1]
    d = x_ref.shape[1]
    heads = range(SB_HEADS)
    i = pl.program_id(0)
    slot = i % 2
    qi = jnp.minimum(i, n_tiles - 1) % tiles_per_seq

    @pl.when(i == 0)
    def _():
        osb_buf[1] = jnp.zeros(osb_buf.shape[1:], osb_buf.dtype)

    row = lax.broadcasted_iota(jnp.int32, (t, t), 0)
    col = lax.broadcasted_iota(jnp.int32, (t, t), 1)
    tri = (col > row).astype(BF16)
    causal = row < col
    lane = lax.broadcasted_iota(jnp.int32, (t, V7X_LANES), 1)
    zero = jnp.zeros((), BF16)
    q = q_ref[0]
    for h in heads:
        grp = q[:, (h // 2) * V7X_LANES:(h // 2 + 1) * V7X_LANES]
        qm_ref[h] = jnp.where((lane // SB_HEAD_DIM) == (h % 2), grp, zero)

    def logits(j):
        start = pl.multiple_of(j * t, t)
        k = k_ref[0, pl.ds(start, t), :]
        return [_dot_nt(k[:, (h // 2) * V7X_LANES:(h // 2 + 1) * V7X_LANES], qm_ref[h])
                for h in heads]

    def values_t(j, h):
        return vt_ref[0, h * SB_HEAD_DIM:(h + 1) * SB_HEAD_DIM, pl.ds(pl.multiple_of(j * t, t), t)]

    def softplus_phase(z, mask):
        sp = [_softplus2(z[h], mask) for h in heads]
        return sp, [sp[h].astype(BF16) for h in heads]

    def cumsum_phase(spb):
        return [_dot(tri, spb[h]) for h in heads]

    def weight_phase(z, sp, later, mask):
        w = [jnp.exp2(z[h] - sp[h] - later[h]) for h in heads]
        if mask is not None:
            w = [jnp.where(mask, w[h], 0.0) for h in heads]
        return [w[h].astype(BF16) for h in heads]

    def value_phase(j, w, later, spb):
        pv = [_dot(values_t(j, h), w[h]) for h in heads]
        return pv, [later[h][0:1, :] + spb[h][0:1, :].astype(F32) for h in heads]

    has_prev = qi > 0
    jp = jnp.maximum(qi - 1, 0)
    a_sb = _dot(osb_buf[1 - slot], wsb_ref[...])
    a_hg = _dot(ohg_ref[...], whg_ref[...])
    z0 = logits(qi)
    gates = 1.0 / (1.0 + jnp.exp(-(gate_ref[...] + bg_ref[...])))
    merged = (gates[:, :d] * a_sb + gates[:, d:] * a_hg).astype(BF16)
    hres = x_ref[...] + _dot(merged, wout_ref[...])
    z1 = logits(jp)
    hn = _rms(hres, g2_ref[...]).astype(BF16)
    half = w1_ref.shape[1] // 2

    def mlp_up(lo):
        act = jnp.maximum(_dot(hn, w1_ref[:, lo:lo + half]), 0.0)
        return (act * act).astype(BF16)

    act_a = mlp_up(0)
    sp0, spb0 = softplus_phase(z0, causal)
    lat0 = cumsum_phase(spb0)
    act_b = mlp_up(half)
    sp1, spb1 = softplus_phase(z1, None)
    lat1 = cumsum_phase(spb1)
    hres = hres + _dot(act_a, w2_ref[0:half, :])
    w0 = weight_phase(z0, sp0, lat0, causal)
    w1 = weight_phase(z1, sp1, lat1, None)
    hres = hres + _dot(act_b, w2_ref[half:2 * half, :])
    pv0, tot0 = value_phase(qi, w0, lat0, spb0)
    pv1, tot1 = value_phase(jp, w1, lat1, spb1)
    o_ref[...] = _rms(hres, gf_ref[...]).astype(o_ref.dtype)
    for h in heads:
        scale = jnp.where(has_prev, jnp.exp2(-tot0[h]), 0.0)
        acc_ref[h] = pv0[h] + scale * pv1[h]
        carry_ref[h:h + 1, :] = tot0[h] + jnp.where(has_prev, tot1[h], 0.0)

    def more(state):
        n, live = state
        return jnp.logical_and(n < qi, live)

    def body(state):
        n, _ = state
        j = qi - 1 - n
        z = logits(j)
        sp, spb = softplus_phase(z, None)
        later = cumsum_phase(spb)
        pv, tot = value_phase(j, weight_phase(z, sp, later, None), later, spb)
        for h in heads:
            c = carry_ref[h:h + 1, :]
            acc_ref[h] += jnp.exp2(-c) * pv[h]
            carry_ref[h:h + 1, :] = c + tot[h]
        return n + 1, jnp.min(carry_ref[...]) < SB_DEAD_CARRY

    lax.while_loop(more, body, (jnp.int32(1), jnp.min(carry_ref[...]) < SB_DEAD_CARRY))
    osb_buf[slot] = acc_ref[...].reshape(SB_HEADS * SB_HEAD_DIM, t).T.astype(osb_buf.dtype)


def _attn_ffn(q3, k3, vt3, x2, ohg, gates, bg, wsb, whg, wout, g2, w1, w2, gf):
    b, s, sb_w = q3.shape
    n, d = x2.shape
    t = ATTN_TILE
    dff = w1.shape[1]
    tiles_per_seq = s // t
    n_tiles = n // t
    assert sb_w == SB_HEADS * SB_HEAD_DIM and tiles_per_seq * t == s and n_tiles == b * tiles_per_seq
    cur = lambda i: jnp.minimum(i, n_tiles - 1)
    prv = lambda i: (jnp.maximum(i - 1, 0), 0)
    seq = lambda i: cur(i) // tiles_per_seq
    whole_seq = lambda shape: pl.BlockSpec(shape, lambda i: (seq(i), 0, 0),
                                           pipeline_mode=pl.Buffered(1))
    full = lambda a: _resident(a.shape)
    resident_bytes = sum(a.size * a.dtype.itemsize for a in (bg, wsb, whg, wout, g2, w1, w2, gf)) \
        + 2 * s * sb_w * 2 + 2 * t * sb_w * 2 + SB_HEADS * t * (V7X_LANES * 2 + SB_HEAD_DIM * 4 + 4)
    moving = t * (sb_w * 2 + d * 4 + ohg.shape[1] * 2 + 2 * d * 4 + d * 4)
    return pl.pallas_call(
        functools.partial(_attn_ffn_kernel, tiles_per_seq=tiles_per_seq, n_tiles=n_tiles),
        grid=(n_tiles + 1,),
        in_specs=[pl.BlockSpec((1, t, sb_w), lambda i: (seq(i), cur(i) % tiles_per_seq, 0)),
                  whole_seq((1, s, sb_w)), whole_seq((1, sb_w, s)),
                  pl.BlockSpec((t, d), prv),
                  pl.BlockSpec((t, ohg.shape[1]), prv),
                  pl.BlockSpec((t, 2 * d), prv),
                  full(bg), full(wsb), full(whg), full(wout), full(g2), full(w1), full(w2),
                  full(gf)],
        out_specs=pl.BlockSpec((t, d), prv),
        out_shape=jax.ShapeDtypeStruct((n, d), x2.dtype),
        scratch_shapes=[pltpu.VMEM((2, t, sb_w), BF16),
                        pltpu.VMEM((SB_HEADS, t, V7X_LANES), BF16),
                        pltpu.VMEM((SB_HEADS, SB_HEAD_DIM, t), F32),
                        pltpu.VMEM((SB_HEADS, t), F32)],
        compiler_params=pltpu.CompilerParams(
            dimension_semantics=("arbitrary",),
            vmem_limit_bytes=_vmem_limit(moving, resident_bytes,
                                         t * (dff * 6 + d * 24) + 8 * SB_HEADS * t * t * 4)),
        name="attn_ffn",
    )(q3, k3, vt3, x2, ohg, gates, bg, wsb, whg, wout, g2, w1, w2, gf)


def kernel(x, norm1_g, w_in, b_gate, lb_logits, hg_norm_g, w_o_sb, w_o_hg, w_out, norm2_g,
           w_ff1, w_ff2, final_g):
    b, s, d = x.shape
    assert w_in.shape[0] == 1, "single-layer block"
    sb_w = SB_HEADS * SB_HEAD_DIM
    hg_w = HG_HEADS * HG_HEAD_DIM
    x2 = x.reshape(b * s, d)
    q, k, vt, gates, o_hg = _proj_hgrn2(x2, norm1_g, w_in[0].astype(BF16), lb_logits,
                                        hg_norm_g, b, sb_w, hg_w)
    out = _attn_ffn(q.reshape(b, s, sb_w), k.reshape(b, s, sb_w), vt, x2, o_hg, gates, b_gate,
                    w_o_sb[0].astype(BF16), w_o_hg[0].astype(BF16), w_out[0].astype(BF16),
                    norm2_g, w_ff1[0].astype(BF16), w_ff2[0].astype(BF16), final_g.reshape(1, d))
    return out.reshape(b, s, d)
```

```python
import functools

import jax
import jax.numpy as jnp
import numpy as np
from jax import lax
from jax.experimental import pallas as pl
from jax.experimental.pallas import tpu as pltpu

F32 = jnp.float32
BF16 = jnp.bfloat16

SB_HEADS = 8
SB_HEAD_DIM = 64
HG_HEADS = 4
HG_HEAD_DIM = 128
EPS = 1e-6
LOG2E = 1.4426950408889634
SB_DEAD_CARRY = 151.0

V7X_LANES = 128
V7X_MXU_DIM = 256
V7X_VMEM_BYTES = 64 * 1024 * 1024

ATTN_TILE = V7X_MXU_DIM
HG_BLOCK = V7X_MXU_DIM


def _vmem_limit(pipelined_bytes, resident_bytes, temp_bytes):
    need = 2 * pipelined_bytes + resident_bytes + temp_bytes
    return int(min(need + need // 4, V7X_VMEM_BYTES - 8 * 1024 * 1024))


def _resident(shape):
    return pl.BlockSpec(shape, lambda *_: (0,) * len(shape), pipeline_mode=pl.Buffered(1))


def _rms(x, g):
    ms = jnp.mean(x * x, axis=-1, keepdims=True)
    return x * lax.rsqrt(ms + EPS) * g


def _dot(a, b):
    return jnp.dot(a, b, preferred_element_type=F32)


def _dot_nt(a, b):
    return lax.dot_general(a, b, (((1,), (1,)), ((), ())), preferred_element_type=F32)


def _dot_tn(a, b):
    return lax.dot_general(a, b, (((0,), (0,)), ((), ())), preferred_element_type=F32)


def _softplus2(z, mask):
    sp = jnp.maximum(z, 0.0) + jnp.log(1.0 + jnp.exp2(-jnp.abs(z))) * LOG2E
    return sp if mask is None else jnp.where(mask, sp, 0.0)


def _split3(x):
    a = x.astype(BF16)
    r = x - a.astype(F32)
    b = r.astype(BF16)
    c = (r - b.astype(F32)).astype(BF16)
    return a, b, c


def _rows_from_group(b, group, r):
    n, c = b.shape
    if group == n:
        return jnp.broadcast_to(b[r:r + 1, :], (n, c))
    b3 = b.reshape(n // group, group, c)
    return jnp.broadcast_to(b3[:, r:r + 1, :], b3.shape).reshape(n, c)


def _midpoint_rows(b, group, pos):
    n = b.shape[0]
    half = group // 2
    if group >= 16:
        return _rows_from_group(b, group, half - 1)
    if group == 8:
        return _rows_from_group(b, 8, 3)
    up1 = pltpu.roll(b, n - 1, 0)
    dn1 = pltpu.roll(b, 1, 0)
    if group == 2:
        return jnp.where((pos & 1) == 0, b, dn1)
    assert group == 4
    dn2 = pltpu.roll(b, 2, 0)
    r4 = pos & 3
    return jnp.where(r4 == 0, up1, jnp.where(r4 == 1, b, jnp.where(r4 == 2, dn1, dn2)))


def _hgrn2_gates(fr, qr, lb):
    e = jnp.exp(-jnp.abs(fr))
    r = 1.0 / (1.0 + e)
    er = e * r
    sig = jnp.where(fr >= 0, r, er)
    nsig = jnp.where(fr >= 0, er, r)
    logf2 = jnp.log(lb + (1.0 - lb) * sig) * LOG2E
    k = (1.0 - lb) * nsig
    q = qr / (1.0 + jnp.exp(-qr))
    return logf2, k, q


def _hgrn2_head(b, q, k, v, gr, ng, lvl, pos, st_ref, h, keep):
    t, dk = b.shape
    n_levels = t.bit_length() - 1
    hb = t // 2
    zeros = jnp.zeros((hb, dk), BF16)
    lvl_d = jnp.concatenate([lvl[0:hb, 0:hb], lvl[hb:t, hb:t]], axis=1)
    diag = jnp.zeros((hb, t), F32)
    for level in range(1, n_levels):
        group = 1 << level
        d = b - _midpoint_rows(b, group, pos)
        later = (pos & (group - 1)) >= (group // 2)
        fac = jnp.exp2(-jnp.abs(d))
        ql = jnp.where(later, q * fac, 0.0).astype(BF16)
        kl = jnp.where(later, 0.0, k * fac).astype(BF16)
        lhs = jnp.concatenate([ql[0:hb], ql[hb:t]], axis=1)
        rhs = jnp.concatenate([jnp.concatenate([kl[0:hb], zeros], axis=1),
                               jnp.concatenate([zeros, kl[hb:t]], axis=1)], axis=0)
        diag = jnp.where(lvl_d == level, _dot_nt(lhs, rhs), diag)
    b_mid = b[hb - 1:hb, :]
    q_top = (q[hb:t] * jnp.exp2(b[hb:t] - b_mid)).astype(BF16)
    k_top = (k[0:hb] * jnp.exp2(b_mid - b[0:hb])).astype(BF16)
    top = _dot_nt(q_top, k_top)
    scores = jnp.concatenate(
        [jnp.concatenate([diag[:, 0:hb], jnp.zeros((hb, hb), F32)], axis=1),
         jnp.concatenate([top, diag[:, hb:t]], axis=1)], axis=0)

    st = st_ref[h] * keep
    b_last = b[t - 1:t, :]
    o = _dot(scores.astype(BF16), v)
    o = o + _dot_nt((q * jnp.exp2(b)).astype(BF16), st.astype(BF16))
    o = o + jnp.sum(q * k, axis=1, keepdims=True) * v.astype(F32)
    k_dec = (k * jnp.exp2(b_last - b)).astype(BF16)
    st_ref[h] = jnp.exp2(b_last) * st + _dot_tn(v, k_dec)
    return _rms(o, ng) * (gr / (1.0 + jnp.exp(-gr)))


def _proj_hgrn2_kernel(x0_ref, xnext_ref, g_ref, w_ref, lbl_ref, ng_ref, lvl_ref,
                       q_ref, k_ref, vt_ref, gate_ref, ohg_ref, xn_buf, f_buf, iqg_buf, st_ref,
                       *, tiles_per_seq):
    t = xnext_ref.shape[0]
    d = xnext_ref.shape[1]
    dk = HG_HEAD_DIM
    hg_w = HG_HEADS * dk
    sb_w = q_ref.shape[1]
    i = pl.program_id(0)
    slot = i % 2
    prev = 1 - slot

    @pl.when(i == 0)
    def _():
        f_buf[1] = jnp.zeros(f_buf.shape[1:], f_buf.dtype)
        iqg_buf[1] = jnp.zeros(iqg_buf.shape[1:], iqg_buf.dtype)
        st_ref[...] = jnp.zeros_like(st_ref)
        xn_buf[0] = _rms(x0_ref[...], g_ref[...]).astype(BF16)

    xn = xn_buf[slot]

    row = lax.broadcasted_iota(jnp.int32, (t, t), 0)
    col = lax.broadcasted_iota(jnp.int32, (t, t), 1)
    tril = (row >= col).astype(BF16)
    pos = lax.broadcasted_iota(jnp.int32, (t, dk), 0)
    lvl = lvl_ref[...]
    keep = jnp.where(i % tiles_per_seq == 1, 0.0, 1.0)

    lbl = lbl_ref[...]
    ex = jnp.exp(lbl - jnp.max(lbl, axis=0, keepdims=True))
    lb_all = ex[0:1, :] / jnp.sum(ex, axis=0, keepdims=True)

    def project(lo, hi):
        return _dot(xn, w_ref[:, lo:hi])

    def finish_head(h):
        sl = slice(h * dk, (h + 1) * dk)
        y = _hgrn2_head(b_all[:, sl], q_all[:, sl], k_all[:, sl], iqg_buf[prev, :, sl],
                        iqg_buf[prev, :, 2 * hg_w + h * dk:2 * hg_w + (h + 1) * dk].astype(F32),
                        ng_ref[:, sl], lvl, pos, st_ref, h, keep)
        ohg_ref[:, sl] = y.astype(ohg_ref.dtype)

    c0 = 3 * sb_w
    c1 = c0 + hg_w
    c2 = c1 + 3 * hg_w
    q_ref[...] = (project(0, sb_w) * (SB_HEAD_DIM ** -0.5 * LOG2E)).astype(q_ref.dtype)
    k_ref[...] = project(sb_w, 2 * sb_w).astype(k_ref.dtype)
    logf2, k_all, q_all = _hgrn2_gates(f_buf[prev], iqg_buf[prev, :, hg_w:2 * hg_w].astype(F32),
                                       lb_all)
    g1, g2, g3 = _split3(logf2)
    b_all = _dot(tril, g1) + _dot(tril, g2) + _dot(tril, g3)
    vt_ref[0] = project(2 * sb_w, 3 * sb_w).T.astype(vt_ref.dtype)
    f_buf[slot] = project(c0, c1)
    finish_head(0)
    iqg_buf[slot] = project(c1, c2).astype(iqg_buf.dtype)
    finish_head(1)
    gate_ref[:, 0:d] = project(c2, c2 + d)
    finish_head(2)
    gate_ref[:, d:d + d // 2] = project(c2 + d, c2 + d + d // 2)
    finish_head(3)
    gate_ref[:, d + d // 2:2 * d] = project(c2 + d + d // 2, c2 + 2 * d)
    xn_buf[prev] = _rms(xnext_ref[...], g_ref[...]).astype(BF16)


def _pair_levels(t):
    idx = np.arange(t)
    x = idx[:, None] ^ idx[None, :]
    lev = np.where(x > 0, np.floor(np.log2(np.maximum(x, 1))).astype(np.int64) + 1, 0)
    return np.where(idx[:, None] > idx[None, :], lev, 0).astype(np.int32)


def _proj_hgrn2(x2, g1, w_in, lb_logits, ng, batch, sb_w, hg_w):
    n, d = x2.shape
    t = HG_BLOCK
    cols = w_in.shape[1]
    n_tiles = n // t
    tiles_per_seq = n_tiles // batch
    assert hg_w == HG_HEADS * HG_HEAD_DIM and n_tiles * t == n and tiles_per_seq * batch == n_tiles
    lvl = jnp.asarray(_pair_levels(t))
    cur = lambda i: jnp.minimum(i, n_tiles - 1)
    rows = lambda w: pl.BlockSpec((t, w), lambda i: (cur(i), 0))
    vt_spec = pl.BlockSpec((1, sb_w, t),
                           lambda i: (cur(i) // tiles_per_seq, 0, cur(i) % tiles_per_seq))
    sds = lambda w, dt: jax.ShapeDtypeStruct((n, w), dt)
    moving = t * (d * 4 + 2 * sb_w * 2 + sb_w * 2 + 2 * d * 4 + hg_w * 2)
    resident = d * 4 + d * cols * 2 + 3 * hg_w * 4 + t * t * 4 + t * d * 4 + 2 * t * d * 2 \
        + 2 * t * hg_w * (4 + 3 * 2) + HG_HEADS * HG_HEAD_DIM * HG_HEAD_DIM * 4
    return pl.pallas_call(
        functools.partial(_proj_hgrn2_kernel, tiles_per_seq=tiles_per_seq),
        grid=(n_tiles + 1,),
        in_specs=[_resident((t, d)), pl.BlockSpec((t, d), lambda i: (cur(i + 1), 0)),
                  _resident((1, d)), _resident((d, cols)),
                  _resident(lb_logits.shape), _resident(ng.shape), _resident((t, t))],
        out_specs=[rows(sb_w), rows(sb_w), vt_spec, rows(2 * d),
                   pl.BlockSpec((t, hg_w), lambda i: (jnp.maximum(i - 1, 0), 0))],
        out_shape=[sds(sb_w, BF16), sds(sb_w, BF16),
                   jax.ShapeDtypeStruct((batch, sb_w, n // batch), BF16),
                   sds(2 * d, F32), sds(hg_w, BF16)],
        scratch_shapes=[pltpu.VMEM((2, t, d), BF16),
                        pltpu.VMEM((2, t, hg_w), F32),
                        pltpu.VMEM((2, t, 3 * hg_w), BF16),
                        pltpu.VMEM((HG_HEADS, HG_HEAD_DIM, HG_HEAD_DIM), F32)],
        compiler_params=pltpu.CompilerParams(
            dimension_semantics=("arbitrary",),
            vmem_limit_bytes=_vmem_limit(moving, resident,
                                         t * (d * 6 + cols * 4) + 64 * t * t * 4)),
        name="proj_hgrn2",
    )(x2, x2, g1, w_in, lb_logits, ng, lvl)


def _attn_ffn_kernel(q_ref, k_ref, vt_ref, x_ref, ohg_ref, gate_ref, bg_ref, wsb_ref, whg_ref,
                     wout_ref, g2_ref, w1_ref, w2_ref, gf_ref, o_ref,
                     osb_buf, qm_ref, acc_ref, carry_ref, *, tiles_per_seq, n_tiles):
    t = q_ref.shape[1]
    d = x_ref.shape[1]
    heads = range(SB_HEADS)
    i = pl.program_id(0)
    slot = i % 2
    qi = jnp.minimum(i, n_tiles - 1) % tiles_per_seq

    @pl.when(i == 0)
    def _():
        osb_buf[1] = jnp.zeros(osb_buf.shape[1:], osb_buf.dtype)

    row = lax.broadcasted_iota(jnp.int32, (t, t), 0)
    col = lax.broadcasted_iota(jnp.int32, (t, t), 1)
    tri = (col > row).astype(BF16)
    causal = row < col
    lane = lax.broadcasted_iota(jnp.int32, (t, V7X_LANES), 1)
    zero = jnp.zeros((), BF16)
    q = q_ref[0]
    for h in heads:
        grp = q[:, (h // 2) * V7X_LANES:(h // 2 + 1) * V7X_LANES]
        qm_ref[h] = jnp.where((lane // SB_HEAD_DIM) == (h % 2), grp, zero)

    def logits(j):
        start = pl.multiple_of(j * t, t)
        k = k_ref[0, pl.ds(start, t), :]
        return [_dot_nt(k[:, (h // 2) * V7X_LANES:(h // 2 + 1) * V7X_LANES], qm_ref[h])
                for h in heads]

    def values_t(j, h):
        return vt_ref[0, h * SB_HEAD_DIM:(h + 1) * SB_HEAD_DIM, pl.ds(pl.multiple_of(j * t, t), t)]

    def softplus_phase(z, mask):
        sp = [_softplus2(z[h], mask) for h in heads]
        return sp, [sp[h].astype(BF16) for h in heads]

    def cumsum_phase(spb):
        return [_dot(tri, spb[h]) for h in heads]

    def weight_phase(z, sp, later, mask):
        w = [jnp.exp2(z[h] - sp[h] - later[h]) for h in heads]
        if mask is not None:
            w = [jnp.where(mask, w[h], 0.0) for h in heads]
        return [w[h].astype(BF16) for h in heads]

    def value_phase(j, w, later, spb):
        pv = [_dot(values_t(j, h), w[h]) for h in heads]
        return pv, [later[h][0:1, :] + spb[h][0:1, :].astype(F32) for h in heads]

    has_prev = qi > 0
    jp = jnp.maximum(qi - 1, 0)
    a_sb = _dot_tn(osb_buf[1 - slot], wsb_ref[...])
    a_hg = _dot(ohg_ref[...], whg_ref[...])
    z0 = logits(qi)
    gates = 1.0 / (1.0 + jnp.exp(-(gate_ref[...] + bg_ref[...])))
    merged = (gates[:, :d] * a_sb + gates[:, d:] * a_hg).astype(BF16)
    hres = x_ref[...] + _dot(merged, wout_ref[...])
    z1 = logits(jp)
    hn = _rms(hres, g2_ref[...]).astype(BF16)
    half = w1_ref.shape[1] // 2

    def mlp_up(lo):
        act = jnp.maximum(_dot(hn, w1_ref[:, lo:lo + half]), 0.0)
        return (act * act).astype(BF16)

    act_a = mlp_up(0)
    sp0, spb0 = softplus_phase(z0, causal)
    lat0 = cumsum_phase(spb0)
    act_b = mlp_up(half)
    sp1, spb1 = softplus_phase(z1, None)
    lat1 = cumsum_phase(spb1)
    hres = hres + _dot(act_a, w2_ref[0:half, :])
    w0 = weight_phase(z0, sp0, lat0, causal)
    w1 = weight_phase(z1, sp1, lat1, None)
    hres = hres + _dot(act_b, w2_ref[half:2 * half, :])
    pv0, tot0 = value_phase(qi, w0, lat0, spb0)
    pv1, tot1 = value_phase(jp, w1, lat1, spb1)
    o_ref[...] = _rms(hres, gf_ref[...]).astype(o_ref.dtype)
    for h in heads:
        scale = jnp.where(has_prev, jnp.exp2(-tot0[h]), 0.0)
        acc_ref[h] = pv0[h] + scale * pv1[h]
        carry_ref[h:h + 1, :] = tot0[h] + jnp.where(has_prev, tot1[h], 0.0)

    def more(state):
        n, live = state
        return jnp.logical_and(n < qi, live)

    def body(state):
        n, _ = state
        j = qi - 1 - n
        z = logits(j)
        sp, spb = softplus_phase(z, None)
        later = cumsum_phase(spb)
        pv, tot = value_phase(j, weight_phase(z, sp, later, None), later, spb)
        for h in heads:
            c = carry_ref[h:h + 1, :]
            acc_ref[h] += jnp.exp2(-c) * pv[h]
            carry_ref[h:h + 1, :] = c + tot[h]
        return n + 1, jnp.min(carry_ref[...]) < SB_DEAD_CARRY

    lax.while_loop(more, body, (jnp.int32(1), jnp.min(carry_ref[...]) < SB_DEAD_CARRY))
    osb_buf[slot] = acc_ref[...].reshape(SB_HEADS * SB_HEAD_DIM, t).astype(osb_buf.dtype)


def _attn_ffn(q3, k3, vt3, x2, ohg, gates, bg, wsb, whg, wout, g2, w1, w2, gf):
    b, s, sb_w = q3.shape
    n, d = x2.shape
    t = ATTN_TILE
    dff = w1.shape[1]
    tiles_per_seq = s // t
    n_tiles = n // t
    assert sb_w == SB_HEADS * SB_HEAD_DIM and tiles_per_seq * t == s and n_tiles == b * tiles_per_seq
    cur = lambda i: jnp.minimum(i, n_tiles - 1)
    prv = lambda i: (jnp.maximum(i - 1, 0), 0)
    seq = lambda i: cur(i) // tiles_per_seq
    whole_seq = lambda shape: pl.BlockSpec(shape, lambda i: (seq(i), 0, 0),
                                           pipeline_mode=pl.Buffered(1))
    full = lambda a: _resident(a.shape)
    resident_bytes = sum(a.size * a.dtype.itemsize for a in (bg, wsb, whg, wout, g2, w1, w2, gf)) \
        + 2 * s * sb_w * 2 + 2 * t * sb_w * 2 + SB_HEADS * t * (V7X_LANES * 2 + SB_HEAD_DIM * 4 + 4)
    moving = t * (sb_w * 2 + d * 4 + ohg.shape[1] * 2 + 2 * d * 4 + d * 4)
    return pl.pallas_call(
        functools.partial(_attn_ffn_kernel, tiles_per_seq=tiles_per_seq, n_tiles=n_tiles),
        grid=(n_tiles + 1,),
        in_specs=[pl.BlockSpec((1, t, sb_w), lambda i: (seq(i), cur(i) % tiles_per_seq, 0)),
                  whole_seq((1, s, sb_w)), whole_seq((1, sb_w, s)),
                  pl.BlockSpec((t, d), prv),
                  pl.BlockSpec((t, ohg.shape[1]), prv),
                  pl.BlockSpec((t, 2 * d), prv),
                  full(bg), full(wsb), full(whg), full(wout), full(g2), full(w1), full(w2),
                  full(gf)],
        out_specs=pl.BlockSpec((t, d), prv),
        out_shape=jax.ShapeDtypeStruct((n, d), x2.dtype),
        scratch_shapes=[pltpu.VMEM((2, sb_w, t), BF16),
                        pltpu.VMEM((SB_HEADS, t, V7X_LANES), BF16),
                        pltpu.VMEM((SB_HEADS, SB_HEAD_DIM, t), F32),
                        pltpu.VMEM((SB_HEADS, t), F32)],
        compiler_params=pltpu.CompilerParams(
            dimension_semantics=("arbitrary",),
            vmem_limit_bytes=_vmem_limit(moving, resident_bytes,
                                         t * (dff * 6 + d * 24) + 8 * SB_HEADS * t * t * 4)),
        name="attn_ffn",
    )(q3, k3, vt3, x2, ohg, gates, bg, wsb, whg, wout, g2, w1, w2, gf)


def kernel(x, norm1_g, w_in, b_gate, lb_logits, hg_norm_g, w_o_sb, w_o_hg, w_out, norm2_g,
           w_ff1, w_ff2, final_g):
    b, s, d = x.shape
    assert w_in.shape[0] == 1, "single-layer block"
    sb_w = SB_HEADS * SB_HEAD_DIM
    hg_w = HG_HEADS * HG_HEAD_DIM
    x2 = x.reshape(b * s, d)
    q, k, vt, gates, o_hg = _proj_hgrn2(x2, norm1_g, w_in[0].astype(BF16), lb_logits,
                                        hg_norm_g, b, sb_w, hg_w)
    out = _attn_ffn(q.reshape(b, s, sb_w), k.reshape(b, s, sb_w), vt, x2, o_hg, gates, b_gate,
                    w_o_sb[0].astype(BF16), w_o_hg[0].astype(BF16), w_out[0].astype(BF16),
                    norm2_g, w_ff1[0].astype(BF16), w_ff2[0].astype(BF16), final_g.reshape(1, d))
    return out.reshape(b, s, d)
```

```python
import functools

import jax
import jax.numpy as jnp
import numpy as np
from jax import lax
from jax.experimental import pallas as pl
from jax.experimental.pallas import tpu as pltpu

F32 = jnp.float32
BF16 = jnp.bfloat16

SB_HEADS = 8
SB_HEAD_DIM = 64
HG_HEADS = 4
HG_HEAD_DIM = 128
EPS = 1e-6
LOG2E = 1.4426950408889634
SB_DEAD_CARRY = 151.0

V7X_LANES = 128
V7X_MXU_DIM = 256
V7X_VMEM_BYTES = 64 * 1024 * 1024

ATTN_TILE = V7X_MXU_DIM
HG_BLOCK = V7X_MXU_DIM
PROJ_BLOCKS = 2


def _vmem_limit(pipelined_bytes, resident_bytes, temp_bytes):
    need = 2 * pipelined_bytes + resident_bytes + temp_bytes
    return int(min(need + need // 4, V7X_VMEM_BYTES - 8 * 1024 * 1024))


def _resident(shape):
    return pl.BlockSpec(shape, lambda *_: (0,) * len(shape), pipeline_mode=pl.Buffered(1))


def _rms(x, g):
    ms = jnp.mean(x * x, axis=-1, keepdims=True)
    return x * lax.rsqrt(ms + EPS) * g


def _dot(a, b):
    return jnp.dot(a, b, preferred_element_type=F32)


def _dot_nt(a, b):
    return lax.dot_general(a, b, (((1,), (1,)), ((), ())), preferred_element_type=F32)


def _dot_tn(a, b):
    return lax.dot_general(a, b, (((0,), (0,)), ((), ())), preferred_element_type=F32)


def _softplus2(z, mask):
    sp = jnp.maximum(z, 0.0) + jnp.log(1.0 + jnp.exp2(-jnp.abs(z))) * LOG2E
    return sp if mask is None else jnp.where(mask, sp, 0.0)


def _split3(x):
    a = x.astype(BF16)
    r = x - a.astype(F32)
    b = r.astype(BF16)
    c = (r - b.astype(F32)).astype(BF16)
    return a, b, c


def _rows_from_group(b, group, r):
    n, c = b.shape
    if group == n:
        return jnp.broadcast_to(b[r:r + 1, :], (n, c))
    b3 = b.reshape(n // group, group, c)
    return jnp.broadcast_to(b3[:, r:r + 1, :], b3.shape).reshape(n, c)


def _midpoint_rows(b, group, pos):
    n = b.shape[0]
    half = group // 2
    if group >= 16:
        return _rows_from_group(b, group, half - 1)
    if group == 8:
        return _rows_from_group(b, 8, 3)
    up1 = pltpu.roll(b, n - 1, 0)
    dn1 = pltpu.roll(b, 1, 0)
    if group == 2:
        return jnp.where((pos & 1) == 0, b, dn1)
    assert group == 4
    dn2 = pltpu.roll(b, 2, 0)
    r4 = pos & 3
    return jnp.where(r4 == 0, up1, jnp.where(r4 == 1, b, jnp.where(r4 == 2, dn1, dn2)))


def _hgrn2_gates(fr, qr, lb):
    e = jnp.exp(-jnp.abs(fr))
    r = 1.0 / (1.0 + e)
    er = e * r
    sig = jnp.where(fr >= 0, r, er)
    nsig = jnp.where(fr >= 0, er, r)
    logf2 = jnp.log(lb + (1.0 - lb) * sig) * LOG2E
    k = (1.0 - lb) * nsig
    q = qr / (1.0 + jnp.exp(-qr))
    return logf2, k, q


def _hgrn2_head(b, q, k, v, gr, ng, lvl, pos, st_ref, h, keep):
    t, dk = b.shape
    n_levels = t.bit_length() - 1
    hb = t // 2
    zeros = jnp.zeros((hb, dk), BF16)
    lvl_d = jnp.concatenate([lvl[0:hb, 0:hb], lvl[hb:t, hb:t]], axis=1)
    diag = jnp.zeros((hb, t), F32)
    for level in range(1, n_levels):
        group = 1 << level
        d = b - _midpoint_rows(b, group, pos)
        later = (pos & (group - 1)) >= (group // 2)
        fac = jnp.exp2(-jnp.abs(d))
        ql = jnp.where(later, q * fac, 0.0).astype(BF16)
        kl = jnp.where(later, 0.0, k * fac).astype(BF16)
        lhs = jnp.concatenate([ql[0:hb], ql[hb:t]], axis=1)
        rhs = jnp.concatenate([jnp.concatenate([kl[0:hb], zeros], axis=1),
                               jnp.concatenate([zeros, kl[hb:t]], axis=1)], axis=0)
        diag = jnp.where(lvl_d == level, _dot_nt(lhs, rhs), diag)
    b_mid = b[hb - 1:hb, :]
    q_top = (q[hb:t] * jnp.exp2(b[hb:t] - b_mid)).astype(BF16)
    k_top = (k[0:hb] * jnp.exp2(b_mid - b[0:hb])).astype(BF16)
    top = _dot_nt(q_top, k_top)
    scores = jnp.concatenate(
        [jnp.concatenate([diag[:, 0:hb], jnp.zeros((hb, hb), F32)], axis=1),
         jnp.concatenate([top, diag[:, hb:t]], axis=1)], axis=0)

    st = st_ref[h] * keep
    b_last = b[t - 1:t, :]
    o = _dot(scores.astype(BF16), v)
    o = o + _dot_nt((q * jnp.exp2(b)).astype(BF16), st.astype(BF16))
    o = o + jnp.sum(q * k, axis=1, keepdims=True) * v.astype(F32)
    k_dec = (k * jnp.exp2(b_last - b)).astype(BF16)
    st_ref[h] = jnp.exp2(b_last) * st + _dot_tn(v, k_dec)
    return _rms(o, ng) * (gr / (1.0 + jnp.exp(-gr)))


def _proj_hgrn2_kernel(x0_ref, xnext_ref, g_ref, w_ref, lbl_ref, ng_ref, lvl_ref,
                       q_ref, k_ref, vt_ref, gate_ref, ohg_ref, xn_buf, f_buf, iqg_buf, st_ref,
                       *, tiles_per_seq):
    d = xnext_ref.shape[1]
    t = HG_BLOCK
    n_blocks = xnext_ref.shape[0] // t
    dk = HG_HEAD_DIM
    hg_w = HG_HEADS * dk
    sb_w = q_ref.shape[1]
    i = pl.program_id(0)
    slot = i % 2
    prev = 1 - slot

    @pl.when(i == 0)
    def _():
        f_buf[1] = jnp.zeros(f_buf.shape[1:], f_buf.dtype)
        iqg_buf[1] = jnp.zeros(iqg_buf.shape[1:], iqg_buf.dtype)
        st_ref[...] = jnp.zeros_like(st_ref)
        xn_buf[0] = _rms(x0_ref[...], g_ref[...]).astype(BF16)

    xn = xn_buf[slot]

    row = lax.broadcasted_iota(jnp.int32, (t, t), 0)
    col = lax.broadcasted_iota(jnp.int32, (t, t), 1)
    tril = (row >= col).astype(BF16)
    pos = lax.broadcasted_iota(jnp.int32, (t, dk), 0)
    lvl = lvl_ref[...]
    first_keep = jnp.where(i % tiles_per_seq == 1, 0.0, 1.0)

    lbl = lbl_ref[...]
    ex = jnp.exp(lbl - jnp.max(lbl, axis=0, keepdims=True))
    lb_all = ex[0:1, :] / jnp.sum(ex, axis=0, keepdims=True)

    def project(lo, hi):
        return _dot(xn, w_ref[:, lo:hi])

    def prepare_block(blk):
        rows = slice(blk * t, (blk + 1) * t)
        logf2, k_all, q_all = _hgrn2_gates(f_buf[prev, rows, :],
                                           iqg_buf[prev, rows, hg_w:2 * hg_w].astype(F32), lb_all)
        g1, g2, g3 = _split3(logf2)
        return _dot(tril, g1) + _dot(tril, g2) + _dot(tril, g3), q_all, k_all

    def finish_head(blk, h, prepared):
        b_all, q_all, k_all = prepared
        rows = slice(blk * t, (blk + 1) * t)
        sl = slice(h * dk, (h + 1) * dk)
        y = _hgrn2_head(b_all[:, sl], q_all[:, sl], k_all[:, sl], iqg_buf[prev, rows, sl],
                        iqg_buf[prev, rows, 2 * hg_w + h * dk:2 * hg_w + (h + 1) * dk].astype(F32),
                        ng_ref[:, sl], lvl, pos, st_ref, h, first_keep if blk == 0 else 1.0)
        ohg_ref[rows, sl] = y.astype(ohg_ref.dtype)

    c0 = 3 * sb_w
    c1 = c0 + hg_w
    c2 = c1 + 3 * hg_w
    slab = d // 2

    def q_slab():
        q_ref[...] = (project(0, sb_w) * (SB_HEAD_DIM ** -0.5 * LOG2E)).astype(q_ref.dtype)

    def k_slab():
        k_ref[...] = project(sb_w, 2 * sb_w).astype(k_ref.dtype)

    def v_slab():
        v = project(2 * sb_w, 3 * sb_w)
        for blk in range(n_blocks):
            vt_ref[0, :, blk * t:(blk + 1) * t] = v[blk * t:(blk + 1) * t].T.astype(vt_ref.dtype)

    def f_slab():
        f_buf[slot] = project(c0, c1)

    def iqg_slab(lo):
        def run():
            iqg_buf[slot, :, lo:lo + slab] = project(c1 + lo, c1 + lo + slab).astype(iqg_buf.dtype)
        return run

    def gate_slab(lo):
        def run():
            gate_ref[:, lo:lo + slab] = project(c2 + lo, c2 + lo + slab)
        return run

    slabs = [v_slab, f_slab] + [iqg_slab(lo) for lo in range(0, 3 * hg_w, slab)] \
        + [gate_slab(lo) for lo in range(0, 2 * d, slab)]
    q_slab()
    k_slab()
    prepared = prepare_block(0)
    pieces = [(blk, h) for blk in range(n_blocks) for h in range(HG_HEADS)]
    n_between = min(len(pieces), len(slabs) - 1)
    for n, (blk, h) in enumerate(pieces):
        if n < n_between:
            slabs[n]()
        if blk > 0 and h == 0:
            prepared = prepare_block(blk)
        finish_head(blk, h, prepared)
    for run in slabs[n_between:]:
        run()
    xn_buf[prev] = _rms(xnext_ref[...], g_ref[...]).astype(BF16)


def _pair_levels(t):
    idx = np.arange(t)
    x = idx[:, None] ^ idx[None, :]
    lev = np.where(x > 0, np.floor(np.log2(np.maximum(x, 1))).astype(np.int64) + 1, 0)
    return np.where(idx[:, None] > idx[None, :], lev, 0).astype(np.int32)


def _proj_hgrn2(x2, g1, w_in, lb_logits, ng, batch, sb_w, hg_w):
    n, d = x2.shape
    t = PROJ_BLOCKS * HG_BLOCK
    cols = w_in.shape[1]
    n_tiles = n // t
    tiles_per_seq = n_tiles // batch
    assert hg_w == HG_HEADS * HG_HEAD_DIM and n_tiles * t == n and tiles_per_seq * batch == n_tiles
    lvl = jnp.asarray(_pair_levels(HG_BLOCK))
    cur = lambda i: jnp.minimum(i, n_tiles - 1)
    rows = lambda w: pl.BlockSpec((t, w), lambda i: (cur(i), 0))
    vt_spec = pl.BlockSpec((1, sb_w, t),
                           lambda i: (cur(i) // tiles_per_seq, 0, cur(i) % tiles_per_seq))
    sds = lambda w, dt: jax.ShapeDtypeStruct((n, w), dt)
    moving = t * (d * 4 + 2 * sb_w * 2 + sb_w * 2 + 2 * d * 4 + hg_w * 2)
    resident = d * 4 + d * cols * 2 + 3 * hg_w * 4 + HG_BLOCK * HG_BLOCK * 4 + t * d * 4 \
        + 2 * t * d * 2 + 2 * t * hg_w * (4 + 3 * 2) + HG_HEADS * HG_HEAD_DIM * HG_HEAD_DIM * 4
    return pl.pallas_call(
        functools.partial(_proj_hgrn2_kernel, tiles_per_seq=tiles_per_seq),
        grid=(n_tiles + 1,),
        in_specs=[_resident((t, d)), pl.BlockSpec((t, d), lambda i: (cur(i + 1), 0)),
                  _resident((1, d)), _resident((d, cols)),
                  _resident(lb_logits.shape), _resident(ng.shape),
                  _resident((HG_BLOCK, HG_BLOCK))],
        out_specs=[rows(sb_w), rows(sb_w), vt_spec, rows(2 * d),
                   pl.BlockSpec((t, hg_w), lambda i: (jnp.maximum(i - 1, 0), 0))],
        out_shape=[sds(sb_w, BF16), sds(sb_w, BF16),
                   jax.ShapeDtypeStruct((batch, sb_w, n // batch), BF16),
                   sds(2 * d, F32), sds(hg_w, BF16)],
        scratch_shapes=[pltpu.VMEM((2, t, d), BF16),
                        pltpu.VMEM((2, t, hg_w), F32),
                        pltpu.VMEM((2, t, 3 * hg_w), BF16),
                        pltpu.VMEM((HG_HEADS, HG_HEAD_DIM, HG_HEAD_DIM), F32)],
        compiler_params=pltpu.CompilerParams(
            dimension_semantics=("arbitrary",),
            vmem_limit_bytes=_vmem_limit(moving, resident,
                                         t * (d * 6 + cols * 4) + 64 * HG_BLOCK * HG_BLOCK * 4)),
        name="proj_hgrn2",
    )(x2, x2, g1, w_in, lb_logits, ng, lvl)


def _attn_ffn_kernel(q_ref, k_ref, vt_ref, x_ref, ohg_ref, gate_ref, bg_ref, wsb_ref, whg_ref,
                     wout_ref, g2_ref, w1_ref, w2_ref, gf_ref, o_ref,
                     osb_buf, qm_ref, acc_ref, carry_ref, *, tiles_per_seq, n_tiles):
    t = q_ref.shape[1]
    d = x_ref.shape[1]
    heads = range(SB_HEADS)
    i = pl.program_id(0)
    slot = i % 2
    qi = jnp.minimum(i, n_tiles - 1) % tiles_per_seq

    @pl.when(i == 0)
    def _():
        osb_buf[1] = jnp.zeros(osb_buf.shape[1:], osb_buf.dtype)

    row = lax.broadcasted_iota(jnp.int32, (t, t), 0)
    col = lax.broadcasted_iota(jnp.int32, (t, t), 1)
    tri = (col > row).astype(BF16)
    causal = row < col
    lane = lax.broadcasted_iota(jnp.int32, (t, V7X_LANES), 1)
    zero = jnp.zeros((), BF16)
    q = q_ref[0]
    for h in heads:
        grp = q[:, (h // 2) * V7X_LANES:(h // 2 + 1) * V7X_LANES]
        qm_ref[h] = jnp.where((lane // SB_HEAD_DIM) == (h % 2), grp, zero)

    def logits(j):
        start = pl.multiple_of(j * t, t)
        k = k_ref[0, pl.ds(start, t), :]
        return [_dot_nt(k[:, (h // 2) * V7X_LANES:(h // 2 + 1) * V7X_LANES], qm_ref[h])
                for h in heads]

    def values_t(j, h):
        return vt_ref[0, h * SB_HEAD_DIM:(h + 1) * SB_HEAD_DIM, pl.ds(pl.multiple_of(j * t, t), t)]

    def softplus_phase(z, mask):
        sp = [_softplus2(z[h], mask) for h in heads]
        return sp, [sp[h].astype(BF16) for h in heads]

    def cumsum_phase(spb):
        return [_dot(tri, spb[h]) for h in heads]

    def weight_phase(z, sp, later, mask):
        w = [jnp.exp2(z[h] - sp[h] - later[h]) for h in heads]
        if mask is not None:
            w = [jnp.where(mask, w[h], 0.0) for h in heads]
        return [w[h].astype(BF16) for h in heads]

    def value_phase(j, w, later, spb):
        pv = [_dot(values_t(j, h), w[h]) for h in heads]
        return pv, [later[h][0:1, :] + spb[h][0:1, :].astype(F32) for h in heads]

    has_prev = qi > 0
    jp = jnp.maximum(qi - 1, 0)
    a_sb = _dot_tn(osb_buf[1 - slot], wsb_ref[...])
    a_hg = _dot(ohg_ref[...], whg_ref[...])
    z0 = logits(qi)
    gates = 1.0 / (1.0 + jnp.exp(-(gate_ref[...] + bg_ref[...])))
    merged = (gates[:, :d] * a_sb + gates[:, d:] * a_hg).astype(BF16)
    hres = x_ref[...] + _dot(merged, wout_ref[...])
    z1 = logits(jp)
    hn = _rms(hres, g2_ref[...]).astype(BF16)
    half = w1_ref.shape[1] // 2

    def mlp_up(lo):
        act = jnp.maximum(_dot(hn, w1_ref[:, lo:lo + half]), 0.0)
        return (act * act).astype(BF16)

    act_a = mlp_up(0)
    sp0, spb0 = softplus_phase(z0, causal)
    lat0 = cumsum_phase(spb0)
    act_b = mlp_up(half)
    sp1, spb1 = softplus_phase(z1, None)
    lat1 = cumsum_phase(spb1)
    hres = hres + _dot(act_a, w2_ref[0:half, :])
    w0 = weight_phase(z0, sp0, lat0, causal)
    w1 = weight_phase(z1, sp1, lat1, None)
    hres = hres + _dot(act_b, w2_ref[half:2 * half, :])
    pv0, tot0 = value_phase(qi, w0, lat0, spb0)
    pv1, tot1 = value_phase(jp, w1, lat1, spb1)
    o_ref[...] = _rms(hres, gf_ref[...]).astype(o_ref.dtype)
    for h in heads:
        scale = jnp.where(has_prev, jnp.exp2(-tot0[h]), 0.0)
        acc_ref[h] = pv0[h] + scale * pv1[h]
        carry_ref[h:h + 1, :] = tot0[h] + jnp.where(has_prev, tot1[h], 0.0)

    def more(state):
        n, live = state
        return jnp.logical_and(n < qi, live)

    def body(state):
        n, _ = state
        j = qi - 1 - n
        z = logits(j)
        sp, spb = softplus_phase(z, None)
        later = cumsum_phase(spb)
        pv, tot = value_phase(j, weight_phase(z, sp, later, None), later, spb)
        for h in heads:
            c = carry_ref[h:h + 1, :]
            acc_ref[h] += jnp.exp2(-c) * pv[h]
            carry_ref[h:h + 1, :] = c + tot[h]
        return n + 1, jnp.min(carry_ref[...]) < SB_DEAD_CARRY

    lax.while_loop(more, body, (jnp.int32(1), jnp.min(carry_ref[...]) < SB_DEAD_CARRY))
    osb_buf[slot] = acc_ref[...].reshape(SB_HEADS * SB_HEAD_DIM, t).astype(osb_buf.dtype)


def _attn_ffn(q3, k3, vt3, x2, ohg, gates, bg, wsb, whg, wout, g2, w1, w2, gf):
    b, s, sb_w = q3.shape
    n, d = x2.shape
    t = ATTN_TILE
    dff = w1.shape[1]
    tiles_per_seq = s // t
    n_tiles = n // t
    assert sb_w == SB_HEADS * SB_HEAD_DIM and tiles_per_seq * t == s and n_tiles == b * tiles_per_seq
    cur = lambda i: jnp.minimum(i, n_tiles - 1)
    prv = lambda i: (jnp.maximum(i - 1, 0), 0)
    seq = lambda i: cur(i) // tiles_per_seq
    whole_seq = lambda shape: pl.BlockSpec(shape, lambda i: (seq(i), 0, 0),
                                           pipeline_mode=pl.Buffered(1))
    full = lambda a: _resident(a.shape)
    resident_bytes = sum(a.size * a.dtype.itemsize for a in (bg, wsb, whg, wout, g2, w1, w2, gf)) \
        + 2 * s * sb_w * 2 + 2 * t * sb_w * 2 + SB_HEADS * t * (V7X_LANES * 2 + SB_HEAD_DIM * 4 + 4)
    moving = t * (sb_w * 2 + d * 4 + ohg.shape[1] * 2 + 2 * d * 4 + d * 4)
    return pl.pallas_call(
        functools.partial(_attn_ffn_kernel, tiles_per_seq=tiles_per_seq, n_tiles=n_tiles),
        grid=(n_tiles + 1,),
        in_specs=[pl.BlockSpec((1, t, sb_w), lambda i: (seq(i), cur(i) % tiles_per_seq, 0)),
                  whole_seq((1, s, sb_w)), whole_seq((1, sb_w, s)),
                  pl.BlockSpec((t, d), prv),
                  pl.BlockSpec((t, ohg.shape[1]), prv),
                  pl.BlockSpec((t, 2 * d), prv),
                  full(bg), full(wsb), full(whg), full(wout), full(g2), full(w1), full(w2),
                  full(gf)],
        out_specs=pl.BlockSpec((t, d), prv),
        out_shape=jax.ShapeDtypeStruct((n, d), x2.dtype),
        scratch_shapes=[pltpu.VMEM((2, sb_w, t), BF16),
                        pltpu.VMEM((SB_HEADS, t, V7X_LANES), BF16),
                        pltpu.VMEM((SB_HEADS, SB_HEAD_DIM, t), F32),
                        pltpu.VMEM((SB_HEADS, t), F32)],
        compiler_params=pltpu.CompilerParams(
            dimension_semantics=("arbitrary",),
            vmem_limit_bytes=_vmem_limit(moving, resident_bytes,
                                         t * (dff * 6 + d * 24) + 8 * SB_HEADS * t * t * 4)),
        name="attn_ffn",
    )(q3, k3, vt3, x2, ohg, gates, bg, wsb, whg, wout, g2, w1, w2, gf)


def kernel(x, norm1_g, w_in, b_gate, lb_logits, hg_norm_g, w_o_sb, w_o_hg, w_out, norm2_g,
           w_ff1, w_ff2, final_g):
    b, s, d = x.shape
    assert w_in.shape[0] == 1, "single-layer block"
    sb_w = SB_HEADS * SB_HEAD_DIM
    hg_w = HG_HEADS * HG_HEAD_DIM
    x2 = x.reshape(b * s, d)
    q, k, vt, gates, o_hg = _proj_hgrn2(x2, norm1_g, w_in[0].astype(BF16), lb_logits,
                                        hg_norm_g, b, sb_w, hg_w)
    out = _attn_ffn(q.reshape(b, s, sb_w), k.reshape(b, s, sb_w), vt, x2, o_hg, gates, b_gate,
                    w_o_sb[0].astype(BF16), w_o_hg[0].astype(BF16), w_out[0].astype(BF16),
                    norm2_g, w_ff1[0].astype(BF16), w_ff2[0].astype(BF16), final_g.reshape(1, d))
    return out.reshape(b, s, d)
```

```python
import functools

import jax
import jax.numpy as jnp
import numpy as np
from jax import lax
from jax.experimental import pallas as pl
from jax.experimental.pallas import tpu as pltpu

F32 = jnp.float32
BF16 = jnp.bfloat16

SB_HEADS = 8
SB_HEAD_DIM = 64
HG_HEADS = 4
HG_HEAD_DIM = 128
EPS = 1e-6
LOG2E = 1.4426950408889634
SB_DEAD_CARRY = 151.0

V7X_LANES = 128
V7X_MXU_DIM = 256
V7X_VMEM_BYTES = 64 * 1024 * 1024

ATTN_TILE = V7X_MXU_DIM
HG_BLOCK = V7X_MXU_DIM
PROJ_BLOCKS = 2


def _vmem_limit(pipelined_bytes, resident_bytes, temp_bytes):
    need = 2 * pipelined_bytes + resident_bytes + temp_bytes
    return int(min(need + need // 4, V7X_VMEM_BYTES - 8 * 1024 * 1024))


def _resident(shape):
    return pl.BlockSpec(shape, lambda *_: (0,) * len(shape), pipeline_mode=pl.Buffered(1))


def _rms(x, g):
    ms = jnp.mean(x * x, axis=-1, keepdims=True)
    return x * lax.rsqrt(ms + EPS) * g


def _dot(a, b):
    return jnp.dot(a, b, preferred_element_type=F32)


def _dot_nt(a, b):
    return lax.dot_general(a, b, (((1,), (1,)), ((), ())), preferred_element_type=F32)


def _dot_tn(a, b):
    return lax.dot_general(a, b, (((0,), (0,)), ((), ())), preferred_element_type=F32)


def _softplus2(z, mask):
    sp = jnp.maximum(z, 0.0) + jnp.log(1.0 + jnp.exp2(-jnp.abs(z))) * LOG2E
    return sp if mask is None else jnp.where(mask, sp, 0.0)


def _split3(x):
    a = x.astype(BF16)
    r = x - a.astype(F32)
    b = r.astype(BF16)
    c = (r - b.astype(F32)).astype(BF16)
    return a, b, c


def _rows_from_group(b, group, r):
    n, c = b.shape
    if group == n:
        return jnp.broadcast_to(b[r:r + 1, :], (n, c))
    b3 = b.reshape(n // group, group, c)
    return jnp.broadcast_to(b3[:, r:r + 1, :], b3.shape).reshape(n, c)


def _midpoint_rows(b, group, pos):
    n = b.shape[0]
    half = group // 2
    if group >= 16:
        return _rows_from_group(b, group, half - 1)
    if group == 8:
        return _rows_from_group(b, 8, 3)
    up1 = pltpu.roll(b, n - 1, 0)
    dn1 = pltpu.roll(b, 1, 0)
    if group == 2:
        return jnp.where((pos & 1) == 0, b, dn1)
    assert group == 4
    dn2 = pltpu.roll(b, 2, 0)
    r4 = pos & 3
    return jnp.where(r4 == 0, up1, jnp.where(r4 == 1, b, jnp.where(r4 == 2, dn1, dn2)))


def _hgrn2_gates(fr, qr, lb):
    e = jnp.exp(-jnp.abs(fr))
    r = 1.0 / (1.0 + e)
    er = e * r
    sig = jnp.where(fr >= 0, r, er)
    nsig = jnp.where(fr >= 0, er, r)
    logf2 = jnp.log(lb + (1.0 - lb) * sig) * LOG2E
    k = (1.0 - lb) * nsig
    q = qr / (1.0 + jnp.exp(-qr))
    return logf2, k, q


def _hgrn2_head_products(b, q, k, v, lvl, pos, st_ref, h, keep):
    t, dk = b.shape
    n_levels = t.bit_length() - 1
    hb = t // 2
    zeros = jnp.zeros((hb, dk), BF16)
    lvl_d = jnp.concatenate([lvl[0:hb, 0:hb], lvl[hb:t, hb:t]], axis=1)
    diag = jnp.zeros((hb, t), F32)
    for level in range(1, n_levels):
        group = 1 << level
        d = b - _midpoint_rows(b, group, pos)
        later = (pos & (group - 1)) >= (group // 2)
        fac = jnp.exp2(-jnp.abs(d))
        ql = jnp.where(later, q * fac, 0.0).astype(BF16)
        kl = jnp.where(later, 0.0, k * fac).astype(BF16)
        lhs = jnp.concatenate([ql[0:hb], ql[hb:t]], axis=1)
        rhs = jnp.concatenate([jnp.concatenate([kl[0:hb], zeros], axis=1),
                               jnp.concatenate([zeros, kl[hb:t]], axis=1)], axis=0)
        diag = jnp.where(lvl_d == level, _dot_nt(lhs, rhs), diag)
    b_mid = b[hb - 1:hb, :]
    q_top = (q[hb:t] * jnp.exp2(b[hb:t] - b_mid)).astype(BF16)
    k_top = (k[0:hb] * jnp.exp2(b_mid - b[0:hb])).astype(BF16)
    top = _dot_nt(q_top, k_top)
    scores = jnp.concatenate(
        [jnp.concatenate([diag[:, 0:hb], jnp.zeros((hb, hb), F32)], axis=1),
         jnp.concatenate([top, diag[:, hb:t]], axis=1)], axis=0)

    st = st_ref[h] * keep
    b_last = b[t - 1:t, :]
    o = _dot_nt((q * jnp.exp2(b)).astype(BF16), st.astype(BF16))
    o = o + jnp.sum(q * k, axis=1, keepdims=True) * v.astype(F32)
    k_dec = (k * jnp.exp2(b_last - b)).astype(BF16)
    st_ref[h] = jnp.exp2(b_last) * st + _dot_tn(v, k_dec)
    return scores.astype(BF16), o


def _hgrn2_head_output(scores, o, v, gr, ng):
    return _rms(o + _dot(scores, v), ng) * (gr / (1.0 + jnp.exp(-gr)))


def _proj_hgrn2_kernel(*refs, tiles_per_seq, n_cast):
    x0_ref, xnext_ref, g_ref, w_ref, lbl_ref, ng_ref, lvl_ref = refs[:7]
    cast_in = refs[7:7 + n_cast]
    q_ref, k_ref, vt_ref, gate_ref, ohg_ref = refs[7 + n_cast:12 + n_cast]
    cast_out = refs[12 + n_cast:12 + 2 * n_cast]
    xn_buf, f_buf, iqg_buf, st_ref = refs[12 + 2 * n_cast:]
    for src, dst in zip(cast_in, cast_out):
        dst[...] = src[...].astype(dst.dtype)
    d = xnext_ref.shape[1]
    t = HG_BLOCK
    n_blocks = xnext_ref.shape[0] // t
    dk = HG_HEAD_DIM
    hg_w = HG_HEADS * dk
    sb_w = q_ref.shape[1]
    i = pl.program_id(0)
    slot = i % 2
    prev = 1 - slot

    @pl.when(i == 0)
    def _():
        f_buf[1] = jnp.zeros(f_buf.shape[1:], f_buf.dtype)
        iqg_buf[1] = jnp.zeros(iqg_buf.shape[1:], iqg_buf.dtype)
        st_ref[...] = jnp.zeros_like(st_ref)
        xn_buf[0] = _rms(x0_ref[...], g_ref[...]).astype(BF16)

    xn = xn_buf[slot]

    row = lax.broadcasted_iota(jnp.int32, (t, t), 0)
    col = lax.broadcasted_iota(jnp.int32, (t, t), 1)
    tril = (row >= col).astype(BF16)
    pos = lax.broadcasted_iota(jnp.int32, (t, dk), 0)
    lvl = lvl_ref[...]
    first_keep = jnp.where(i % tiles_per_seq == 1, 0.0, 1.0)

    lbl = lbl_ref[...]
    ex = jnp.exp(lbl - jnp.max(lbl, axis=0, keepdims=True))
    lb_all = ex[0:1, :] / jnp.sum(ex, axis=0, keepdims=True)

    def project(lo, hi):
        return _dot(xn, w_ref[:, lo:hi])

    def prepare_block(blk):
        rows = slice(blk * t, (blk + 1) * t)
        logf2, k_all, q_all = _hgrn2_gates(f_buf[prev, rows, :],
                                           iqg_buf[prev, rows, hg_w:2 * hg_w].astype(F32), lb_all)
        g1, g2, g3 = _split3(logf2)
        return _dot(tril, g1) + _dot(tril, g2) + _dot(tril, g3), q_all, k_all

    def head_products(blk, h, prepared):
        b_all, q_all, k_all = prepared
        rows = slice(blk * t, (blk + 1) * t)
        sl = slice(h * dk, (h + 1) * dk)
        return _hgrn2_head_products(b_all[:, sl], q_all[:, sl], k_all[:, sl],
                                    iqg_buf[prev, rows, sl], lvl, pos, st_ref, h,
                                    first_keep if blk == 0 else 1.0)

    def head_output(blk, h, products):
        rows = slice(blk * t, (blk + 1) * t)
        sl = slice(h * dk, (h + 1) * dk)
        gr = iqg_buf[prev, rows, 2 * hg_w + h * dk:2 * hg_w + (h + 1) * dk].astype(F32)
        y = _hgrn2_head_output(*products, iqg_buf[prev, rows, sl], gr, ng_ref[:, sl])
        ohg_ref[rows, sl] = y.astype(ohg_ref.dtype)

    c0 = 3 * sb_w
    c1 = c0 + hg_w
    c2 = c1 + 3 * hg_w
    slab = d // 2

    def q_slab():
        q_ref[...] = (project(0, sb_w) * (SB_HEAD_DIM ** -0.5 * LOG2E)).astype(q_ref.dtype)

    def k_slab():
        k_ref[...] = project(sb_w, 2 * sb_w).astype(k_ref.dtype)

    def v_slab():
        v = project(2 * sb_w, 3 * sb_w)
        for blk in range(n_blocks):
            vt_ref[0, :, blk * t:(blk + 1) * t] = v[blk * t:(blk + 1) * t].T.astype(vt_ref.dtype)

    def f_slab():
        f_buf[slot] = project(c0, c1)

    def iqg_slab(lo):
        def run():
            iqg_buf[slot, :, lo:lo + slab] = project(c1 + lo, c1 + lo + slab).astype(iqg_buf.dtype)
        return run

    def gate_slab(lo):
        def run():
            gate_ref[:, lo:lo + slab] = project(c2 + lo, c2 + lo + slab)
        return run

    slabs = [v_slab, f_slab] + [iqg_slab(lo) for lo in range(0, 3 * hg_w, slab)] \
        + [gate_slab(lo) for lo in range(0, 2 * d, slab)]
    q_slab()
    k_slab()
    prepared = prepare_block(0)
    slabs.pop(0)()
    for blk in range(n_blocks):
        for h in range(HG_HEADS):
            if blk > 0 and h == 0:
                prepared = prepare_block(blk)
            products = head_products(blk, h, prepared)
            if slabs:
                slabs.pop(0)()
            head_output(blk, h, products)
    for run in slabs:
        run()
    xn_buf[prev] = _rms(xnext_ref[...], g_ref[...]).astype(BF16)


def _pair_levels(t):
    idx = np.arange(t)
    x = idx[:, None] ^ idx[None, :]
    lev = np.where(x > 0, np.floor(np.log2(np.maximum(x, 1))).astype(np.int64) + 1, 0)
    return np.where(idx[:, None] > idx[None, :], lev, 0).astype(np.int32)


def _proj_hgrn2(x2, g1, w_in, lb_logits, ng, later_weights, batch, sb_w, hg_w):
    n, d = x2.shape
    t = PROJ_BLOCKS * HG_BLOCK
    cols = w_in.shape[1]
    n_tiles = n // t
    tiles_per_seq = n_tiles // batch
    assert hg_w == HG_HEADS * HG_HEAD_DIM and n_tiles * t == n and tiles_per_seq * batch == n_tiles
    lvl = jnp.asarray(_pair_levels(HG_BLOCK))
    cur = lambda i: jnp.minimum(i, n_tiles - 1)
    rows = lambda w: pl.BlockSpec((t, w), lambda i: (cur(i), 0))
    vt_spec = pl.BlockSpec((1, sb_w, t),
                           lambda i: (cur(i) // tiles_per_seq, 0, cur(i) % tiles_per_seq))
    sds = lambda w, dt: jax.ShapeDtypeStruct((n, w), dt)
    cast_specs = [pl.BlockSpec((w.shape[0] // n_tiles, w.shape[1]), lambda i: (cur(i), 0))
                  for w in later_weights]
    assert all(w.shape[0] % (n_tiles * 16) == 0 for w in later_weights)
    cast_bytes = sum(w.size // n_tiles * 6 for w in later_weights)
    moving = t * (d * 4 + 2 * sb_w * 2 + sb_w * 2 + 2 * d * 4 + hg_w * 2) + cast_bytes
    resident = d * 4 + d * cols * 2 + 3 * hg_w * 4 + HG_BLOCK * HG_BLOCK * 4 + t * d * 4 \
        + 2 * t * d * 2 + 2 * t * hg_w * (4 + 3 * 2) + HG_HEADS * HG_HEAD_DIM * HG_HEAD_DIM * 4
    return pl.pallas_call(
        functools.partial(_proj_hgrn2_kernel, tiles_per_seq=tiles_per_seq,
                          n_cast=len(later_weights)),
        grid=(n_tiles + 1,),
        in_specs=[_resident((t, d)), pl.BlockSpec((t, d), lambda i: (cur(i + 1), 0)),
                  _resident((1, d)), _resident((d, cols)),
                  _resident(lb_logits.shape), _resident(ng.shape),
                  _resident((HG_BLOCK, HG_BLOCK))] + cast_specs,
        out_specs=[rows(sb_w), rows(sb_w), vt_spec, rows(2 * d),
                   pl.BlockSpec((t, hg_w), lambda i: (jnp.maximum(i - 1, 0), 0))] + cast_specs,
        out_shape=[sds(sb_w, BF16), sds(sb_w, BF16),
                   jax.ShapeDtypeStruct((batch, sb_w, n // batch), BF16),
                   sds(2 * d, F32), sds(hg_w, BF16)]
        + [jax.ShapeDtypeStruct(w.shape, BF16) for w in later_weights],
        scratch_shapes=[pltpu.VMEM((2, t, d), BF16),
                        pltpu.VMEM((2, t, hg_w), F32),
                        pltpu.VMEM((2, t, 3 * hg_w), BF16),
                        pltpu.VMEM((HG_HEADS, HG_HEAD_DIM, HG_HEAD_DIM), F32)],
        compiler_params=pltpu.CompilerParams(
            dimension_semantics=("arbitrary",),
            vmem_limit_bytes=_vmem_limit(moving, resident,
                                         t * (d * 6 + cols * 4) + 64 * HG_BLOCK * HG_BLOCK * 4)),
        name="proj_hgrn2",
    )(x2, x2, g1, w_in, lb_logits, ng, lvl, *later_weights)


def _attn_ffn_kernel(q_ref, k_ref, vt_ref, x_ref, ohg_ref, gate_ref, bg_ref, wsb_ref, whg_ref,
                     wout_ref, g2_ref, w1_ref, w2_ref, gf_ref, o_ref,
                     osb_buf, qm_ref, acc_ref, carry_ref, *, tiles_per_seq, n_tiles):
    t = q_ref.shape[1]
    d = x_ref.shape[1]
    heads = range(SB_HEADS)
    i = pl.program_id(0)
    slot = i % 2
    qi = jnp.minimum(i, n_tiles - 1) % tiles_per_seq

    @pl.when(i == 0)
    def _():
        osb_buf[1] = jnp.zeros(osb_buf.shape[1:], osb_buf.dtype)

    row = lax.broadcasted_iota(jnp.int32, (t, t), 0)
    col = lax.broadcasted_iota(jnp.int32, (t, t), 1)
    tri = (col > row).astype(BF16)
    causal = row < col
    lane = lax.broadcasted_iota(jnp.int32, (t, V7X_LANES), 1)
    zero = jnp.zeros((), BF16)
    q = q_ref[0]
    for h in heads:
        grp = q[:, (h // 2) * V7X_LANES:(h // 2 + 1) * V7X_LANES]
        qm_ref[h] = jnp.where((lane // SB_HEAD_DIM) == (h % 2), grp, zero)

    def logits(j):
        start = pl.multiple_of(j * t, t)
        k = k_ref[0, pl.ds(start, t), :]
        return [_dot_nt(k[:, (h // 2) * V7X_LANES:(h // 2 + 1) * V7X_LANES], qm_ref[h])
                for h in heads]

    def values_t(j, h):
        return vt_ref[0, h * SB_HEAD_DIM:(h + 1) * SB_HEAD_DIM, pl.ds(pl.multiple_of(j * t, t), t)]

    def softplus_phase(z, mask):
        sp = [_softplus2(z[h], mask) for h in heads]
        return sp, [sp[h].astype(BF16) for h in heads]

    def cumsum_phase(spb):
        return [_dot(tri, spb[h]) for h in heads]

    def weight_phase(z, sp, later, mask):
        w = [jnp.exp2(z[h] - sp[h] - later[h]) for h in heads]
        if mask is not None:
            w = [jnp.where(mask, w[h], 0.0) for h in heads]
        return [w[h].astype(BF16) for h in heads]

    def value_phase(j, w, later, spb):
        pv = [_dot(values_t(j, h), w[h]) for h in heads]
        return pv, [later[h][0:1, :] + spb[h][0:1, :].astype(F32) for h in heads]

    has_prev = qi > 0
    jp = jnp.maximum(qi - 1, 0)
    a_sb = _dot_tn(osb_buf[1 - slot], wsb_ref[...])
    a_hg = _dot(ohg_ref[...], whg_ref[...])
    z0 = logits(qi)
    gates = 1.0 / (1.0 + jnp.exp(-(gate_ref[...] + bg_ref[...])))
    merged = (gates[:, :d] * a_sb + gates[:, d:] * a_hg).astype(BF16)
    hres = x_ref[...] + _dot(merged, wout_ref[...])
    z1 = logits(jp)
    hn = _rms(hres, g2_ref[...]).astype(BF16)
    half = w1_ref.shape[1] // 2

    def mlp_up(lo):
        act = jnp.maximum(_dot(hn, w1_ref[:, lo:lo + half]), 0.0)
        return (act * act).astype(BF16)

    act_a = mlp_up(0)
    sp0, spb0 = softplus_phase(z0, causal)
    lat0 = cumsum_phase(spb0)
    act_b = mlp_up(half)
    sp1, spb1 = softplus_phase(z1, None)
    lat1 = cumsum_phase(spb1)
    hres = hres + _dot(act_a, w2_ref[0:half, :])
    w0 = weight_phase(z0, sp0, lat0, causal)
    w1 = weight_phase(z1, sp1, lat1, None)
    hres = hres + _dot(act_b, w2_ref[half:2 * half, :])
    pv0, tot0 = value_phase(qi, w0, lat0, spb0)
    pv1, tot1 = value_phase(jp, w1, lat1, spb1)
    o_ref[...] = _rms(hres, gf_ref[...]).astype(o_ref.dtype)
    for h in heads:
        scale = jnp.where(has_prev, jnp.exp2(-tot0[h]), 0.0)
        acc_ref[h] = pv0[h] + scale * pv1[h]
        carry_ref[h:h + 1, :] = tot0[h] + jnp.where(has_prev, tot1[h], 0.0)

    def more(state):
        n, live = state
        return jnp.logical_and(n < qi, live)

    def body(state):
        n, _ = state
        j = qi - 1 - n
        z = logits(j)
        sp, spb = softplus_phase(z, None)
        later = cumsum_phase(spb)
        pv, tot = value_phase(j, weight_phase(z, sp, later, None), later, spb)
        for h in heads:
            c = carry_ref[h:h + 1, :]
            acc_ref[h] += jnp.exp2(-c) * pv[h]
            carry_ref[h:h + 1, :] = c + tot[h]
        return n + 1, jnp.min(carry_ref[...]) < SB_DEAD_CARRY

    lax.while_loop(more, body, (jnp.int32(1), jnp.min(carry_ref[...]) < SB_DEAD_CARRY))
    osb_buf[slot] = acc_ref[...].reshape(SB_HEADS * SB_HEAD_DIM, t).astype(osb_buf.dtype)


def _attn_ffn(q3, k3, vt3, x2, ohg, gates, bg, wsb, whg, wout, g2, w1, w2, gf):
    b, s, sb_w = q3.shape
    n, d = x2.shape
    t = ATTN_TILE
    dff = w1.shape[1]
    tiles_per_seq = s // t
    n_tiles = n // t
    assert sb_w == SB_HEADS * SB_HEAD_DIM and tiles_per_seq * t == s and n_tiles == b * tiles_per_seq
    cur = lambda i: jnp.minimum(i, n_tiles - 1)
    prv = lambda i: (jnp.maximum(i - 1, 0), 0)
    seq = lambda i: cur(i) // tiles_per_seq
    whole_seq = lambda shape: pl.BlockSpec(shape, lambda i: (seq(i), 0, 0),
                                           pipeline_mode=pl.Buffered(1))
    full = lambda a: _resident(a.shape)
    resident_bytes = sum(a.size * a.dtype.itemsize for a in (bg, wsb, whg, wout, g2, w1, w2, gf)) \
        + 2 * s * sb_w * 2 + 2 * t * sb_w * 2 + SB_HEADS * t * (V7X_LANES * 2 + SB_HEAD_DIM * 4 + 4)
    moving = t * (sb_w * 2 + d * 4 + ohg.shape[1] * 2 + 2 * d * 4 + d * 4)
    return pl.pallas_call(
        functools.partial(_attn_ffn_kernel, tiles_per_seq=tiles_per_seq, n_tiles=n_tiles),
        grid=(n_tiles + 1,),
        in_specs=[pl.BlockSpec((1, t, sb_w), lambda i: (seq(i), cur(i) % tiles_per_seq, 0)),
                  whole_seq((1, s, sb_w)), whole_seq((1, sb_w, s)),
                  pl.BlockSpec((t, d), prv),
                  pl.BlockSpec((t, ohg.shape[1]), prv),
                  pl.BlockSpec((t, 2 * d), prv),
                  full(bg), full(wsb), full(whg), full(wout), full(g2), full(w1), full(w2),
                  full(gf)],
        out_specs=pl.BlockSpec((t, d), prv),
        out_shape=jax.ShapeDtypeStruct((n, d), x2.dtype),
        scratch_shapes=[pltpu.VMEM((2, sb_w, t), BF16),
                        pltpu.VMEM((SB_HEADS, t, V7X_LANES), BF16),
                        pltpu.VMEM((SB_HEADS, SB_HEAD_DIM, t), F32),
                        pltpu.VMEM((SB_HEADS, t), F32)],
        compiler_params=pltpu.CompilerParams(
            dimension_semantics=("arbitrary",),
            vmem_limit_bytes=_vmem_limit(moving, resident_bytes,
                                         t * (dff * 6 + d * 24) + 8 * SB_HEADS * t * t * 4)),
        name="attn_ffn",
    )(q3, k3, vt3, x2, ohg, gates, bg, wsb, whg, wout, g2, w1, w2, gf)


def kernel(x, norm1_g, w_in, b_gate, lb_logits, hg_norm_g, w_o_sb, w_o_hg, w_out, norm2_g,
           w_ff1, w_ff2, final_g):
    b, s, d = x.shape
    assert w_in.shape[0] == 1, "single-layer block"
    sb_w = SB_HEADS * SB_HEAD_DIM
    hg_w = HG_HEADS * HG_HEAD_DIM
    x2 = x.reshape(b * s, d)
    later = (w_o_sb[0], w_o_hg[0], w_out[0], w_ff1[0], w_ff2[0])
    q, k, vt, gates, o_hg, wsb, whg, wout, w1, w2 = _proj_hgrn2(
        x2, norm1_g, w_in[0].astype(BF16), lb_logits, hg_norm_g, later, b, sb_w, hg_w)
    out = _attn_ffn(q.reshape(b, s, sb_w), k.reshape(b, s, sb_w), vt, x2, o_hg, gates, b_gate,
                    wsb, whg, wout, norm2_g, w1, w2, final_g.reshape(1, d))
    return out.reshape(b, s, d)
```

```python
import functools

import jax
import jax.numpy as jnp
import numpy as np
from jax import lax
from jax.experimental import pallas as pl
from jax.experimental.pallas import tpu as pltpu

F32 = jnp.float32
BF16 = jnp.bfloat16

SB_HEADS = 8
SB_HEAD_DIM = 64
HG_HEADS = 4
HG_HEAD_DIM = 128
EPS = 1e-6
LOG2E = 1.4426950408889634
SB_DEAD_CARRY = 151.0

V7X_LANES = 128
V7X_MXU_DIM = 256
V7X_VMEM_BYTES = 64 * 1024 * 1024

ATTN_TILE = V7X_MXU_DIM
HG_BLOCK = V7X_MXU_DIM
PROJ_BLOCKS = 2


def _vmem_limit(pipelined_bytes, resident_bytes, temp_bytes):
    need = 2 * pipelined_bytes + resident_bytes + temp_bytes
    return int(min(need + need // 4, V7X_VMEM_BYTES - 8 * 1024 * 1024))


def _resident(shape):
    return pl.BlockSpec(shape, lambda *_: (0,) * len(shape), pipeline_mode=pl.Buffered(1))


def _rms(x, g):
    ms = jnp.mean(x * x, axis=-1, keepdims=True)
    return x * lax.rsqrt(ms + EPS) * g


def _dot(a, b):
    return jnp.dot(a, b, preferred_element_type=F32)


def _dot_nt(a, b):
    return lax.dot_general(a, b, (((1,), (1,)), ((), ())), preferred_element_type=F32)


def _dot_tn(a, b):
    return lax.dot_general(a, b, (((0,), (0,)), ((), ())), preferred_element_type=F32)


def _softplus2(z, mask):
    sp = jnp.maximum(z, 0.0) + jnp.log(1.0 + jnp.exp2(-jnp.abs(z))) * LOG2E
    return sp if mask is None else jnp.where(mask, sp, 0.0)


def _split3(x):
    a = x.astype(BF16)
    r = x - a.astype(F32)
    b = r.astype(BF16)
    c = (r - b.astype(F32)).astype(BF16)
    return a, b, c


def _rows_from_group(b, group, r):
    n, c = b.shape
    if group == n:
        return jnp.broadcast_to(b[r:r + 1, :], (n, c))
    b3 = b.reshape(n // group, group, c)
    return jnp.broadcast_to(b3[:, r:r + 1, :], b3.shape).reshape(n, c)


def _midpoint_rows(b, group, pos):
    n = b.shape[0]
    half = group // 2
    if group >= 16:
        return _rows_from_group(b, group, half - 1)
    if group == 8:
        return _rows_from_group(b, 8, 3)
    up1 = pltpu.roll(b, n - 1, 0)
    dn1 = pltpu.roll(b, 1, 0)
    if group == 2:
        return jnp.where((pos & 1) == 0, b, dn1)
    assert group == 4
    dn2 = pltpu.roll(b, 2, 0)
    r4 = pos & 3
    return jnp.where(r4 == 0, up1, jnp.where(r4 == 1, b, jnp.where(r4 == 2, dn1, dn2)))


def _hgrn2_gates(fr, qr, lb):
    e = jnp.exp(-jnp.abs(fr))
    r = 1.0 / (1.0 + e)
    er = e * r
    sig = jnp.where(fr >= 0, r, er)
    nsig = jnp.where(fr >= 0, er, r)
    logf2 = jnp.log(lb + (1.0 - lb) * sig) * LOG2E
    k = (1.0 - lb) * nsig
    q = qr / (1.0 + jnp.exp(-qr))
    return logf2, k, q


def _hgrn2_head_products(b, q, k, v, lvl, pos, st_ref, h, keep):
    t, dk = b.shape
    n_levels = t.bit_length() - 1
    hb = t // 2
    zeros = jnp.zeros((hb, dk), BF16)
    lvl_d = jnp.concatenate([lvl[0:hb, 0:hb], lvl[hb:t, hb:t]], axis=1)
    diag = jnp.zeros((hb, t), F32)
    for level in range(1, n_levels):
        group = 1 << level
        d = b - _midpoint_rows(b, group, pos)
        later = (pos & (group - 1)) >= (group // 2)
        fac = jnp.exp2(-jnp.abs(d))
        ql = jnp.where(later, q * fac, 0.0).astype(BF16)
        kl = jnp.where(later, 0.0, k * fac).astype(BF16)
        lhs = jnp.concatenate([ql[0:hb], ql[hb:t]], axis=1)
        rhs = jnp.concatenate([jnp.concatenate([kl[0:hb], zeros], axis=1),
                               jnp.concatenate([zeros, kl[hb:t]], axis=1)], axis=0)
        diag = jnp.where(lvl_d == level, _dot_nt(lhs, rhs), diag)
    b_mid = b[hb - 1:hb, :]
    q_top = (q[hb:t] * jnp.exp2(b[hb:t] - b_mid)).astype(BF16)
    k_top = (k[0:hb] * jnp.exp2(b_mid - b[0:hb])).astype(BF16)
    top = _dot_nt(q_top, k_top)
    scores = jnp.concatenate(
        [jnp.concatenate([diag[:, 0:hb], jnp.zeros((hb, hb), F32)], axis=1),
         jnp.concatenate([top, diag[:, hb:t]], axis=1)], axis=0)

    st = st_ref[h] * keep
    b_last = b[t - 1:t, :]
    o = _dot_nt((q * jnp.exp2(b)).astype(BF16), st.astype(BF16))
    o = o + jnp.sum(q * k, axis=1, keepdims=True) * v.astype(F32)
    k_dec = (k * jnp.exp2(b_last - b)).astype(BF16)
    st_ref[h] = jnp.exp2(b_last) * st + _dot_tn(v, k_dec)
    return scores.astype(BF16), o


def _hgrn2_head_output(scores, o, v, gr, ng):
    return _rms(o + _dot(scores, v), ng) * (gr / (1.0 + jnp.exp(-gr)))


def _proj_hgrn2_kernel(*refs, tiles_per_seq, n_cast):
    x0_ref, xnext_ref, g_ref, w_ref, lbl_ref, ng_ref, lvl_ref = refs[:7]
    cast_in = refs[7:7 + n_cast]
    q_ref, k_ref, vt_ref, gate_ref, ohg_ref = refs[7 + n_cast:12 + n_cast]
    cast_out = refs[12 + n_cast:12 + 2 * n_cast]
    xn_buf, f_buf, iqg_buf, st_ref = refs[12 + 2 * n_cast:]
    for src, dst in zip(cast_in, cast_out):
        dst[...] = src[...].astype(dst.dtype)
    d = xnext_ref.shape[1]
    t = HG_BLOCK
    n_blocks = xnext_ref.shape[0] // t
    dk = HG_HEAD_DIM
    hg_w = HG_HEADS * dk
    sb_w = q_ref.shape[1]
    i = pl.program_id(0)
    slot = i % 2
    prev = 1 - slot

    @pl.when(i == 0)
    def _():
        f_buf[1] = jnp.zeros(f_buf.shape[1:], f_buf.dtype)
        iqg_buf[1] = jnp.zeros(iqg_buf.shape[1:], iqg_buf.dtype)
        st_ref[...] = jnp.zeros_like(st_ref)
        xn_buf[0] = _rms(x0_ref[...], g_ref[...]).astype(BF16)

    xn = xn_buf[slot]

    row = lax.broadcasted_iota(jnp.int32, (t, t), 0)
    col = lax.broadcasted_iota(jnp.int32, (t, t), 1)
    tril = (row >= col).astype(BF16)
    pos = lax.broadcasted_iota(jnp.int32, (t, dk), 0)
    lvl = lvl_ref[...]
    first_keep = jnp.where(i % tiles_per_seq == 1, 0.0, 1.0)

    lbl = lbl_ref[...]
    ex = jnp.exp(lbl - jnp.max(lbl, axis=0, keepdims=True))
    lb_all = ex[0:1, :] / jnp.sum(ex, axis=0, keepdims=True)

    def project(lo, hi):
        return _dot(xn, w_ref[:, lo:hi])

    def prepare_block(blk):
        rows = slice(blk * t, (blk + 1) * t)
        logf2, k_all, q_all = _hgrn2_gates(f_buf[prev, rows, :],
                                           iqg_buf[prev, rows, hg_w:2 * hg_w].astype(F32), lb_all)
        g1, g2, g3 = _split3(logf2)
        return _dot(tril, g1) + _dot(tril, g2) + _dot(tril, g3), q_all, k_all

    def head_products(blk, h, prepared):
        b_all, q_all, k_all = prepared
        rows = slice(blk * t, (blk + 1) * t)
        sl = slice(h * dk, (h + 1) * dk)
        return _hgrn2_head_products(b_all[:, sl], q_all[:, sl], k_all[:, sl],
                                    iqg_buf[prev, rows, sl], lvl, pos, st_ref, h,
                                    first_keep if blk == 0 else 1.0)

    def head_output(blk, h, products):
        rows = slice(blk * t, (blk + 1) * t)
        sl = slice(h * dk, (h + 1) * dk)
        gr = iqg_buf[prev, rows, 2 * hg_w + h * dk:2 * hg_w + (h + 1) * dk].astype(F32)
        y = _hgrn2_head_output(*products, iqg_buf[prev, rows, sl], gr, ng_ref[:, sl])
        ohg_ref[rows, sl] = y.astype(ohg_ref.dtype)

    c0 = 3 * sb_w
    c1 = c0 + hg_w
    c2 = c1 + 3 * hg_w
    slab = d // 2

    def q_slab():
        q_ref[...] = (project(0, sb_w) * (SB_HEAD_DIM ** -0.5 * LOG2E)).astype(q_ref.dtype)

    def k_slab():
        k_ref[...] = project(sb_w, 2 * sb_w).astype(k_ref.dtype)

    def v_slab():
        v = project(2 * sb_w, 3 * sb_w)
        for blk in range(n_blocks):
            vt_ref[0, :, blk * t:(blk + 1) * t] = v[blk * t:(blk + 1) * t].T.astype(vt_ref.dtype)

    def f_slab():
        f_buf[slot] = project(c0, c1)

    def iqg_slab(lo):
        def run():
            iqg_buf[slot, :, lo:lo + slab] = project(c1 + lo, c1 + lo + slab).astype(iqg_buf.dtype)
        return run

    def gate_slab(lo):
        def run():
            gate_ref[:, lo:lo + slab] = project(c2 + lo, c2 + lo + slab).astype(gate_ref.dtype)
        return run

    slabs = [v_slab, f_slab] + [iqg_slab(lo) for lo in range(0, 3 * hg_w, slab)] \
        + [gate_slab(lo) for lo in range(0, 2 * d, slab)]
    q_slab()
    k_slab()
    prepared = prepare_block(0)
    slabs.pop(0)()
    for blk in range(n_blocks):
        for h in range(HG_HEADS):
            if blk > 0 and h == 0:
                prepared = prepare_block(blk)
            products = head_products(blk, h, prepared)
            if slabs:
                slabs.pop(0)()
            head_output(blk, h, products)
    for run in slabs:
        run()
    xn_buf[prev] = _rms(xnext_ref[...], g_ref[...]).astype(BF16)


def _pair_levels(t):
    idx = np.arange(t)
    x = idx[:, None] ^ idx[None, :]
    lev = np.where(x > 0, np.floor(np.log2(np.maximum(x, 1))).astype(np.int64) + 1, 0)
    return np.where(idx[:, None] > idx[None, :], lev, 0).astype(np.int32)


def _proj_hgrn2(x2, g1, w_in, lb_logits, ng, later_weights, batch, sb_w, hg_w):
    n, d = x2.shape
    t = PROJ_BLOCKS * HG_BLOCK
    cols = w_in.shape[1]
    n_tiles = n // t
    tiles_per_seq = n_tiles // batch
    assert hg_w == HG_HEADS * HG_HEAD_DIM and n_tiles * t == n and tiles_per_seq * batch == n_tiles
    lvl = jnp.asarray(_pair_levels(HG_BLOCK))
    cur = lambda i: jnp.minimum(i, n_tiles - 1)
    rows = lambda w: pl.BlockSpec((t, w), lambda i: (cur(i), 0))
    vt_spec = pl.BlockSpec((1, sb_w, t),
                           lambda i: (cur(i) // tiles_per_seq, 0, cur(i) % tiles_per_seq))
    sds = lambda w, dt: jax.ShapeDtypeStruct((n, w), dt)
    cast_specs = [pl.BlockSpec((w.shape[0] // n_tiles, w.shape[1]), lambda i: (cur(i), 0))
                  for w in later_weights]
    assert all(w.shape[0] % (n_tiles * 16) == 0 for w in later_weights)
    cast_bytes = sum(w.size // n_tiles * 6 for w in later_weights)
    moving = t * (d * 4 + 2 * sb_w * 2 + sb_w * 2 + 2 * d * 2 + hg_w * 2) + cast_bytes
    resident = d * 4 + d * cols * 2 + 3 * hg_w * 4 + HG_BLOCK * HG_BLOCK * 4 + t * d * 4 \
        + 2 * t * d * 2 + 2 * t * hg_w * (4 + 3 * 2) + HG_HEADS * HG_HEAD_DIM * HG_HEAD_DIM * 4
    return pl.pallas_call(
        functools.partial(_proj_hgrn2_kernel, tiles_per_seq=tiles_per_seq,
                          n_cast=len(later_weights)),
        grid=(n_tiles + 1,),
        in_specs=[_resident((t, d)), pl.BlockSpec((t, d), lambda i: (cur(i + 1), 0)),
                  _resident((1, d)), _resident((d, cols)),
                  _resident(lb_logits.shape), _resident(ng.shape),
                  _resident((HG_BLOCK, HG_BLOCK))] + cast_specs,
        out_specs=[rows(sb_w), rows(sb_w), vt_spec, rows(2 * d),
                   pl.BlockSpec((t, hg_w), lambda i: (jnp.maximum(i - 1, 0), 0))] + cast_specs,
        out_shape=[sds(sb_w, BF16), sds(sb_w, BF16),
                   jax.ShapeDtypeStruct((batch, sb_w, n // batch), BF16),
                   sds(2 * d, BF16), sds(hg_w, BF16)]
        + [jax.ShapeDtypeStruct(w.shape, BF16) for w in later_weights],
        scratch_shapes=[pltpu.VMEM((2, t, d), BF16),
                        pltpu.VMEM((2, t, hg_w), F32),
                        pltpu.VMEM((2, t, 3 * hg_w), BF16),
                        pltpu.VMEM((HG_HEADS, HG_HEAD_DIM, HG_HEAD_DIM), F32)],
        compiler_params=pltpu.CompilerParams(
            dimension_semantics=("arbitrary",),
            vmem_limit_bytes=_vmem_limit(moving, resident,
                                         t * (d * 6 + cols * 4) + 64 * HG_BLOCK * HG_BLOCK * 4)),
        name="proj_hgrn2",
    )(x2, x2, g1, w_in, lb_logits, ng, lvl, *later_weights)


def _attn_ffn_kernel(q_ref, k_ref, vt_ref, x_ref, ohg_ref, gate_ref, bg_ref, wsb_ref, whg_ref,
                     wout_ref, g2_ref, w1_ref, w2_ref, gf_ref, o_ref,
                     osb_buf, qm_ref, acc_ref, carry_ref, *, tiles_per_seq, n_tiles):
    t = q_ref.shape[1]
    d = x_ref.shape[1]
    heads = range(SB_HEADS)
    i = pl.program_id(0)
    slot = i % 2
    qi = jnp.minimum(i, n_tiles - 1) % tiles_per_seq

    @pl.when(i == 0)
    def _():
        osb_buf[1] = jnp.zeros(osb_buf.shape[1:], osb_buf.dtype)

    row = lax.broadcasted_iota(jnp.int32, (t, t), 0)
    col = lax.broadcasted_iota(jnp.int32, (t, t), 1)
    tri = (col > row).astype(BF16)
    causal = row < col
    lane = lax.broadcasted_iota(jnp.int32, (t, V7X_LANES), 1)
    zero = jnp.zeros((), BF16)
    q = q_ref[0]
    for h in heads:
        grp = q[:, (h // 2) * V7X_LANES:(h // 2 + 1) * V7X_LANES]
        qm_ref[h] = jnp.where((lane // SB_HEAD_DIM) == (h % 2), grp, zero)

    def logits(j):
        start = pl.multiple_of(j * t, t)
        k = k_ref[0, pl.ds(start, t), :]
        return [_dot_nt(k[:, (h // 2) * V7X_LANES:(h // 2 + 1) * V7X_LANES], qm_ref[h])
                for h in heads]

    def values_t(j, h):
        return vt_ref[0, h * SB_HEAD_DIM:(h + 1) * SB_HEAD_DIM, pl.ds(pl.multiple_of(j * t, t), t)]

    def softplus_phase(z, mask):
        sp = [_softplus2(z[h], mask) for h in heads]
        return sp, [sp[h].astype(BF16) for h in heads]

    def cumsum_phase(spb):
        return [_dot(tri, spb[h]) for h in heads]

    def weight_phase(z, sp, later, mask):
        w = [jnp.exp2(z[h] - sp[h] - later[h]) for h in heads]
        if mask is not None:
            w = [jnp.where(mask, w[h], 0.0) for h in heads]
        return [w[h].astype(BF16) for h in heads]

    def value_phase(j, w, later, spb):
        pv = [_dot(values_t(j, h), w[h]) for h in heads]
        return pv, [later[h][0:1, :] + spb[h][0:1, :].astype(F32) for h in heads]

    has_prev = qi > 0
    jp = jnp.maximum(qi - 1, 0)
    a_sb = _dot_tn(osb_buf[1 - slot], wsb_ref[...])
    a_hg = _dot(ohg_ref[...], whg_ref[...])
    z0 = logits(qi)
    gates = 1.0 / (1.0 + jnp.exp(-(gate_ref[...].astype(F32) + bg_ref[...])))
    merged = (gates[:, :d] * a_sb + gates[:, d:] * a_hg).astype(BF16)
    hres = x_ref[...] + _dot(merged, wout_ref[...])
    z1 = logits(jp)
    hn = _rms(hres, g2_ref[...]).astype(BF16)
    half = w1_ref.shape[1] // 2

    def mlp_up(lo):
        act = jnp.maximum(_dot(hn, w1_ref[:, lo:lo + half]), 0.0)
        return (act * act).astype(BF16)

    act_a = mlp_up(0)
    sp0, spb0 = softplus_phase(z0, causal)
    lat0 = cumsum_phase(spb0)
    act_b = mlp_up(half)
    sp1, spb1 = softplus_phase(z1, None)
    lat1 = cumsum_phase(spb1)
    hres = hres + _dot(act_a, w2_ref[0:half, :])
    w0 = weight_phase(z0, sp0, lat0, causal)
    w1 = weight_phase(z1, sp1, lat1, None)
    hres = hres + _dot(act_b, w2_ref[half:2 * half, :])
    pv0, tot0 = value_phase(qi, w0, lat0, spb0)
    pv1, tot1 = value_phase(jp, w1, lat1, spb1)
    o_ref[...] = _rms(hres, gf_ref[...]).astype(o_ref.dtype)
    for h in heads:
        scale = jnp.where(has_prev, jnp.exp2(-tot0[h]), 0.0)
        acc_ref[h] = pv0[h] + scale * pv1[h]
        carry_ref[h:h + 1, :] = tot0[h] + jnp.where(has_prev, tot1[h], 0.0)

    def more(state):
        n, live = state
        return jnp.logical_and(n < qi, live)

    def body(state):
        n, _ = state
        j = qi - 1 - n
        z = logits(j)
        sp, spb = softplus_phase(z, None)
        later = cumsum_phase(spb)
        pv, tot = value_phase(j, weight_phase(z, sp, later, None), later, spb)
        for h in heads:
            c = carry_ref[h:h + 1, :]
            acc_ref[h] += jnp.exp2(-c) * pv[h]
            carry_ref[h:h + 1, :] = c + tot[h]
        return n + 1, jnp.min(carry_ref[...]) < SB_DEAD_CARRY

    lax.while_loop(more, body, (jnp.int32(1), jnp.min(carry_ref[...]) < SB_DEAD_CARRY))
    osb_buf[slot] = acc_ref[...].reshape(SB_HEADS * SB_HEAD_DIM, t).astype(osb_buf.dtype)


def _attn_ffn(q3, k3, vt3, x2, ohg, gates, bg, wsb, whg, wout, g2, w1, w2, gf):
    b, s, sb_w = q3.shape
    n, d = x2.shape
    t = ATTN_TILE
    dff = w1.shape[1]
    tiles_per_seq = s // t
    n_tiles = n // t
    assert sb_w == SB_HEADS * SB_HEAD_DIM and tiles_per_seq * t == s and n_tiles == b * tiles_per_seq
    cur = lambda i: jnp.minimum(i, n_tiles - 1)
    prv = lambda i: (jnp.maximum(i - 1, 0), 0)
    seq = lambda i: cur(i) // tiles_per_seq
    whole_seq = lambda shape: pl.BlockSpec(shape, lambda i: (seq(i), 0, 0),
                                           pipeline_mode=pl.Buffered(1))
    full = lambda a: _resident(a.shape)
    resident_bytes = sum(a.size * a.dtype.itemsize for a in (bg, wsb, whg, wout, g2, w1, w2, gf)) \
        + 2 * s * sb_w * 2 + 2 * t * sb_w * 2 + SB_HEADS * t * (V7X_LANES * 2 + SB_HEAD_DIM * 4 + 4)
    moving = t * (sb_w * 2 + d * 4 + ohg.shape[1] * 2 + 2 * d * 2 + d * 4)
    return pl.pallas_call(
        functools.partial(_attn_ffn_kernel, tiles_per_seq=tiles_per_seq, n_tiles=n_tiles),
        grid=(n_tiles + 1,),
        in_specs=[pl.BlockSpec((1, t, sb_w), lambda i: (seq(i), cur(i) % tiles_per_seq, 0)),
                  whole_seq((1, s, sb_w)), whole_seq((1, sb_w, s)),
                  pl.BlockSpec((t, d), prv),
                  pl.BlockSpec((t, ohg.shape[1]), prv),
                  pl.BlockSpec((t, 2 * d), prv),
                  full(bg), full(wsb), full(whg), full(wout), full(g2), full(w1), full(w2),
                  full(gf)],
        out_specs=pl.BlockSpec((t, d), prv),
        out_shape=jax.ShapeDtypeStruct((n, d), x2.dtype),
        scratch_shapes=[pltpu.VMEM((2, sb_w, t), BF16),
                        pltpu.VMEM((SB_HEADS, t, V7X_LANES), BF16),
                        pltpu.VMEM((SB_HEADS, SB_HEAD_DIM, t), F32),
                        pltpu.VMEM((SB_HEADS, t), F32)],
        compiler_params=pltpu.CompilerParams(
            dimension_semantics=("arbitrary",),
            vmem_limit_bytes=_vmem_limit(moving, resident_bytes,
                                         t * (dff * 6 + d * 24) + 8 * SB_HEADS * t * t * 4)),
        name="attn_ffn",
    )(q3, k3, vt3, x2, ohg, gates, bg, wsb, whg, wout, g2, w1, w2, gf)


def kernel(x, norm1_g, w_in, b_gate, lb_logits, hg_norm_g, w_o_sb, w_o_hg, w_out, norm2_g,
           w_ff1, w_ff2, final_g):
    b, s, d = x.shape
    assert w_in.shape[0] == 1, "single-layer block"
    sb_w = SB_HEADS * SB_HEAD_DIM
    hg_w = HG_HEADS * HG_HEAD_DIM
    x2 = x.reshape(b * s, d)
    later = (w_o_sb[0], w_o_hg[0], w_out[0], w_ff1[0], w_ff2[0])
    q, k, vt, gates, o_hg, wsb, whg, wout, w1, w2 = _proj_hgrn2(
        x2, norm1_g, w_in[0].astype(BF16), lb_logits, hg_norm_g, later, b, sb_w, hg_w)
    out = _attn_ffn(q.reshape(b, s, sb_w), k.reshape(b, s, sb_w), vt, x2, o_hg, gates, b_gate,
                    wsb, whg, wout, norm2_g, w1, w2, final_g.reshape(1, d))
    return out.reshape(b, s, d)
```

```python
import functools

import jax
import jax.numpy as jnp
import numpy as np
from jax import lax
from jax.experimental import pallas as pl
from jax.experimental.pallas import tpu as pltpu

F32 = jnp.float32
BF16 = jnp.bfloat16

SB_HEADS = 8
SB_HEAD_DIM = 64
HG_HEADS = 4
HG_HEAD_DIM = 128
EPS = 1e-6
LOG2E = 1.4426950408889634
SB_DEAD_CARRY = 151.0

V7X_LANES = 128
V7X_MXU_DIM = 256
V7X_VMEM_BYTES = 64 * 1024 * 1024

ATTN_TILE = V7X_MXU_DIM
HG_BLOCK = V7X_MXU_DIM
PROJ_BLOCKS = 2


def _vmem_limit(pipelined_bytes, resident_bytes, temp_bytes):
    need = 2 * pipelined_bytes + resident_bytes + temp_bytes
    return int(min(need + need // 4, V7X_VMEM_BYTES - 8 * 1024 * 1024))


def _resident(shape):
    return pl.BlockSpec(shape, lambda *_: (0,) * len(shape), pipeline_mode=pl.Buffered(1))


def _rms(x, g):
    ms = jnp.mean(x * x, axis=-1, keepdims=True)
    return x * lax.rsqrt(ms + EPS) * g


def _dot(a, b):
    return jnp.dot(a, b, preferred_element_type=F32)


def _dot_nt(a, b):
    return lax.dot_general(a, b, (((1,), (1,)), ((), ())), preferred_element_type=F32)


def _dot_tn(a, b):
    return lax.dot_general(a, b, (((0,), (0,)), ((), ())), preferred_element_type=F32)


def _softplus2(z, mask):
    sp = jnp.maximum(z, 0.0) + jnp.log(1.0 + jnp.exp2(-jnp.abs(z))) * LOG2E
    return sp if mask is None else jnp.where(mask, sp, 0.0)


def _split3(x):
    a = x.astype(BF16)
    r = x - a.astype(F32)
    b = r.astype(BF16)
    c = (r - b.astype(F32)).astype(BF16)
    return a, b, c


def _rows_from_group(b, group, r):
    n, c = b.shape
    if group == n:
        return jnp.broadcast_to(b[r:r + 1, :], (n, c))
    b3 = b.reshape(n // group, group, c)
    return jnp.broadcast_to(b3[:, r:r + 1, :], b3.shape).reshape(n, c)


def _midpoint_rows(b, group, pos):
    n = b.shape[0]
    half = group // 2
    if group >= 16:
        return _rows_from_group(b, group, half - 1)
    if group == 8:
        return _rows_from_group(b, 8, 3)
    up1 = pltpu.roll(b, n - 1, 0)
    dn1 = pltpu.roll(b, 1, 0)
    if group == 2:
        return jnp.where((pos & 1) == 0, b, dn1)
    assert group == 4
    dn2 = pltpu.roll(b, 2, 0)
    r4 = pos & 3
    return jnp.where(r4 == 0, up1, jnp.where(r4 == 1, b, jnp.where(r4 == 2, dn1, dn2)))


def _hgrn2_gates(fr, qr, lb):
    e = jnp.exp(-jnp.abs(fr))
    r = 1.0 / (1.0 + e)
    er = e * r
    sig = jnp.where(fr >= 0, r, er)
    nsig = jnp.where(fr >= 0, er, r)
    logf2 = jnp.log(lb + (1.0 - lb) * sig) * LOG2E
    k = (1.0 - lb) * nsig
    q = qr / (1.0 + jnp.exp(-qr))
    return logf2, k, q


def _hgrn2_head_products(b, q, k, v, lvl, pos, st_ref, h, keep):
    t, dk = b.shape
    n_levels = t.bit_length() - 1
    hb = t // 2
    zeros = jnp.zeros((hb, dk), BF16)
    lvl_d = jnp.concatenate([lvl[0:hb, 0:hb], lvl[hb:t, hb:t]], axis=1)
    diag = jnp.zeros((hb, t), F32)
    for level in range(1, n_levels):
        group = 1 << level
        d = b - _midpoint_rows(b, group, pos)
        later = (pos & (group - 1)) >= (group // 2)
        fac = jnp.exp2(-jnp.abs(d))
        ql = jnp.where(later, q * fac, 0.0).astype(BF16)
        kl = jnp.where(later, 0.0, k * fac).astype(BF16)
        lhs = jnp.concatenate([ql[0:hb], ql[hb:t]], axis=1)
        rhs = jnp.concatenate([jnp.concatenate([kl[0:hb], zeros], axis=1),
                               jnp.concatenate([zeros, kl[hb:t]], axis=1)], axis=0)
        diag = jnp.where(lvl_d == level, _dot_nt(lhs, rhs), diag)
    b_mid = b[hb - 1:hb, :]
    q_top = (q[hb:t] * jnp.exp2(b[hb:t] - b_mid)).astype(BF16)
    k_top = (k[0:hb] * jnp.exp2(b_mid - b[0:hb])).astype(BF16)
    top = _dot_nt(q_top, k_top)
    scores = jnp.concatenate(
        [jnp.concatenate([diag[:, 0:hb], jnp.zeros((hb, hb), F32)], axis=1),
         jnp.concatenate([top, diag[:, hb:t]], axis=1)], axis=0)

    st = st_ref[h] * keep
    b_last = b[t - 1:t, :]
    o = _dot_nt((q * jnp.exp2(b)).astype(BF16), st.astype(BF16))
    o = o + jnp.sum(q * k, axis=1, keepdims=True) * v.astype(F32)
    k_dec = (k * jnp.exp2(b_last - b)).astype(BF16)
    st_ref[h] = jnp.exp2(b_last) * st + _dot_tn(v, k_dec)
    return scores.astype(BF16), o


def _hgrn2_head_output(scores, o, v, gr, ng):
    return _rms(o + _dot(scores, v), ng) * (gr / (1.0 + jnp.exp(-gr)))


def _proj_hgrn2_kernel(*refs, tiles_per_seq, n_cast):
    x0_ref, xnext_ref, g_ref, w_ref, lbl_ref, ng_ref, lvl_ref = refs[:7]
    cast_in = refs[7:7 + n_cast]
    q_ref, k_ref, vt_ref, gate_ref, ohg_ref = refs[7 + n_cast:12 + n_cast]
    cast_out = refs[12 + n_cast:12 + 2 * n_cast]
    xn_buf, f_buf, iqg_buf, st_ref = refs[12 + 2 * n_cast:]
    d = xnext_ref.shape[1]
    t = HG_BLOCK
    n_blocks = xnext_ref.shape[0] // t
    dk = HG_HEAD_DIM
    hg_w = HG_HEADS * dk
    sb_w = q_ref.shape[1]
    i = pl.program_id(0)
    n_steps = pl.num_programs(0)
    slot = i % 2
    prev = 1 - slot

    def step(do_proj, do_hgrn2):
        row = lax.broadcasted_iota(jnp.int32, (t, t), 0)
        col = lax.broadcasted_iota(jnp.int32, (t, t), 1)
        tril = (row >= col).astype(BF16)
        pos = lax.broadcasted_iota(jnp.int32, (t, dk), 0)
        lvl = lvl_ref[...]
        first_keep = jnp.where(i % tiles_per_seq == 1, 0.0, 1.0)

        lbl = lbl_ref[...]
        ex = jnp.exp(lbl - jnp.max(lbl, axis=0, keepdims=True))
        lb_all = ex[0:1, :] / jnp.sum(ex, axis=0, keepdims=True)

        def prepare_block(blk):
            rows = slice(blk * t, (blk + 1) * t)
            logf2, k_all, q_all = _hgrn2_gates(
                f_buf[prev, rows, :], iqg_buf[prev, rows, hg_w:2 * hg_w].astype(F32), lb_all)
            g1, g2, g3 = _split3(logf2)
            return _dot(tril, g1) + _dot(tril, g2) + _dot(tril, g3), q_all, k_all

        def head_products(blk, h, prepared):
            b_all, q_all, k_all = prepared
            rows = slice(blk * t, (blk + 1) * t)
            sl = slice(h * dk, (h + 1) * dk)
            return _hgrn2_head_products(b_all[:, sl], q_all[:, sl], k_all[:, sl],
                                        iqg_buf[prev, rows, sl], lvl, pos, st_ref, h,
                                        first_keep if blk == 0 else 1.0)

        def head_output(blk, h, products):
            rows = slice(blk * t, (blk + 1) * t)
            sl = slice(h * dk, (h + 1) * dk)
            gr = iqg_buf[prev, rows, 2 * hg_w + h * dk:2 * hg_w + (h + 1) * dk].astype(F32)
            y = _hgrn2_head_output(*products, iqg_buf[prev, rows, sl], gr, ng_ref[:, sl])
            ohg_ref[rows, sl] = y.astype(ohg_ref.dtype)

        c0 = 3 * sb_w
        c1 = c0 + hg_w
        c2 = c1 + 3 * hg_w
        slab = d // 2
        slabs = []
        if do_proj:
            xn = xn_buf[slot]

            def project(lo, hi):
                return _dot(xn, w_ref[:, lo:hi])

            def q_slab():
                q_ref[...] = (project(0, sb_w) * (SB_HEAD_DIM ** -0.5 * LOG2E)).astype(q_ref.dtype)

            def k_slab():
                k_ref[...] = project(sb_w, 2 * sb_w).astype(k_ref.dtype)

            def v_slab():
                v = project(2 * sb_w, 3 * sb_w)
                for blk in range(n_blocks):
                    vt_ref[0, :, blk * t:(blk + 1) * t] = \
                        v[blk * t:(blk + 1) * t].T.astype(vt_ref.dtype)

            def f_slab():
                f_buf[slot] = project(c0, c1)

            def iqg_slab(lo):
                def run():
                    iqg_buf[slot, :, lo:lo + slab] = \
                        project(c1 + lo, c1 + lo + slab).astype(iqg_buf.dtype)
                return run

            def gate_slab(lo):
                def run():
                    gate_ref[:, lo:lo + slab] = project(c2 + lo, c2 + lo + slab)
                return run

            for src, dst in zip(cast_in, cast_out):
                dst[...] = src[...].astype(dst.dtype)
            slabs = [v_slab, f_slab] + [iqg_slab(lo) for lo in range(0, 3 * hg_w, slab)] \
                + [gate_slab(lo) for lo in range(0, 2 * d, slab)]
            q_slab()
            k_slab()
        if do_hgrn2:
            prepared = prepare_block(0)
            if slabs:
                slabs.pop(0)()
            for blk in range(n_blocks):
                for h in range(HG_HEADS):
                    if blk > 0 and h == 0:
                        prepared = prepare_block(blk)
                    products = head_products(blk, h, prepared)
                    if slabs:
                        slabs.pop(0)()
                    head_output(blk, h, products)
        for run in slabs:
            run()
        if do_proj:
            xn_buf[prev] = _rms(xnext_ref[...], g_ref[...]).astype(BF16)

    @pl.when(i == 0)
    def _():
        st_ref[...] = jnp.zeros_like(st_ref)
        xn_buf[0] = _rms(x0_ref[...], g_ref[...]).astype(BF16)
        step(True, False)

    @pl.when(jnp.logical_and(i > 0, i < n_steps - 1))
    def _():
        step(True, True)

    @pl.when(i == n_steps - 1)
    def _():
        step(False, True)


def _pair_levels(t):
    idx = np.arange(t)
    x = idx[:, None] ^ idx[None, :]
    lev = np.where(x > 0, np.floor(np.log2(np.maximum(x, 1))).astype(np.int64) + 1, 0)
    return np.where(idx[:, None] > idx[None, :], lev, 0).astype(np.int32)


def _proj_hgrn2(x2, g1, w_in, lb_logits, ng, later_weights, batch, sb_w, hg_w):
    n, d = x2.shape
    t = PROJ_BLOCKS * HG_BLOCK
    cols = w_in.shape[1]
    n_tiles = n // t
    tiles_per_seq = n_tiles // batch
    assert hg_w == HG_HEADS * HG_HEAD_DIM and n_tiles * t == n and tiles_per_seq * batch == n_tiles
    lvl = jnp.asarray(_pair_levels(HG_BLOCK))
    cur = lambda i: jnp.minimum(i, n_tiles - 1)
    rows = lambda w: pl.BlockSpec((t, w), lambda i: (cur(i), 0))
    vt_spec = pl.BlockSpec((1, sb_w, t),
                           lambda i: (cur(i) // tiles_per_seq, 0, cur(i) % tiles_per_seq))
    sds = lambda w, dt: jax.ShapeDtypeStruct((n, w), dt)
    cast_specs = [pl.BlockSpec((w.shape[0] // n_tiles, w.shape[1]), lambda i: (cur(i), 0))
                  for w in later_weights]
    assert all(w.shape[0] % (n_tiles * 16) == 0 for w in later_weights)
    cast_bytes = sum(w.size // n_tiles * 6 for w in later_weights)
    moving = t * (d * 4 + 2 * sb_w * 2 + sb_w * 2 + 2 * d * 4 + hg_w * 2) + cast_bytes
    resident = d * 4 + d * cols * 2 + 3 * hg_w * 4 + HG_BLOCK * HG_BLOCK * 4 + t * d * 4 \
        + 2 * t * d * 2 + 2 * t * hg_w * (4 + 3 * 2) + HG_HEADS * HG_HEAD_DIM * HG_HEAD_DIM * 4
    return pl.pallas_call(
        functools.partial(_proj_hgrn2_kernel, tiles_per_seq=tiles_per_seq,
                          n_cast=len(later_weights)),
        grid=(n_tiles + 1,),
        in_specs=[_resident((t, d)), pl.BlockSpec((t, d), lambda i: (cur(i + 1), 0)),
                  _resident((1, d)), _resident((d, cols)),
                  _resident(lb_logits.shape), _resident(ng.shape),
                  _resident((HG_BLOCK, HG_BLOCK))] + cast_specs,
        out_specs=[rows(sb_w), rows(sb_w), vt_spec, rows(2 * d),
                   pl.BlockSpec((t, hg_w), lambda i: (jnp.maximum(i - 1, 0), 0))] + cast_specs,
        out_shape=[sds(sb_w, BF16), sds(sb_w, BF16),
                   jax.ShapeDtypeStruct((batch, sb_w, n // batch), BF16),
                   sds(2 * d, F32), sds(hg_w, BF16)]
        + [jax.ShapeDtypeStruct(w.shape, BF16) for w in later_weights],
        scratch_shapes=[pltpu.VMEM((2, t, d), BF16),
                        pltpu.VMEM((2, t, hg_w), F32),
                        pltpu.VMEM((2, t, 3 * hg_w), BF16),
                        pltpu.VMEM((HG_HEADS, HG_HEAD_DIM, HG_HEAD_DIM), F32)],
        compiler_params=pltpu.CompilerParams(
            dimension_semantics=("arbitrary",),
            vmem_limit_bytes=_vmem_limit(moving, resident,
                                         t * (d * 6 + cols * 4) + 64 * HG_BLOCK * HG_BLOCK * 4)),
        name="proj_hgrn2",
    )(x2, x2, g1, w_in, lb_logits, ng, lvl, *later_weights)


def _attn_ffn_kernel(q_ref, k_ref, vt_ref, x_ref, ohg_ref, gate_ref, bg_ref, wsb_ref, whg_ref,
                     wout_ref, g2_ref, w1_ref, w2_ref, gf_ref, o_ref,
                     osb_buf, qm_ref, acc_ref, carry_ref, *, tiles_per_seq, n_tiles):
    t = q_ref.shape[1]
    d = x_ref.shape[1]
    heads = range(SB_HEADS)
    i = pl.program_id(0)
    slot = i % 2
    qi = jnp.minimum(i, n_tiles - 1) % tiles_per_seq

    @pl.when(i == 0)
    def _():
        osb_buf[1] = jnp.zeros(osb_buf.shape[1:], osb_buf.dtype)

    row = lax.broadcasted_iota(jnp.int32, (t, t), 0)
    col = lax.broadcasted_iota(jnp.int32, (t, t), 1)
    tri = (col > row).astype(BF16)
    causal = row < col
    lane = lax.broadcasted_iota(jnp.int32, (t, V7X_LANES), 1)
    zero = jnp.zeros((), BF16)
    q = q_ref[0]
    for h in heads:
        grp = q[:, (h // 2) * V7X_LANES:(h // 2 + 1) * V7X_LANES]
        qm_ref[h] = jnp.where((lane // SB_HEAD_DIM) == (h % 2), grp, zero)

    def logits(j):
        start = pl.multiple_of(j * t, t)
        k = k_ref[0, pl.ds(start, t), :]
        return [_dot_nt(k[:, (h // 2) * V7X_LANES:(h // 2 + 1) * V7X_LANES], qm_ref[h])
                for h in heads]

    def values_t(j, h):
        return vt_ref[0, h * SB_HEAD_DIM:(h + 1) * SB_HEAD_DIM, pl.ds(pl.multiple_of(j * t, t), t)]

    def softplus_phase(z, mask):
        sp = [_softplus2(z[h], mask) for h in heads]
        return sp, [sp[h].astype(BF16) for h in heads]

    def cumsum_phase(spb):
        return [_dot(tri, spb[h]) for h in heads]

    def weight_phase(z, sp, later, mask):
        w = [jnp.exp2(z[h] - sp[h] - later[h]) for h in heads]
        if mask is not None:
            w = [jnp.where(mask, w[h], 0.0) for h in heads]
        return [w[h].astype(BF16) for h in heads]

    def value_phase(j, w, later, spb):
        pv = [_dot(values_t(j, h), w[h]) for h in heads]
        return pv, [later[h][0:1, :] + spb[h][0:1, :].astype(F32) for h in heads]

    has_prev = qi > 0
    jp = jnp.maximum(qi - 1, 0)
    a_sb = _dot_tn(osb_buf[1 - slot], wsb_ref[...])
    a_hg = _dot(ohg_ref[...], whg_ref[...])
    z0 = logits(qi)
    gates = 1.0 / (1.0 + jnp.exp(-(gate_ref[...] + bg_ref[...])))
    merged = (gates[:, :d] * a_sb + gates[:, d:] * a_hg).astype(BF16)
    hres = x_ref[...] + _dot(merged, wout_ref[...])
    z1 = logits(jp)
    hn = _rms(hres, g2_ref[...]).astype(BF16)
    half = w1_ref.shape[1] // 2

    def mlp_up(lo):
        act = jnp.maximum(_dot(hn, w1_ref[:, lo:lo + half]), 0.0)
        return (act * act).astype(BF16)

    act_a = mlp_up(0)
    sp0, spb0 = softplus_phase(z0, causal)
    lat0 = cumsum_phase(spb0)
    act_b = mlp_up(half)
    sp1, spb1 = softplus_phase(z1, None)
    lat1 = cumsum_phase(spb1)
    hres = hres + _dot(act_a, w2_ref[0:half, :])
    w0 = weight_phase(z0, sp0, lat0, causal)
    w1 = weight_phase(z1, sp1, lat1, None)
    hres = hres + _dot(act_b, w2_ref[half:2 * half, :])
    pv0, tot0 = value_phase(qi, w0, lat0, spb0)
    pv1, tot1 = value_phase(jp, w1, lat1, spb1)
    o_ref[...] = _rms(hres, gf_ref[...]).astype(o_ref.dtype)
    for h in heads:
        scale = jnp.where(has_prev, jnp.exp2(-tot0[h]), 0.0)
        acc_ref[h] = pv0[h] + scale * pv1[h]
        carry_ref[h:h + 1, :] = tot0[h] + jnp.where(has_prev, tot1[h], 0.0)

    def more(state):
        n, live = state
        return jnp.logical_and(n < qi, live)

    def body(state):
        n, _ = state
        j = qi - 1 - n
        z = logits(j)
        sp, spb = softplus_phase(z, None)
        later = cumsum_phase(spb)
        pv, tot = value_phase(j, weight_phase(z, sp, later, None), later, spb)
        for h in heads:
            c = carry_ref[h:h + 1, :]
            acc_ref[h] += jnp.exp2(-c) * pv[h]
            carry_ref[h:h + 1, :] = c + tot[h]
        return n + 1, jnp.min(carry_ref[...]) < SB_DEAD_CARRY

    lax.while_loop(more, body, (jnp.int32(1), jnp.min(carry_ref[...]) < SB_DEAD_CARRY))
    osb_buf[slot] = acc_ref[...].reshape(SB_HEADS * SB_HEAD_DIM, t).astype(osb_buf.dtype)


def _attn_ffn(q3, k3, vt3, x2, ohg, gates, bg, wsb, whg, wout, g2, w1, w2, gf):
    b, s, sb_w = q3.shape
    n, d = x2.shape
    t = ATTN_TILE
    dff = w1.shape[1]
    tiles_per_seq = s // t
    n_tiles = n // t
    assert sb_w == SB_HEADS * SB_HEAD_DIM and tiles_per_seq * t == s and n_tiles == b * tiles_per_seq
    cur = lambda i: jnp.minimum(i, n_tiles - 1)
    prv = lambda i: (jnp.maximum(i - 1, 0), 0)
    seq = lambda i: cur(i) // tiles_per_seq
    whole_seq = lambda shape: pl.BlockSpec(shape, lambda i: (seq(i), 0, 0),
                                           pipeline_mode=pl.Buffered(1))
    full = lambda a: _resident(a.shape)
    resident_bytes = sum(a.size * a.dtype.itemsize for a in (bg, wsb, whg, wout, g2, w1, w2, gf)) \
        + 2 * s * sb_w * 2 + 2 * t * sb_w * 2 + SB_HEADS * t * (V7X_LANES * 2 + SB_HEAD_DIM * 4 + 4)
    moving = t * (sb_w * 2 + d * 4 + ohg.shape[1] * 2 + 2 * d * 4 + d * 4)
    return pl.pallas_call(
        functools.partial(_attn_ffn_kernel, tiles_per_seq=tiles_per_seq, n_tiles=n_tiles),
        grid=(n_tiles + 1,),
        in_specs=[pl.BlockSpec((1, t, sb_w), lambda i: (seq(i), cur(i) % tiles_per_seq, 0)),
                  whole_seq((1, s, sb_w)), whole_seq((1, sb_w, s)),
                  pl.BlockSpec((t, d), prv),
                  pl.BlockSpec((t, ohg.shape[1]), prv),
                  pl.BlockSpec((t, 2 * d), prv),
                  full(bg), full(wsb), full(whg), full(wout), full(g2), full(w1), full(w2),
                  full(gf)],
        out_specs=pl.BlockSpec((t, d), prv),
        out_shape=jax.ShapeDtypeStruct((n, d), x2.dtype),
        scratch_shapes=[pltpu.VMEM((2, sb_w, t), BF16),
                        pltpu.VMEM((SB_HEADS, t, V7X_LANES), BF16),
                        pltpu.VMEM((SB_HEADS, SB_HEAD_DIM, t), F32),
                        pltpu.VMEM((SB_HEADS, t), F32)],
        compiler_params=pltpu.CompilerParams(
            dimension_semantics=("arbitrary",),
            vmem_limit_bytes=_vmem_limit(moving, resident_bytes,
                                         t * (dff * 6 + d * 24) + 8 * SB_HEADS * t * t * 4)),
        name="attn_ffn",
    )(q3, k3, vt3, x2, ohg, gates, bg, wsb, whg, wout, g2, w1, w2, gf)


def kernel(x, norm1_g, w_in, b_gate, lb_logits, hg_norm_g, w_o_sb, w_o_hg, w_out, norm2_g,
           w_ff1, w_ff2, final_g):
    b, s, d = x.shape
    assert w_in.shape[0] == 1, "single-layer block"
    sb_w = SB_HEADS * SB_HEAD_DIM
    hg_w = HG_HEADS * HG_HEAD_DIM
    x2 = x.reshape(b * s, d)
    later = (w_o_sb[0], w_o_hg[0], w_out[0], w_ff1[0], w_ff2[0])
    q, k, vt, gates, o_hg, wsb, whg, wout, w1, w2 = _proj_hgrn2(
        x2, norm1_g, w_in[0].astype(BF16), lb_logits, hg_norm_g, later, b, sb_w, hg_w)
    out = _attn_ffn(q.reshape(b, s, sb_w), k.reshape(b, s, sb_w), vt, x2, o_hg, gates, b_gate,
                    wsb, whg, wout, norm2_g, w1, w2, final_g.reshape(1, d))
    return out.reshape(b, s, d)
```

```python
import functools

import jax
import jax.numpy as jnp
import numpy as np
from jax import lax
from jax.experimental import pallas as pl
from jax.experimental.pallas import tpu as pltpu

F32 = jnp.float32
BF16 = jnp.bfloat16

SB_HEADS = 8
SB_HEAD_DIM = 64
HG_HEADS = 4
HG_HEAD_DIM = 128
EPS = 1e-6
LOG2E = 1.4426950408889634
SB_DEAD_CARRY = 151.0

V7X_LANES = 128
V7X_MXU_DIM = 256
V7X_VMEM_BYTES = 64 * 1024 * 1024

ATTN_TILE = V7X_MXU_DIM
HG_BLOCK = V7X_MXU_DIM
PROJ_BLOCKS = 2


def _vmem_limit(pipelined_bytes, resident_bytes, temp_bytes):
    need = 2 * pipelined_bytes + resident_bytes + temp_bytes
    return int(min(need + need // 4, V7X_VMEM_BYTES - 8 * 1024 * 1024))


def _resident(shape):
    return pl.BlockSpec(shape, lambda *_: (0,) * len(shape), pipeline_mode=pl.Buffered(1))


def _rms(x, g):
    ms = jnp.mean(x * x, axis=-1, keepdims=True)
    return x * lax.rsqrt(ms + EPS) * g


def _dot(a, b):
    return jnp.dot(a, b, preferred_element_type=F32)


def _dot_nt(a, b):
    return lax.dot_general(a, b, (((1,), (1,)), ((), ())), preferred_element_type=F32)


def _dot_tn(a, b):
    return lax.dot_general(a, b, (((0,), (0,)), ((), ())), preferred_element_type=F32)


def _softplus2(z, mask):
    sp = jnp.maximum(z, 0.0) + jnp.log(1.0 + jnp.exp2(-jnp.abs(z))) * LOG2E
    return sp if mask is None else jnp.where(mask, sp, 0.0)


def _split3(x):
    a = x.astype(BF16)
    r = x - a.astype(F32)
    b = r.astype(BF16)
    c = (r - b.astype(F32)).astype(BF16)
    return a, b, c


def _rows_from_group(b, group, r):
    n, c = b.shape
    if group == n:
        return jnp.broadcast_to(b[r:r + 1, :], (n, c))
    b3 = b.reshape(n // group, group, c)
    return jnp.broadcast_to(b3[:, r:r + 1, :], b3.shape).reshape(n, c)


def _midpoint_rows(b, group, pos):
    n = b.shape[0]
    half = group // 2
    if group >= 16:
        return _rows_from_group(b, group, half - 1)
    if group == 8:
        return _rows_from_group(b, 8, 3)
    up1 = pltpu.roll(b, n - 1, 0)
    dn1 = pltpu.roll(b, 1, 0)
    if group == 2:
        return jnp.where((pos & 1) == 0, b, dn1)
    assert group == 4
    dn2 = pltpu.roll(b, 2, 0)
    r4 = pos & 3
    return jnp.where(r4 == 0, up1, jnp.where(r4 == 1, b, jnp.where(r4 == 2, dn1, dn2)))


def _hgrn2_gates(fr, qr, lb):
    e = jnp.exp(-jnp.abs(fr))
    r = 1.0 / (1.0 + e)
    er = e * r
    sig = jnp.where(fr >= 0, r, er)
    nsig = jnp.where(fr >= 0, er, r)
    logf2 = jnp.log(lb + (1.0 - lb) * sig) * LOG2E
    k = (1.0 - lb) * nsig
    q = qr / (1.0 + jnp.exp(-qr))
    return logf2, k, q


def _hgrn2_head_products(b, q, k, v, lvl, pos, st_ref, h, keep):
    t, dk = b.shape
    n_levels = t.bit_length() - 1
    hb = t // 2
    zeros = jnp.zeros((hb, dk), BF16)
    lvl_d = jnp.concatenate([lvl[0:hb, 0:hb], lvl[hb:t, hb:t]], axis=1)
    diag = jnp.zeros((hb, t), F32)
    for level in range(1, n_levels):
        group = 1 << level
        d = b - _midpoint_rows(b, group, pos)
        later = (pos & (group - 1)) >= (group // 2)
        fac = jnp.exp2(-jnp.abs(d))
        ql = jnp.where(later, q * fac, 0.0).astype(BF16)
        kl = jnp.where(later, 0.0, k * fac).astype(BF16)
        lhs = jnp.concatenate([ql[0:hb], ql[hb:t]], axis=1)
        rhs = jnp.concatenate([jnp.concatenate([kl[0:hb], zeros], axis=1),
                               jnp.concatenate([zeros, kl[hb:t]], axis=1)], axis=0)
        diag = jnp.where(lvl_d == level, _dot_nt(lhs, rhs), diag)
    b_mid = b[hb - 1:hb, :]
    q_top = (q[hb:t] * jnp.exp2(b[hb:t] - b_mid)).astype(BF16)
    k_top = (k[0:hb] * jnp.exp2(b_mid - b[0:hb])).astype(BF16)
    top = _dot_nt(q_top, k_top)
    scores = jnp.concatenate(
        [jnp.concatenate([diag[:, 0:hb], jnp.zeros((hb, hb), F32)], axis=1),
         jnp.concatenate([top, diag[:, hb:t]], axis=1)], axis=0)

    st = st_ref[h] * keep
    b_last = b[t - 1:t, :]
    o = _dot_nt((q * jnp.exp2(b)).astype(BF16), st.astype(BF16))
    o = o + jnp.sum(q * k, axis=1, keepdims=True) * v.astype(F32)
    k_dec = (k * jnp.exp2(b_last - b)).astype(BF16)
    st_ref[h] = jnp.exp2(b_last) * st + _dot_tn(v, k_dec)
    return scores.astype(BF16), o


def _hgrn2_head_output(scores, o, v, gr, ng):
    return _rms(o + _dot(scores, v), ng) * (gr / (1.0 + jnp.exp(-gr)))


def _proj_hgrn2_kernel(*refs, tiles_per_seq, n_cast):
    x0_ref, xnext_ref, g_ref, w_ref, lbl_ref, ng_ref, lvl_ref = refs[:7]
    cast_in = refs[7:7 + n_cast]
    q_ref, k_ref, vt_ref, gate_ref, ohg_ref = refs[7 + n_cast:12 + n_cast]
    cast_out = refs[12 + n_cast:12 + 2 * n_cast]
    xn_buf, f_buf, iqg_buf, st_ref = refs[12 + 2 * n_cast:]
    d = xnext_ref.shape[1]
    t = HG_BLOCK
    n_blocks = xnext_ref.shape[0] // t
    dk = HG_HEAD_DIM
    hg_w = HG_HEADS * dk
    sb_w = q_ref.shape[1]
    i = pl.program_id(0)
    n_steps = pl.num_programs(0)
    slot = i % 2
    prev = 1 - slot

    def step(do_proj, do_hgrn2):
        row = lax.broadcasted_iota(jnp.int32, (t, t), 0)
        col = lax.broadcasted_iota(jnp.int32, (t, t), 1)
        tril = (row >= col).astype(BF16)
        pos = lax.broadcasted_iota(jnp.int32, (t, dk), 0)
        lvl = lvl_ref[...]
        first_keep = jnp.where(i % tiles_per_seq == 1, 0.0, 1.0)

        lbl = lbl_ref[...]
        ex = jnp.exp(lbl - jnp.max(lbl, axis=0, keepdims=True))
        lb_all = ex[0:1, :] / jnp.sum(ex, axis=0, keepdims=True)

        def prepare_block(blk):
            rows = slice(blk * t, (blk + 1) * t)
            logf2, k_all, q_all = _hgrn2_gates(
                f_buf[prev, rows, :], iqg_buf[prev, rows, hg_w:2 * hg_w].astype(F32), lb_all)
            g1, g2, g3 = _split3(logf2)
            return _dot(tril, g1) + _dot(tril, g2) + _dot(tril, g3), q_all, k_all

        def head_products(blk, h, prepared):
            b_all, q_all, k_all = prepared
            rows = slice(blk * t, (blk + 1) * t)
            sl = slice(h * dk, (h + 1) * dk)
            return _hgrn2_head_products(b_all[:, sl], q_all[:, sl], k_all[:, sl],
                                        iqg_buf[prev, rows, sl], lvl, pos, st_ref, h,
                                        first_keep if blk == 0 else 1.0)

        def head_output(blk, h, products):
            rows = slice(blk * t, (blk + 1) * t)
            sl = slice(h * dk, (h + 1) * dk)
            gr = iqg_buf[prev, rows, 2 * hg_w + h * dk:2 * hg_w + (h + 1) * dk].astype(F32)
            y = _hgrn2_head_output(*products, iqg_buf[prev, rows, sl], gr, ng_ref[:, sl])
            ohg_ref[rows, sl] = y.astype(ohg_ref.dtype)

        c0 = 3 * sb_w
        c1 = c0 + hg_w
        c2 = c1 + 3 * hg_w
        slab = d // 2
        slabs = []
        if do_proj:
            xn = xn_buf[slot]

            def project(lo, hi):
                return _dot(xn, w_ref[:, lo:hi])

            def q_slab():
                q_ref[...] = (project(0, sb_w) * (SB_HEAD_DIM ** -0.5 * LOG2E)).astype(q_ref.dtype)

            def k_slab():
                k_ref[...] = project(sb_w, 2 * sb_w).astype(k_ref.dtype)

            def v_slab():
                v = project(2 * sb_w, 3 * sb_w)
                for blk in range(n_blocks):
                    vt_ref[0, :, blk * t:(blk + 1) * t] = \
                        v[blk * t:(blk + 1) * t].T.astype(vt_ref.dtype)

            def f_slab():
                f_buf[slot] = project(c0, c1)

            def iqg_slab(lo):
                def run():
                    iqg_buf[slot, :, lo:lo + slab] = \
                        project(c1 + lo, c1 + lo + slab).astype(iqg_buf.dtype)
                return run

            def gate_slab(lo):
                def run():
                    gate_ref[:, lo:lo + slab] = project(c2 + lo, c2 + lo + slab)
                return run

            for src, dst in zip(cast_in, cast_out):
                dst[...] = src[...].astype(dst.dtype)
            slabs = [v_slab, f_slab] + [iqg_slab(lo) for lo in range(0, 3 * hg_w, slab)] \
                + [gate_slab(lo) for lo in range(0, 2 * d, slab)]
            q_slab()
            k_slab()
        if do_hgrn2:
            prepared = prepare_block(0)
            if slabs:
                slabs.pop(0)()
            for blk in range(n_blocks):
                for h in range(HG_HEADS):
                    if blk > 0 and h == 0:
                        prepared = prepare_block(blk)
                    products = head_products(blk, h, prepared)
                    if slabs:
                        slabs.pop(0)()
                    head_output(blk, h, products)
        for run in slabs:
            run()
        if do_proj:
            xn_buf[prev] = _rms(xnext_ref[...], g_ref[...]).astype(BF16)

    @pl.when(i == 0)
    def _():
        st_ref[...] = jnp.zeros_like(st_ref)
        xn_buf[0] = _rms(x0_ref[...], g_ref[...]).astype(BF16)
        step(True, False)

    @pl.when(jnp.logical_and(i > 0, i < n_steps - 1))
    def _():
        step(True, True)

    @pl.when(i == n_steps - 1)
    def _():
        step(False, True)


def _pair_levels(t):
    idx = np.arange(t)
    x = idx[:, None] ^ idx[None, :]
    lev = np.where(x > 0, np.floor(np.log2(np.maximum(x, 1))).astype(np.int64) + 1, 0)
    return np.where(idx[:, None] > idx[None, :], lev, 0).astype(np.int32)


def _proj_hgrn2(x2, g1, w_in, lb_logits, ng, later_weights, batch, sb_w, hg_w):
    n, d = x2.shape
    t = PROJ_BLOCKS * HG_BLOCK
    cols = w_in.shape[1]
    n_tiles = n // t
    tiles_per_seq = n_tiles // batch
    assert hg_w == HG_HEADS * HG_HEAD_DIM and n_tiles * t == n and tiles_per_seq * batch == n_tiles
    lvl = jnp.asarray(_pair_levels(HG_BLOCK))
    cur = lambda i: jnp.minimum(i, n_tiles - 1)
    rows = lambda w: pl.BlockSpec((t, w), lambda i: (cur(i), 0))
    vt_spec = pl.BlockSpec((1, sb_w, t),
                           lambda i: (cur(i) // tiles_per_seq, 0, cur(i) % tiles_per_seq))
    sds = lambda w, dt: jax.ShapeDtypeStruct((n, w), dt)
    cast_specs = [pl.BlockSpec((w.shape[0] // n_tiles, w.shape[1]), lambda i: (cur(i), 0))
                  for w in later_weights]
    assert all(w.shape[0] % (n_tiles * 16) == 0 for w in later_weights)
    cast_bytes = sum(w.size // n_tiles * 6 for w in later_weights)
    moving = t * (d * 4 + 2 * sb_w * 2 + sb_w * 2 + 2 * d * 4 + hg_w * 2) + cast_bytes
    resident = d * 4 + d * cols * 2 + 3 * hg_w * 4 + HG_BLOCK * HG_BLOCK * 4 + t * d * 4 \
        + 2 * t * d * 2 + 2 * t * hg_w * (4 + 3 * 2) + HG_HEADS * HG_HEAD_DIM * HG_HEAD_DIM * 4
    return pl.pallas_call(
        functools.partial(_proj_hgrn2_kernel, tiles_per_seq=tiles_per_seq,
                          n_cast=len(later_weights)),
        grid=(n_tiles + 1,),
        in_specs=[_resident((t, d)), pl.BlockSpec((t, d), lambda i: (cur(i + 1), 0)),
                  _resident((1, d)), _resident((d, cols)),
                  _resident(lb_logits.shape), _resident(ng.shape),
                  _resident((HG_BLOCK, HG_BLOCK))] + cast_specs,
        out_specs=[rows(sb_w), rows(sb_w), vt_spec, rows(2 * d),
                   pl.BlockSpec((t, hg_w), lambda i: (jnp.maximum(i - 1, 0), 0))] + cast_specs,
        out_shape=[sds(sb_w, BF16), sds(sb_w, BF16),
                   jax.ShapeDtypeStruct((batch, sb_w, n // batch), BF16),
                   sds(2 * d, F32), sds(hg_w, BF16)]
        + [jax.ShapeDtypeStruct(w.shape, BF16) for w in later_weights],
        scratch_shapes=[pltpu.VMEM((2, t, d), BF16),
                        pltpu.VMEM((2, t, hg_w), F32),
                        pltpu.VMEM((2, t, 3 * hg_w), BF16),
                        pltpu.VMEM((HG_HEADS, HG_HEAD_DIM, HG_HEAD_DIM), F32)],
        compiler_params=pltpu.CompilerParams(
            dimension_semantics=("arbitrary",),
            vmem_limit_bytes=_vmem_limit(moving, resident,
                                         t * (d * 6 + cols * 4) + 64 * HG_BLOCK * HG_BLOCK * 4)),
        name="proj_hgrn2",
    )(x2, x2, g1, w_in, lb_logits, ng, lvl, *later_weights)


def _attn_ffn_kernel(q_ref, k_ref, vt_ref, x_ref, ohg_ref, gate_ref, bg_ref, wsb_ref, whg_ref,
                     wout_ref, g2_ref, w1_ref, w2_ref, gf_ref, o_ref,
                     osb_buf, qm_ref, acc_ref, carry_ref, *, tiles_per_seq, n_tiles):
    t = q_ref.shape[1]
    d = x_ref.shape[1]
    heads = range(SB_HEADS)
    i = pl.program_id(0)
    slot = i % 2
    qi = jnp.minimum(i, n_tiles - 1) % tiles_per_seq

    row = lax.broadcasted_iota(jnp.int32, (t, t), 0)
    col = lax.broadcasted_iota(jnp.int32, (t, t), 1)
    tri = (col > row).astype(BF16)
    causal = row < col

    def logits(j):
        start = pl.multiple_of(j * t, t)
        k = k_ref[0, pl.ds(start, t), :]
        return [_dot_nt(k[:, (h // 2) * V7X_LANES:(h // 2 + 1) * V7X_LANES], qm_ref[h])
                for h in heads]

    def values_t(j, h):
        return vt_ref[0, h * SB_HEAD_DIM:(h + 1) * SB_HEAD_DIM, pl.ds(pl.multiple_of(j * t, t), t)]

    def softplus_phase(z, mask):
        sp = [_softplus2(z[h], mask) for h in heads]
        return sp, [sp[h].astype(BF16) for h in heads]

    def cumsum_phase(spb):
        return [_dot(tri, spb[h]) for h in heads]

    def weight_phase(z, sp, later, mask):
        w = [jnp.exp2(z[h] - sp[h] - later[h]) for h in heads]
        if mask is not None:
            w = [jnp.where(mask, w[h], 0.0) for h in heads]
        return [w[h].astype(BF16) for h in heads]

    def value_phase(j, w, later, spb):
        pv = [_dot(values_t(j, h), w[h]) for h in heads]
        return pv, [later[h][0:1, :] + spb[h][0:1, :].astype(F32) for h in heads]

    def main_block(do_attn, do_ffn):
        has_prev = qi > 0
        jp = jnp.maximum(qi - 1, 0)
        if do_attn:
            lane = lax.broadcasted_iota(jnp.int32, (t, V7X_LANES), 1)
            zero = jnp.zeros((), BF16)
            q = q_ref[0]
            for h in heads:
                grp = q[:, (h // 2) * V7X_LANES:(h // 2 + 1) * V7X_LANES]
                qm_ref[h] = jnp.where((lane // SB_HEAD_DIM) == (h % 2), grp, zero)
        if do_ffn:
            a_sb = _dot_tn(osb_buf[1 - slot], wsb_ref[...])
            a_hg = _dot(ohg_ref[...], whg_ref[...])
        if do_attn:
            z0 = logits(qi)
        if do_ffn:
            gates = 1.0 / (1.0 + jnp.exp(-(gate_ref[...] + bg_ref[...])))
            merged = (gates[:, :d] * a_sb + gates[:, d:] * a_hg).astype(BF16)
            hres = x_ref[...] + _dot(merged, wout_ref[...])
        if do_attn:
            z1 = logits(jp)
        if do_ffn:
            hn = _rms(hres, g2_ref[...]).astype(BF16)
            half = w1_ref.shape[1] // 2

            def mlp_up(lo):
                act = jnp.maximum(_dot(hn, w1_ref[:, lo:lo + half]), 0.0)
                return (act * act).astype(BF16)

            act_a = mlp_up(0)
        if do_attn:
            sp0, spb0 = softplus_phase(z0, causal)
            lat0 = cumsum_phase(spb0)
        if do_ffn:
            act_b = mlp_up(half)
        if do_attn:
            sp1, spb1 = softplus_phase(z1, None)
            lat1 = cumsum_phase(spb1)
        if do_ffn:
            hres = hres + _dot(act_a, w2_ref[0:half, :])
        if do_attn:
            w0 = weight_phase(z0, sp0, lat0, causal)
            w1 = weight_phase(z1, sp1, lat1, None)
        if do_ffn:
            hres = hres + _dot(act_b, w2_ref[half:2 * half, :])
        if do_attn:
            pv0, tot0 = value_phase(qi, w0, lat0, spb0)
            pv1, tot1 = value_phase(jp, w1, lat1, spb1)
        if do_ffn:
            o_ref[...] = _rms(hres, gf_ref[...]).astype(o_ref.dtype)
        if do_attn:
            for h in heads:
                scale = jnp.where(has_prev, jnp.exp2(-tot0[h]), 0.0)
                acc_ref[h] = pv0[h] + scale * pv1[h]
                carry_ref[h:h + 1, :] = tot0[h] + jnp.where(has_prev, tot1[h], 0.0)

    def sweep_rest():
        def more(state):
            n, live = state
            return jnp.logical_and(n < qi, live)

        def body(state):
            n, _ = state
            j = qi - 1 - n
            z = logits(j)
            sp, spb = softplus_phase(z, None)
            later = cumsum_phase(spb)
            pv, tot = value_phase(j, weight_phase(z, sp, later, None), later, spb)
            for h in heads:
                c = carry_ref[h:h + 1, :]
                acc_ref[h] += jnp.exp2(-c) * pv[h]
                carry_ref[h:h + 1, :] = c + tot[h]
            return n + 1, jnp.min(carry_ref[...]) < SB_DEAD_CARRY

        lax.while_loop(more, body, (jnp.int32(1), jnp.min(carry_ref[...]) < SB_DEAD_CARRY))
        osb_buf[slot] = acc_ref[...].reshape(SB_HEADS * SB_HEAD_DIM, t).astype(osb_buf.dtype)

    @pl.when(i == 0)
    def _():
        main_block(True, False)
        sweep_rest()

    @pl.when(jnp.logical_and(i > 0, i < n_tiles))
    def _():
        main_block(True, True)
        sweep_rest()

    @pl.when(i == n_tiles)
    def _():
        main_block(False, True)


def _attn_ffn(q3, k3, vt3, x2, ohg, gates, bg, wsb, whg, wout, g2, w1, w2, gf):
    b, s, sb_w = q3.shape
    n, d = x2.shape
    t = ATTN_TILE
    dff = w1.shape[1]
    tiles_per_seq = s // t
    n_tiles = n // t
    assert sb_w == SB_HEADS * SB_HEAD_DIM and tiles_per_seq * t == s and n_tiles == b * tiles_per_seq
    cur = lambda i: jnp.minimum(i, n_tiles - 1)
    prv = lambda i: (jnp.maximum(i - 1, 0), 0)
    seq = lambda i: cur(i) // tiles_per_seq
    whole_seq = lambda shape: pl.BlockSpec(shape, lambda i: (seq(i), 0, 0),
                                           pipeline_mode=pl.Buffered(1))
    full = lambda a: _resident(a.shape)
    resident_bytes = sum(a.size * a.dtype.itemsize for a in (bg, wsb, whg, wout, g2, w1, w2, gf)) \
        + 2 * s * sb_w * 2 + 2 * t * sb_w * 2 + SB_HEADS * t * (V7X_LANES * 2 + SB_HEAD_DIM * 4 + 4)
    moving = t * (sb_w * 2 + d * 4 + ohg.shape[1] * 2 + 2 * d * 4 + d * 4)
    return pl.pallas_call(
        functools.partial(_attn_ffn_kernel, tiles_per_seq=tiles_per_seq, n_tiles=n_tiles),
        grid=(n_tiles + 1,),
        in_specs=[pl.BlockSpec((1, t, sb_w), lambda i: (seq(i), cur(i) % tiles_per_seq, 0)),
                  whole_seq((1, s, sb_w)), whole_seq((1, sb_w, s)),
                  pl.BlockSpec((t, d), prv),
                  pl.BlockSpec((t, ohg.shape[1]), prv),
                  pl.BlockSpec((t, 2 * d), prv),
                  full(bg), full(wsb), full(whg), full(wout), full(g2), full(w1), full(w2),
                  full(gf)],
        out_specs=pl.BlockSpec((t, d), prv),
        out_shape=jax.ShapeDtypeStruct((n, d), x2.dtype),
        scratch_shapes=[pltpu.VMEM((2, sb_w, t), BF16),
                        pltpu.VMEM((SB_HEADS, t, V7X_LANES), BF16),
                        pltpu.VMEM((SB_HEADS, SB_HEAD_DIM, t), F32),
                        pltpu.VMEM((SB_HEADS, t), F32)],
        compiler_params=pltpu.CompilerParams(
            dimension_semantics=("arbitrary",),
            vmem_limit_bytes=_vmem_limit(moving, resident_bytes,
                                         t * (dff * 6 + d * 24) + 8 * SB_HEADS * t * t * 4)),
        name="attn_ffn",
    )(q3, k3, vt3, x2, ohg, gates, bg, wsb, whg, wout, g2, w1, w2, gf)


def kernel(x, norm1_g, w_in, b_gate, lb_logits, hg_norm_g, w_o_sb, w_o_hg, w_out, norm2_g,
           w_ff1, w_ff2, final_g):
    b, s, d = x.shape
    assert w_in.shape[0] == 1, "single-layer block"
    sb_w = SB_HEADS * SB_HEAD_DIM
    hg_w = HG_HEADS * HG_HEAD_DIM
    x2 = x.reshape(b * s, d)
    later = (w_o_sb[0], w_o_hg[0], w_out[0], w_ff1[0], w_ff2[0])
    q, k, vt, gates, o_hg, wsb, whg, wout, w1, w2 = _proj_hgrn2(
        x2, norm1_g, w_in[0].astype(BF16), lb_logits, hg_norm_g, later, b, sb_w, hg_w)
    out = _attn_ffn(q.reshape(b, s, sb_w), k.reshape(b, s, sb_w), vt, x2, o_hg, gates, b_gate,
                    wsb, whg, wout, norm2_g, w1, w2, final_g.reshape(1, d))
    return out.reshape(b, s, d)
```

```python
import functools

import jax
import jax.numpy as jnp
import numpy as np
from jax import lax
from jax.experimental import pallas as pl
from jax.experimental.pallas import tpu as pltpu

F32 = jnp.float32
BF16 = jnp.bfloat16

SB_HEADS = 8
SB_HEAD_DIM = 64
HG_HEADS = 4
HG_HEAD_DIM = 128
EPS = 1e-6
LOG2E = 1.4426950408889634
SB_DEAD_CARRY = 151.0

V7X_LANES = 128
V7X_MXU_DIM = 256
V7X_VMEM_BYTES = 64 * 1024 * 1024

ATTN_TILE = V7X_MXU_DIM
HG_BLOCK = V7X_MXU_DIM
PROJ_BLOCKS = 2


def _vmem_limit(pipelined_bytes, resident_bytes, temp_bytes):
    need = 2 * pipelined_bytes + resident_bytes + temp_bytes
    return int(min(need + need // 4, V7X_VMEM_BYTES - 8 * 1024 * 1024))


def _resident(shape):
    return pl.BlockSpec(shape, lambda *_: (0,) * len(shape), pipeline_mode=pl.Buffered(1))


def _rms(x, g):
    ms = jnp.mean(x * x, axis=-1, keepdims=True)
    return x * lax.rsqrt(ms + EPS) * g


def _dot(a, b):
    return jnp.dot(a, b, preferred_element_type=F32)


def _dot_nt(a, b):
    return lax.dot_general(a, b, (((1,), (1,)), ((), ())), preferred_element_type=F32)


def _dot_tn(a, b):
    return lax.dot_general(a, b, (((0,), (0,)), ((), ())), preferred_element_type=F32)


def _softplus2(z, mask):
    sp = jnp.maximum(z, 0.0) + jnp.log(1.0 + jnp.exp2(-jnp.abs(z))) * LOG2E
    return sp if mask is None else jnp.where(mask, sp, 0.0)


def _split3(x):
    a = x.astype(BF16)
    r = x - a.astype(F32)
    b = r.astype(BF16)
    c = (r - b.astype(F32)).astype(BF16)
    return a, b, c


def _rows_from_group(b, group, r):
    n, c = b.shape
    if group == n:
        return jnp.broadcast_to(b[r:r + 1, :], (n, c))
    b3 = b.reshape(n // group, group, c)
    return jnp.broadcast_to(b3[:, r:r + 1, :], b3.shape).reshape(n, c)


def _midpoint_rows(b, group, pos):
    n = b.shape[0]
    half = group // 2
    if group >= 16:
        return _rows_from_group(b, group, half - 1)
    if group == 8:
        return _rows_from_group(b, 8, 3)
    up1 = pltpu.roll(b, n - 1, 0)
    dn1 = pltpu.roll(b, 1, 0)
    if group == 2:
        return jnp.where((pos & 1) == 0, b, dn1)
    assert group == 4
    dn2 = pltpu.roll(b, 2, 0)
    r4 = pos & 3
    return jnp.where(r4 == 0, up1, jnp.where(r4 == 1, b, jnp.where(r4 == 2, dn1, dn2)))


def _hgrn2_gates(fr, qr, lb):
    e = jnp.exp(-jnp.abs(fr))
    r = 1.0 / (1.0 + e)
    er = e * r
    sig = jnp.where(fr >= 0, r, er)
    nsig = jnp.where(fr >= 0, er, r)
    logf2 = jnp.log(lb + (1.0 - lb) * sig) * LOG2E
    k = (1.0 - lb) * nsig
    q = qr / (1.0 + jnp.exp(-qr))
    return logf2, k, q


def _hgrn2_head_products(b, q, k, v, lvl, pos, st_ref, h, keep):
    t, dk = b.shape
    n_levels = t.bit_length() - 1
    hb = t // 2
    zeros = jnp.zeros((hb, dk), BF16)
    lvl_d = jnp.concatenate([lvl[0:hb, 0:hb], lvl[hb:t, hb:t]], axis=1)
    diag = jnp.zeros((hb, t), F32)
    for level in range(1, n_levels):
        group = 1 << level
        d = b - _midpoint_rows(b, group, pos)
        later = (pos & (group - 1)) >= (group // 2)
        fac = jnp.exp2(-jnp.abs(d))
        ql = jnp.where(later, q * fac, 0.0).astype(BF16)
        kl = jnp.where(later, 0.0, k * fac).astype(BF16)
        lhs = jnp.concatenate([ql[0:hb], ql[hb:t]], axis=1)
        rhs = jnp.concatenate([jnp.concatenate([kl[0:hb], zeros], axis=1),
                               jnp.concatenate([zeros, kl[hb:t]], axis=1)], axis=0)
        diag = jnp.where(lvl_d == level, _dot_nt(lhs, rhs), diag)
    b_mid = b[hb - 1:hb, :]
    q_top = (q[hb:t] * jnp.exp2(b[hb:t] - b_mid)).astype(BF16)
    k_top = (k[0:hb] * jnp.exp2(b_mid - b[0:hb])).astype(BF16)
    top = _dot_nt(q_top, k_top)
    scores = jnp.concatenate(
        [jnp.concatenate([diag[:, 0:hb], jnp.zeros((hb, hb), F32)], axis=1),
         jnp.concatenate([top, diag[:, hb:t]], axis=1)], axis=0)

    st = st_ref[h] * keep
    b_last = b[t - 1:t, :]
    o = _dot_nt((q * jnp.exp2(b)).astype(BF16), st.astype(BF16))
    o = o + jnp.sum(q * k, axis=1, keepdims=True) * v.astype(F32)
    k_dec = (k * jnp.exp2(b_last - b)).astype(BF16)
    st_ref[h] = jnp.exp2(b_last) * st + _dot_tn(v, k_dec)
    return scores.astype(BF16), o


def _hgrn2_head_output(scores, o, v, gr, ng):
    return _rms(o + _dot(scores, v), ng) * (gr / (1.0 + jnp.exp(-gr)))


def _proj_hgrn2_kernel(*refs, tiles_per_seq, n_cast):
    x0_ref, xnext_ref, g_ref, w_ref, lbl_ref, ng_ref, lvl_ref = refs[:7]
    cast_in = refs[7:7 + n_cast]
    q_ref, k_ref, vt_ref, gate_ref, ohg_ref = refs[7 + n_cast:12 + n_cast]
    cast_out = refs[12 + n_cast:12 + 2 * n_cast]
    xn_buf, f_buf, iqg_buf, st_ref = refs[12 + 2 * n_cast:]
    d = xnext_ref.shape[1]
    t = HG_BLOCK
    n_blocks = xnext_ref.shape[0] // t
    dk = HG_HEAD_DIM
    hg_w = HG_HEADS * dk
    sb_w = q_ref.shape[1]
    i = pl.program_id(0)
    n_steps = pl.num_programs(0)
    slot = i % 2
    prev = 1 - slot

    def step(do_proj, do_hgrn2):
        row = lax.broadcasted_iota(jnp.int32, (t, t), 0)
        col = lax.broadcasted_iota(jnp.int32, (t, t), 1)
        tril = (row >= col).astype(BF16)
        pos = lax.broadcasted_iota(jnp.int32, (t, dk), 0)
        lvl = lvl_ref[...]
        first_keep = jnp.where(i % tiles_per_seq == 1, 0.0, 1.0)

        lbl = lbl_ref[...]
        ex = jnp.exp(lbl - jnp.max(lbl, axis=0, keepdims=True))
        lb_all = ex[0:1, :] / jnp.sum(ex, axis=0, keepdims=True)

        def prepare_block(blk):
            rows = slice(blk * t, (blk + 1) * t)
            logf2, k_all, q_all = _hgrn2_gates(
                f_buf[prev, rows, :], iqg_buf[prev, rows, hg_w:2 * hg_w].astype(F32), lb_all)
            g1, g2, g3 = _split3(logf2)
            return _dot(tril, g1) + _dot(tril, g2) + _dot(tril, g3), q_all, k_all

        def head_products(blk, h, prepared):
            b_all, q_all, k_all = prepared
            rows = slice(blk * t, (blk + 1) * t)
            sl = slice(h * dk, (h + 1) * dk)
            return _hgrn2_head_products(b_all[:, sl], q_all[:, sl], k_all[:, sl],
                                        iqg_buf[prev, rows, sl], lvl, pos, st_ref, h,
                                        first_keep if blk == 0 else 1.0)

        def head_output(blk, h, products):
            rows = slice(blk * t, (blk + 1) * t)
            sl = slice(h * dk, (h + 1) * dk)
            gr = iqg_buf[prev, rows, 2 * hg_w + h * dk:2 * hg_w + (h + 1) * dk].astype(F32)
            y = _hgrn2_head_output(*products, iqg_buf[prev, rows, sl], gr, ng_ref[:, sl])
            ohg_ref[rows, sl] = y.astype(ohg_ref.dtype)

        c0 = 3 * sb_w
        c1 = c0 + hg_w
        c2 = c1 + 3 * hg_w
        slab = d // 2
        slabs = []
        if do_proj:
            xn = xn_buf[slot]

            def project(lo, hi):
                return _dot(xn, w_ref[:, lo:hi])

            def q_slab():
                q_ref[...] = (project(0, sb_w) * (SB_HEAD_DIM ** -0.5 * LOG2E)).astype(q_ref.dtype)

            def k_slab():
                k_ref[...] = project(sb_w, 2 * sb_w).astype(k_ref.dtype)

            def v_slab():
                v = project(2 * sb_w, 3 * sb_w)
                for blk in range(n_blocks):
                    vt_ref[0, :, blk * t:(blk + 1) * t] = \
                        v[blk * t:(blk + 1) * t].T.astype(vt_ref.dtype)

            def f_slab():
                f_buf[slot] = project(c0, c1)

            def iqg_slab(lo):
                def run():
                    iqg_buf[slot, :, lo:lo + slab] = \
                        project(c1 + lo, c1 + lo + slab).astype(iqg_buf.dtype)
                return run

            def gate_slab(lo):
                def run():
                    gate_ref[:, lo:lo + slab] = project(c2 + lo, c2 + lo + slab)
                return run

            for src, dst in zip(cast_in, cast_out):
                dst[...] = src[...].astype(dst.dtype)
            slabs = [v_slab, f_slab] + [iqg_slab(lo) for lo in range(0, 3 * hg_w, slab)] \
                + [gate_slab(lo) for lo in range(0, 2 * d, slab)]
            q_slab()
            k_slab()
        if do_hgrn2:
            prepared = prepare_block(0)
            if slabs:
                slabs.pop(0)()
            for blk in range(n_blocks):
                for h in range(HG_HEADS):
                    if blk > 0 and h == 0:
                        prepared = prepare_block(blk)
                    products = head_products(blk, h, prepared)
                    if slabs:
                        slabs.pop(0)()
                    head_output(blk, h, products)
        for run in slabs:
            run()
        if do_proj:
            xn_buf[prev] = _rms(xnext_ref[...], g_ref[...]).astype(BF16)

    @pl.when(i == 0)
    def _():
        st_ref[...] = jnp.zeros_like(st_ref)
        xn_buf[0] = _rms(x0_ref[...], g_ref[...]).astype(BF16)
        step(True, False)

    @pl.when(jnp.logical_and(i > 0, i < n_steps - 1))
    def _():
        step(True, True)

    @pl.when(i == n_steps - 1)
    def _():
        step(False, True)


def _pair_levels(t):
    idx = np.arange(t)
    x = idx[:, None] ^ idx[None, :]
    lev = np.where(x > 0, np.floor(np.log2(np.maximum(x, 1))).astype(np.int64) + 1, 0)
    return np.where(idx[:, None] > idx[None, :], lev, 0).astype(np.int32)


def _proj_hgrn2(x2, g1, w_in, lb_logits, ng, later_weights, batch, sb_w, hg_w):
    n, d = x2.shape
    t = PROJ_BLOCKS * HG_BLOCK
    cols = w_in.shape[1]
    n_tiles = n // t
    tiles_per_seq = n_tiles // batch
    assert hg_w == HG_HEADS * HG_HEAD_DIM and n_tiles * t == n and tiles_per_seq * batch == n_tiles
    lvl = jnp.asarray(_pair_levels(HG_BLOCK))
    cur = lambda i: jnp.minimum(i, n_tiles - 1)
    rows = lambda w: pl.BlockSpec((t, w), lambda i: (cur(i), 0))
    vt_spec = pl.BlockSpec((1, sb_w, t),
                           lambda i: (cur(i) // tiles_per_seq, 0, cur(i) % tiles_per_seq))
    sds = lambda w, dt: jax.ShapeDtypeStruct((n, w), dt)
    cast_specs = [pl.BlockSpec((w.shape[0] // n_tiles, w.shape[1]), lambda i: (cur(i), 0))
                  for w in later_weights]
    assert all(w.shape[0] % (n_tiles * 16) == 0 for w in later_weights)
    cast_bytes = sum(w.size // n_tiles * 6 for w in later_weights)
    moving = t * (d * 4 + 2 * sb_w * 2 + sb_w * 2 + 2 * d * 4 + hg_w * 2) + cast_bytes
    resident = d * 4 + d * cols * 2 + 3 * hg_w * 4 + HG_BLOCK * HG_BLOCK * 4 + t * d * 4 \
        + 2 * t * d * 2 + 2 * t * hg_w * (4 + 3 * 2) + HG_HEADS * HG_HEAD_DIM * HG_HEAD_DIM * 4
    return pl.pallas_call(
        functools.partial(_proj_hgrn2_kernel, tiles_per_seq=tiles_per_seq,
                          n_cast=len(later_weights)),
        grid=(n_tiles + 1,),
        in_specs=[_resident((t, d)), pl.BlockSpec((t, d), lambda i: (cur(i + 1), 0)),
                  _resident((1, d)), _resident((d, cols)),
                  _resident(lb_logits.shape), _resident(ng.shape),
                  _resident((HG_BLOCK, HG_BLOCK))] + cast_specs,
        out_specs=[rows(sb_w), rows(sb_w), vt_spec, rows(2 * d),
                   pl.BlockSpec((t, hg_w), lambda i: (jnp.maximum(i - 1, 0), 0))] + cast_specs,
        out_shape=[sds(sb_w, BF16), sds(sb_w, BF16),
                   jax.ShapeDtypeStruct((batch, sb_w, n // batch), BF16),
                   sds(2 * d, F32), sds(hg_w, BF16)]
        + [jax.ShapeDtypeStruct(w.shape, BF16) for w in later_weights],
        scratch_shapes=[pltpu.VMEM((2, t, d), BF16),
                        pltpu.VMEM((2, t, hg_w), F32),
                        pltpu.VMEM((2, t, 3 * hg_w), BF16),
                        pltpu.VMEM((HG_HEADS, HG_HEAD_DIM, HG_HEAD_DIM), F32)],
        compiler_params=pltpu.CompilerParams(
            dimension_semantics=("arbitrary",),
            vmem_limit_bytes=_vmem_limit(moving, resident,
                                         t * (d * 6 + cols * 4) + 64 * HG_BLOCK * HG_BLOCK * 4)),
        name="proj_hgrn2",
    )(x2, x2, g1, w_in, lb_logits, ng, lvl, *later_weights)


def _attn_ffn_kernel(q_ref, kd_ref, kp_ref, vtd_ref, vtp_ref, k_hbm, vt_hbm, x_ref, ohg_ref,
                     gate_ref, bg_ref, wsb_ref, whg_ref, wout_ref, g2_ref, w1_ref, w2_ref, gf_ref,
                     o_ref, osb_buf, qm_ref, acc_ref, carry_ref, k_buf, vt_buf, sems,
                     *, tiles_per_seq, n_tiles):
    t = q_ref.shape[1]
    d = x_ref.shape[1]
    heads = range(SB_HEADS)
    i = pl.program_id(0)
    slot = i % 2
    qi = jnp.minimum(i, n_tiles - 1) % tiles_per_seq

    row = lax.broadcasted_iota(jnp.int32, (t, t), 0)
    col = lax.broadcasted_iota(jnp.int32, (t, t), 1)
    tri = (col > row).astype(BF16)
    causal = row < col

    def logits(k):
        return [_dot_nt(k[:, (h // 2) * V7X_LANES:(h // 2 + 1) * V7X_LANES], qm_ref[h])
                for h in heads]

    def softplus_phase(z, mask):
        sp = [_softplus2(z[h], mask) for h in heads]
        return sp, [sp[h].astype(BF16) for h in heads]

    def cumsum_phase(spb):
        return [_dot(tri, spb[h]) for h in heads]

    def weight_phase(z, sp, later, mask):
        w = [jnp.exp2(z[h] - sp[h] - later[h]) for h in heads]
        if mask is not None:
            w = [jnp.where(mask, w[h], 0.0) for h in heads]
        return [w[h].astype(BF16) for h in heads]

    def value_phase(vt, w, later, spb):
        pv = [_dot(vt[h * SB_HEAD_DIM:(h + 1) * SB_HEAD_DIM, :], w[h]) for h in heads]
        return pv, [later[h][0:1, :] + spb[h][0:1, :].astype(F32) for h in heads]

    def main_block(do_attn, do_ffn):
        has_prev = qi > 0
        if do_attn:
            lane = lax.broadcasted_iota(jnp.int32, (t, V7X_LANES), 1)
            zero = jnp.zeros((), BF16)
            q = q_ref[0]
            for h in heads:
                grp = q[:, (h // 2) * V7X_LANES:(h // 2 + 1) * V7X_LANES]
                qm_ref[h] = jnp.where((lane // SB_HEAD_DIM) == (h % 2), grp, zero)
        if do_ffn:
            a_sb = _dot_tn(osb_buf[1 - slot], wsb_ref[...])
            a_hg = _dot(ohg_ref[...], whg_ref[...])
        if do_attn:
            z0 = logits(kd_ref[0])
        if do_ffn:
            gates = 1.0 / (1.0 + jnp.exp(-(gate_ref[...] + bg_ref[...])))
            merged = (gates[:, :d] * a_sb + gates[:, d:] * a_hg).astype(BF16)
            hres = x_ref[...] + _dot(merged, wout_ref[...])
        if do_attn:
            z1 = logits(kp_ref[0])
        if do_ffn:
            hn = _rms(hres, g2_ref[...]).astype(BF16)
            half = w1_ref.shape[1] // 2

            def mlp_up(lo):
                act = jnp.maximum(_dot(hn, w1_ref[:, lo:lo + half]), 0.0)
                return (act * act).astype(BF16)

            act_a = mlp_up(0)
        if do_attn:
            sp0, spb0 = softplus_phase(z0, causal)
            lat0 = cumsum_phase(spb0)
        if do_ffn:
            act_b = mlp_up(half)
        if do_attn:
            sp1, spb1 = softplus_phase(z1, None)
            lat1 = cumsum_phase(spb1)
        if do_ffn:
            hres = hres + _dot(act_a, w2_ref[0:half, :])
        if do_attn:
            w0 = weight_phase(z0, sp0, lat0, causal)
            w1 = weight_phase(z1, sp1, lat1, None)
        if do_ffn:
            hres = hres + _dot(act_b, w2_ref[half:2 * half, :])
        if do_attn:
            pv0, tot0 = value_phase(vtd_ref.at[0], w0, lat0, spb0)
            pv1, tot1 = value_phase(vtp_ref.at[0], w1, lat1, spb1)
        if do_ffn:
            o_ref[...] = _rms(hres, gf_ref[...]).astype(o_ref.dtype)
        if do_attn:
            for h in heads:
                scale = jnp.where(has_prev, jnp.exp2(-tot0[h]), 0.0)
                acc_ref[h] = pv0[h] + scale * pv1[h]
                carry_ref[h:h + 1, :] = tot0[h] + jnp.where(has_prev, tot1[h], 0.0)

    def sweep_rest():
        def more(state):
            n, live = state
            return jnp.logical_and(n < qi, live)

        seq = jnp.minimum(i, n_tiles - 1) // tiles_per_seq

        def tile_copies(j):
            start = pl.multiple_of(j * t, t)
            return (pltpu.make_async_copy(k_hbm.at[seq, pl.ds(start, t), :], k_buf, sems.at[0]),
                    pltpu.make_async_copy(vt_hbm.at[seq, :, pl.ds(start, t)], vt_buf, sems.at[1]))

        def body(state):
            n, _ = state
            copies = tile_copies(qi - 1 - n)
            for c in copies:
                c.start()
            for c in copies:
                c.wait()
            z = logits(k_buf[...])
            sp, spb = softplus_phase(z, None)
            later = cumsum_phase(spb)
            pv, tot = value_phase(vt_buf, weight_phase(z, sp, later, None), later, spb)
            for h in heads:
                c = carry_ref[h:h + 1, :]
                acc_ref[h] += jnp.exp2(-c) * pv[h]
                carry_ref[h:h + 1, :] = c + tot[h]
            return n + 1, jnp.min(carry_ref[...]) < SB_DEAD_CARRY

        lax.while_loop(more, body, (jnp.int32(1), jnp.min(carry_ref[...]) < SB_DEAD_CARRY))
        osb_buf[slot] = acc_ref[...].reshape(SB_HEADS * SB_HEAD_DIM, t).astype(osb_buf.dtype)

    @pl.when(i == 0)
    def _():
        main_block(True, False)
        sweep_rest()

    @pl.when(jnp.logical_and(i > 0, i < n_tiles))
    def _():
        main_block(True, True)
        sweep_rest()

    @pl.when(i == n_tiles)
    def _():
        main_block(False, True)


def _attn_ffn(q3, k3, vt3, x2, ohg, gates, bg, wsb, whg, wout, g2, w1, w2, gf):
    b, s, sb_w = q3.shape
    n, d = x2.shape
    t = ATTN_TILE
    dff = w1.shape[1]
    tiles_per_seq = s // t
    n_tiles = n // t
    assert sb_w == SB_HEADS * SB_HEAD_DIM and tiles_per_seq * t == s and n_tiles == b * tiles_per_seq
    cur = lambda i: jnp.minimum(i, n_tiles - 1)
    prv = lambda i: (jnp.maximum(i - 1, 0), 0)
    seq = lambda i: cur(i) // tiles_per_seq
    tile = lambda i: cur(i) % tiles_per_seq
    before = lambda i: jnp.maximum(tile(i) - 1, 0)
    k_tile = lambda pos: pl.BlockSpec((1, t, sb_w), lambda i: (seq(i), pos(i), 0))
    vt_tile = lambda pos: pl.BlockSpec((1, sb_w, t), lambda i: (seq(i), 0, pos(i)))
    in_hbm = pl.BlockSpec(memory_space=pl.ANY)
    full = lambda a: _resident(a.shape)
    resident_bytes = sum(a.size * a.dtype.itemsize for a in (bg, wsb, whg, wout, g2, w1, w2, gf)) \
        + 2 * t * sb_w * 2 + 2 * t * sb_w * 2 + SB_HEADS * t * (V7X_LANES * 2 + SB_HEAD_DIM * 4 + 4)
    moving = t * (5 * sb_w * 2 + d * 4 + ohg.shape[1] * 2 + 2 * d * 4 + d * 4)
    return pl.pallas_call(
        functools.partial(_attn_ffn_kernel, tiles_per_seq=tiles_per_seq, n_tiles=n_tiles),
        grid=(n_tiles + 1,),
        in_specs=[k_tile(tile), k_tile(tile), k_tile(before), vt_tile(tile), vt_tile(before),
                  in_hbm, in_hbm,
                  pl.BlockSpec((t, d), prv),
                  pl.BlockSpec((t, ohg.shape[1]), prv),
                  pl.BlockSpec((t, 2 * d), prv),
                  full(bg), full(wsb), full(whg), full(wout), full(g2), full(w1), full(w2),
                  full(gf)],
        out_specs=pl.BlockSpec((t, d), prv),
        out_shape=jax.ShapeDtypeStruct((n, d), x2.dtype),
        scratch_shapes=[pltpu.VMEM((2, sb_w, t), BF16),
                        pltpu.VMEM((SB_HEADS, t, V7X_LANES), BF16),
                        pltpu.VMEM((SB_HEADS, SB_HEAD_DIM, t), F32),
                        pltpu.VMEM((SB_HEADS, t), F32),
                        pltpu.VMEM((t, sb_w), BF16),
                        pltpu.VMEM((sb_w, t), BF16),
                        pltpu.SemaphoreType.DMA((2,))],
        compiler_params=pltpu.CompilerParams(
            dimension_semantics=("arbitrary",),
            vmem_limit_bytes=_vmem_limit(moving, resident_bytes,
                                         t * (dff * 6 + d * 24) + 8 * SB_HEADS * t * t * 4)),
        name="attn_ffn",
    )(q3, k3, k3, vt3, vt3, k3, vt3, x2, ohg, gates, bg, wsb, whg, wout, g2, w1, w2, gf)


def kernel(x, norm1_g, w_in, b_gate, lb_logits, hg_norm_g, w_o_sb, w_o_hg, w_out, norm2_g,
           w_ff1, w_ff2, final_g):
    b, s, d = x.shape
    assert w_in.shape[0] == 1, "single-layer block"
    sb_w = SB_HEADS * SB_HEAD_DIM
    hg_w = HG_HEADS * HG_HEAD_DIM
    x2 = x.reshape(b * s, d)
    later = (w_o_sb[0], w_o_hg[0], w_out[0], w_ff1[0], w_ff2[0])
    q, k, vt, gates, o_hg, wsb, whg, wout, w1, w2 = _proj_hgrn2(
        x2, norm1_g, w_in[0].astype(BF16), lb_logits, hg_norm_g, later, b, sb_w, hg_w)
    out = _attn_ffn(q.reshape(b, s, sb_w), k.reshape(b, s, sb_w), vt, x2, o_hg, gates, b_gate,
                    wsb, whg, wout, norm2_g, w1, w2, final_g.reshape(1, d))
    return out.reshape(b, s, d)
```

```python
import functools

import jax
import jax.numpy as jnp
import numpy as np
from jax import lax
from jax.experimental import pallas as pl
from jax.experimental.pallas import tpu as pltpu

F32 = jnp.float32
BF16 = jnp.bfloat16

SB_HEADS = 8
SB_HEAD_DIM = 64
HG_HEADS = 4
HG_HEAD_DIM = 128
EPS = 1e-6
LOG2E = 1.4426950408889634
SB_DEAD_CARRY = 151.0

V7X_LANES = 128
V7X_MXU_DIM = 256
V7X_VMEM_BYTES = 64 * 1024 * 1024

ATTN_TILE = V7X_MXU_DIM
HG_BLOCK = V7X_MXU_DIM
PROJ_BLOCKS = 2


def _vmem_limit(pipelined_bytes, resident_bytes, temp_bytes):
    need = 2 * pipelined_bytes + resident_bytes + temp_bytes
    return int(min(need + need // 4, V7X_VMEM_BYTES - 8 * 1024 * 1024))


def _resident(shape):
    return pl.BlockSpec(shape, lambda *_: (0,) * len(shape), pipeline_mode=pl.Buffered(1))


def _rms(x, g):
    ms = jnp.mean(x * x, axis=-1, keepdims=True)
    return x * lax.rsqrt(ms + EPS) * g


def _dot(a, b):
    return jnp.dot(a, b, preferred_element_type=F32)


def _dot_nt(a, b):
    return lax.dot_general(a, b, (((1,), (1,)), ((), ())), preferred_element_type=F32)


def _dot_tn(a, b):
    return lax.dot_general(a, b, (((0,), (0,)), ((), ())), preferred_element_type=F32)


def _softplus2(z, mask):
    sp = jnp.maximum(z, 0.0) + jnp.log(1.0 + jnp.exp2(-jnp.abs(z))) * LOG2E
    return sp if mask is None else jnp.where(mask, sp, 0.0)


def _split3(x):
    a = x.astype(BF16)
    r = x - a.astype(F32)
    b = r.astype(BF16)
    c = (r - b.astype(F32)).astype(BF16)
    return a, b, c


def _rows_from_group(b, group, r):
    n, c = b.shape
    if group == n:
        return jnp.broadcast_to(b[r:r + 1, :], (n, c))
    b3 = b.reshape(n // group, group, c)
    return jnp.broadcast_to(b3[:, r:r + 1, :], b3.shape).reshape(n, c)


def _midpoint_rows(b, group, pos):
    n = b.shape[0]
    half = group // 2
    if group >= 16:
        return _rows_from_group(b, group, half - 1)
    if group == 8:
        return _rows_from_group(b, 8, 3)
    up1 = pltpu.roll(b, n - 1, 0)
    dn1 = pltpu.roll(b, 1, 0)
    if group == 2:
        return jnp.where((pos & 1) == 0, b, dn1)
    assert group == 4
    dn2 = pltpu.roll(b, 2, 0)
    r4 = pos & 3
    return jnp.where(r4 == 0, up1, jnp.where(r4 == 1, b, jnp.where(r4 == 2, dn1, dn2)))


def _hgrn2_gates(fr, qr, lb):
    e = jnp.exp(-jnp.abs(fr))
    r = 1.0 / (1.0 + e)
    er = e * r
    sig = jnp.where(fr >= 0, r, er)
    nsig = jnp.where(fr >= 0, er, r)
    logf2 = jnp.log(lb + (1.0 - lb) * sig) * LOG2E
    k = (1.0 - lb) * nsig
    q = qr / (1.0 + jnp.exp(-qr))
    return logf2, k, q


def _hgrn2_head_products(b, q, k, v, lvl, pos, st_ref, h, keep):
    t, dk = b.shape
    n_levels = t.bit_length() - 1
    hb = t // 2
    zeros = jnp.zeros((hb, dk), BF16)
    lvl_d = jnp.concatenate([lvl[0:hb, 0:hb], lvl[hb:t, hb:t]], axis=1)
    diag = jnp.zeros((hb, t), F32)
    for level in range(1, n_levels):
        group = 1 << level
        d = b - _midpoint_rows(b, group, pos)
        later = (pos & (group - 1)) >= (group // 2)
        fac = jnp.exp2(-jnp.abs(d))
        ql = jnp.where(later, q * fac, 0.0).astype(BF16)
        kl = jnp.where(later, 0.0, k * fac).astype(BF16)
        lhs = jnp.concatenate([ql[0:hb], ql[hb:t]], axis=1)
        rhs = jnp.concatenate([jnp.concatenate([kl[0:hb], zeros], axis=1),
                               jnp.concatenate([zeros, kl[hb:t]], axis=1)], axis=0)
        diag = jnp.where(lvl_d == level, _dot_nt(lhs, rhs), diag)
    b_mid = b[hb - 1:hb, :]
    q_top = (q[hb:t] * jnp.exp2(b[hb:t] - b_mid)).astype(BF16)
    k_top = (k[0:hb] * jnp.exp2(b_mid - b[0:hb])).astype(BF16)
    top = _dot_nt(q_top, k_top)
    scores = jnp.concatenate(
        [jnp.concatenate([diag[:, 0:hb], jnp.zeros((hb, hb), F32)], axis=1),
         jnp.concatenate([top, diag[:, hb:t]], axis=1)], axis=0)

    st = st_ref[h] * keep
    b_last = b[t - 1:t, :]
    o = _dot_nt((q * jnp.exp2(b)).astype(BF16), st.astype(BF16))
    o = o + jnp.sum(q * k, axis=1, keepdims=True) * v.astype(F32)
    k_dec = (k * jnp.exp2(b_last - b)).astype(BF16)
    st_ref[h] = jnp.exp2(b_last) * st + _dot_tn(v, k_dec)
    return scores.astype(BF16), o


def _hgrn2_head_output(scores, o, v, gr, ng):
    return _rms(o + _dot(scores, v), ng) * (gr / (1.0 + jnp.exp(-gr)))


def _proj_hgrn2_kernel(*refs, tiles_per_seq, n_cast):
    x0_ref, xnext_ref, g_ref, w_ref, lbl_ref, ng_ref, lvl_ref = refs[:7]
    cast_in = refs[7:7 + n_cast]
    q_ref, k_ref, vt_ref, gate_ref, ohg_ref = refs[7 + n_cast:12 + n_cast]
    cast_out = refs[12 + n_cast:12 + 2 * n_cast]
    xn_buf, f_buf, iqg_buf, st_ref = refs[12 + 2 * n_cast:]
    d = xnext_ref.shape[1]
    t = HG_BLOCK
    n_blocks = xnext_ref.shape[0] // t
    dk = HG_HEAD_DIM
    hg_w = HG_HEADS * dk
    sb_w = q_ref.shape[1]
    i = pl.program_id(0)
    n_steps = pl.num_programs(0)
    slot = i % 2
    prev = 1 - slot

    def step(do_proj, do_hgrn2):
        row = lax.broadcasted_iota(jnp.int32, (t, t), 0)
        col = lax.broadcasted_iota(jnp.int32, (t, t), 1)
        tril = (row >= col).astype(BF16)
        pos = lax.broadcasted_iota(jnp.int32, (t, dk), 0)
        lvl = lvl_ref[...]
        first_keep = jnp.where(i % tiles_per_seq == 1, 0.0, 1.0)

        lbl = lbl_ref[...]
        ex = jnp.exp(lbl - jnp.max(lbl, axis=0, keepdims=True))
        lb_all = ex[0:1, :] / jnp.sum(ex, axis=0, keepdims=True)

        def prepare_block(blk):
            rows = slice(blk * t, (blk + 1) * t)
            logf2, k_all, q_all = _hgrn2_gates(
                f_buf[prev, rows, :], iqg_buf[prev, rows, hg_w:2 * hg_w].astype(F32), lb_all)
            g1, g2, g3 = _split3(logf2)
            return _dot(tril, g1) + _dot(tril, g2) + _dot(tril, g3), q_all, k_all

        def head_products(blk, h, prepared):
            b_all, q_all, k_all = prepared
            rows = slice(blk * t, (blk + 1) * t)
            sl = slice(h * dk, (h + 1) * dk)
            return _hgrn2_head_products(b_all[:, sl], q_all[:, sl], k_all[:, sl],
                                        iqg_buf[prev, rows, sl], lvl, pos, st_ref, h,
                                        first_keep if blk == 0 else 1.0)

        def head_output(blk, h, products):
            rows = slice(blk * t, (blk + 1) * t)
            sl = slice(h * dk, (h + 1) * dk)
            gr = iqg_buf[prev, rows, 2 * hg_w + h * dk:2 * hg_w + (h + 1) * dk].astype(F32)
            y = _hgrn2_head_output(*products, iqg_buf[prev, rows, sl], gr, ng_ref[:, sl])
            ohg_ref[rows, sl] = y.astype(ohg_ref.dtype)

        c0 = 3 * sb_w
        c1 = c0 + hg_w
        c2 = c1 + 3 * hg_w
        slab = d // 2
        slabs = []
        if do_proj:
            xn = xn_buf[slot]

            def project(lo, hi):
                return _dot(xn, w_ref[:, lo:hi])

            def q_slab():
                q_ref[...] = (project(0, sb_w) * (SB_HEAD_DIM ** -0.5 * LOG2E)).astype(q_ref.dtype)

            def k_slab():
                k_ref[...] = project(sb_w, 2 * sb_w).astype(k_ref.dtype)

            def v_slab():
                v = project(2 * sb_w, 3 * sb_w)
                for blk in range(n_blocks):
                    vt_ref[0, :, blk * t:(blk + 1) * t] = \
                        v[blk * t:(blk + 1) * t].T.astype(vt_ref.dtype)

            def f_slab():
                f_buf[slot] = project(c0, c1)

            def iqg_slab(lo):
                def run():
                    iqg_buf[slot, :, lo:lo + slab] = \
                        project(c1 + lo, c1 + lo + slab).astype(iqg_buf.dtype)
                return run

            def gate_slab(lo):
                def run():
                    gate_ref[:, lo:lo + slab] = project(c2 + lo, c2 + lo + slab)
                return run

            for src, dst in zip(cast_in, cast_out):
                dst[...] = src[...].astype(dst.dtype)
            slabs = [v_slab, f_slab] + [iqg_slab(lo) for lo in range(0, 3 * hg_w, slab)] \
                + [gate_slab(lo) for lo in range(0, 2 * d, slab)]
            q_slab()
            k_slab()
        if do_hgrn2:
            prepared = prepare_block(0)
            if slabs:
                slabs.pop(0)()
            for blk in range(n_blocks):
                for h in range(HG_HEADS):
                    if blk > 0 and h == 0:
                        prepared = prepare_block(blk)
                    products = head_products(blk, h, prepared)
                    if slabs:
                        slabs.pop(0)()
                    head_output(blk, h, products)
        for run in slabs:
            run()
        if do_proj:
            xn_buf[prev] = _rms(xnext_ref[...], g_ref[...]).astype(BF16)

    @pl.when(i == 0)
    def _():
        st_ref[...] = jnp.zeros_like(st_ref)
        xn_buf[0] = _rms(x0_ref[...], g_ref[...]).astype(BF16)
        step(True, False)

    @pl.when(jnp.logical_and(i > 0, i < n_steps - 1))
    def _():
        step(True, True)

    @pl.when(i == n_steps - 1)
    def _():
        step(False, True)


def _pair_levels(t):
    idx = np.arange(t)
    x = idx[:, None] ^ idx[None, :]
    lev = np.where(x > 0, np.floor(np.log2(np.maximum(x, 1))).astype(np.int64) + 1, 0)
    return np.where(idx[:, None] > idx[None, :], lev, 0).astype(np.int32)


def _proj_hgrn2(x2, g1, w_in, lb_logits, ng, later_weights, batch, sb_w, hg_w):
    n, d = x2.shape
    t = PROJ_BLOCKS * HG_BLOCK
    cols = w_in.shape[1]
    n_tiles = n // t
    tiles_per_seq = n_tiles // batch
    assert hg_w == HG_HEADS * HG_HEAD_DIM and n_tiles * t == n and tiles_per_seq * batch == n_tiles
    lvl = jnp.asarray(_pair_levels(HG_BLOCK))
    cur = lambda i: jnp.minimum(i, n_tiles - 1)
    rows = lambda w: pl.BlockSpec((t, w), lambda i: (cur(i), 0))
    vt_spec = pl.BlockSpec((1, sb_w, t),
                           lambda i: (cur(i) // tiles_per_seq, 0, cur(i) % tiles_per_seq))
    sds = lambda w, dt: jax.ShapeDtypeStruct((n, w), dt)
    cast_specs = [pl.BlockSpec((w.shape[0] // n_tiles, w.shape[1]), lambda i: (cur(i), 0))
                  for w in later_weights]
    assert all(w.shape[0] % (n_tiles * 16) == 0 for w in later_weights)
    cast_bytes = sum(w.size // n_tiles * 6 for w in later_weights)
    moving = t * (d * 4 + 2 * sb_w * 2 + sb_w * 2 + 2 * d * 4 + hg_w * 2) + cast_bytes
    resident = d * 4 + d * cols * 2 + 3 * hg_w * 4 + HG_BLOCK * HG_BLOCK * 4 + t * d * 4 \
        + 2 * t * d * 2 + 2 * t * hg_w * (4 + 3 * 2) + HG_HEADS * HG_HEAD_DIM * HG_HEAD_DIM * 4
    return pl.pallas_call(
        functools.partial(_proj_hgrn2_kernel, tiles_per_seq=tiles_per_seq,
                          n_cast=len(later_weights)),
        grid=(n_tiles + 1,),
        in_specs=[_resident((t, d)), pl.BlockSpec((t, d), lambda i: (cur(i + 1), 0)),
                  _resident((1, d)), _resident((d, cols)),
                  _resident(lb_logits.shape), _resident(ng.shape),
                  _resident((HG_BLOCK, HG_BLOCK))] + cast_specs,
        out_specs=[rows(sb_w), rows(sb_w), vt_spec, rows(2 * d),
                   pl.BlockSpec((t, hg_w), lambda i: (jnp.maximum(i - 1, 0), 0))] + cast_specs,
        out_shape=[sds(sb_w, BF16), sds(sb_w, BF16),
                   jax.ShapeDtypeStruct((batch, sb_w, n // batch), BF16),
                   sds(2 * d, F32), sds(hg_w, BF16)]
        + [jax.ShapeDtypeStruct(w.shape, BF16) for w in later_weights],
        scratch_shapes=[pltpu.VMEM((2, t, d), BF16),
                        pltpu.VMEM((2, t, hg_w), F32),
                        pltpu.VMEM((2, t, 3 * hg_w), BF16),
                        pltpu.VMEM((HG_HEADS, HG_HEAD_DIM, HG_HEAD_DIM), F32)],
        compiler_params=pltpu.CompilerParams(
            dimension_semantics=("arbitrary",),
            vmem_limit_bytes=_vmem_limit(moving, resident,
                                         t * (d * 6 + cols * 4) + 64 * HG_BLOCK * HG_BLOCK * 4)),
        name="proj_hgrn2",
    )(x2, x2, g1, w_in, lb_logits, ng, lvl, *later_weights)


def _attn_ffn_kernel(q_ref, kd_ref, kp_ref, vtd_ref, vtp_ref, k_hbm, vt_hbm, x_ref, ohg_ref,
                     gate_ref, bg_ref, g2_ref, gf_ref, wsb_hbm, whg_hbm, wout_hbm, w1_hbm, w2_hbm,
                     o_ref, osb_buf, qm_ref, acc_ref, carry_ref, k_buf, vt_buf, sems,
                     wsb_ref, whg_ref, wout_ref, w1_ref, w2_ref, w_sems,
                     *, tiles_per_seq, n_tiles):
    t = q_ref.shape[1]
    d = x_ref.shape[1]
    heads = range(SB_HEADS)
    i = pl.program_id(0)
    slot = i % 2
    qi = jnp.minimum(i, n_tiles - 1) % tiles_per_seq

    row = lax.broadcasted_iota(jnp.int32, (t, t), 0)
    col = lax.broadcasted_iota(jnp.int32, (t, t), 1)
    tri = (col > row).astype(BF16)
    causal = row < col

    def logits(k):
        return [_dot_nt(k[:, (h // 2) * V7X_LANES:(h // 2 + 1) * V7X_LANES], qm_ref[h])
                for h in heads]

    def softplus_phase(z, mask):
        sp = [_softplus2(z[h], mask) for h in heads]
        return sp, [sp[h].astype(BF16) for h in heads]

    def cumsum_phase(spb):
        return [_dot(tri, spb[h]) for h in heads]

    def weight_phase(z, sp, later, mask):
        w = [jnp.exp2(z[h] - sp[h] - later[h]) for h in heads]
        if mask is not None:
            w = [jnp.where(mask, w[h], 0.0) for h in heads]
        return [w[h].astype(BF16) for h in heads]

    def value_phase(vt, w, later, spb):
        pv = [_dot(vt[h * SB_HEAD_DIM:(h + 1) * SB_HEAD_DIM, :], w[h]) for h in heads]
        return pv, [later[h][0:1, :] + spb[h][0:1, :].astype(F32) for h in heads]

    def main_block(do_attn, do_ffn):
        has_prev = qi > 0
        if do_attn:
            lane = lax.broadcasted_iota(jnp.int32, (t, V7X_LANES), 1)
            zero = jnp.zeros((), BF16)
            q = q_ref[0]
            for h in heads:
                grp = q[:, (h // 2) * V7X_LANES:(h // 2 + 1) * V7X_LANES]
                qm_ref[h] = jnp.where((lane // SB_HEAD_DIM) == (h % 2), grp, zero)
        if do_ffn:
            a_sb = _dot_tn(osb_buf[1 - slot], wsb_ref[...])
            a_hg = _dot(ohg_ref[...], whg_ref[...])
        if do_attn:
            z0 = logits(kd_ref[0])
        if do_ffn:
            gates = 1.0 / (1.0 + jnp.exp(-(gate_ref[...] + bg_ref[...])))
            merged = (gates[:, :d] * a_sb + gates[:, d:] * a_hg).astype(BF16)
            hres = x_ref[...] + _dot(merged, wout_ref[...])
        if do_attn:
            z1 = logits(kp_ref[0])
        if do_ffn:
            hn = _rms(hres, g2_ref[...]).astype(BF16)
            half = w1_ref.shape[1] // 2

            def mlp_up(lo):
                act = jnp.maximum(_dot(hn, w1_ref[:, lo:lo + half]), 0.0)
                return (act * act).astype(BF16)

            act_a = mlp_up(0)
        if do_attn:
            sp0, spb0 = softplus_phase(z0, causal)
            lat0 = cumsum_phase(spb0)
        if do_ffn:
            act_b = mlp_up(half)
        if do_attn:
            sp1, spb1 = softplus_phase(z1, None)
            lat1 = cumsum_phase(spb1)
        if do_ffn:
            hres = hres + _dot(act_a, w2_ref[0:half, :])
        if do_attn:
            w0 = weight_phase(z0, sp0, lat0, causal)
            w1 = weight_phase(z1, sp1, lat1, None)
        if do_ffn:
            hres = hres + _dot(act_b, w2_ref[half:2 * half, :])
        if do_attn:
            pv0, tot0 = value_phase(vtd_ref.at[0], w0, lat0, spb0)
            pv1, tot1 = value_phase(vtp_ref.at[0], w1, lat1, spb1)
        if do_ffn:
            o_ref[...] = _rms(hres, gf_ref[...]).astype(o_ref.dtype)
        if do_attn:
            for h in heads:
                scale = jnp.where(has_prev, jnp.exp2(-tot0[h]), 0.0)
                acc_ref[h] = pv0[h] + scale * pv1[h]
                carry_ref[h:h + 1, :] = tot0[h] + jnp.where(has_prev, tot1[h], 0.0)

    def sweep_rest():
        def more(state):
            n, live = state
            return jnp.logical_and(n < qi, live)

        seq = jnp.minimum(i, n_tiles - 1) // tiles_per_seq

        def tile_copies(j):
            start = pl.multiple_of(j * t, t)
            return (pltpu.make_async_copy(k_hbm.at[seq, pl.ds(start, t), :], k_buf, sems.at[0]),
                    pltpu.make_async_copy(vt_hbm.at[seq, :, pl.ds(start, t)], vt_buf, sems.at[1]))

        def body(state):
            n, _ = state
            copies = tile_copies(qi - 1 - n)
            for c in copies:
                c.start()
            for c in copies:
                c.wait()
            z = logits(k_buf[...])
            sp, spb = softplus_phase(z, None)
            later = cumsum_phase(spb)
            pv, tot = value_phase(vt_buf, weight_phase(z, sp, later, None), later, spb)
            for h in heads:
                c = carry_ref[h:h + 1, :]
                acc_ref[h] += jnp.exp2(-c) * pv[h]
                carry_ref[h:h + 1, :] = c + tot[h]
            return n + 1, jnp.min(carry_ref[...]) < SB_DEAD_CARRY

        lax.while_loop(more, body, (jnp.int32(1), jnp.min(carry_ref[...]) < SB_DEAD_CARRY))
        osb_buf[slot] = acc_ref[...].reshape(SB_HEADS * SB_HEAD_DIM, t).astype(osb_buf.dtype)

    weight_copies = [pltpu.make_async_copy(src, dst, w_sems.at[n]) for n, (src, dst) in enumerate(
        ((wsb_hbm, wsb_ref), (whg_hbm, whg_ref), (wout_hbm, wout_ref), (w1_hbm, w1_ref),
         (w2_hbm, w2_ref)))]

    @pl.when(i == 0)
    def _():
        for c in weight_copies:
            c.start()
        main_block(True, False)
        sweep_rest()

    @pl.when(i == 1)
    def _():
        for c in weight_copies:
            c.wait()

    @pl.when(jnp.logical_and(i > 0, i < n_tiles))
    def _():
        main_block(True, True)
        sweep_rest()

    @pl.when(i == n_tiles)
    def _():
        main_block(False, True)


def _attn_ffn(q3, k3, vt3, x2, ohg, gates, bg, wsb, whg, wout, g2, w1, w2, gf):
    b, s, sb_w = q3.shape
    n, d = x2.shape
    t = ATTN_TILE
    dff = w1.shape[1]
    tiles_per_seq = s // t
    n_tiles = n // t
    assert sb_w == SB_HEADS * SB_HEAD_DIM and tiles_per_seq * t == s and n_tiles == b * tiles_per_seq
    cur = lambda i: jnp.minimum(i, n_tiles - 1)
    prv = lambda i: (jnp.maximum(i - 1, 0), 0)
    seq = lambda i: cur(i) // tiles_per_seq
    tile = lambda i: cur(i) % tiles_per_seq
    before = lambda i: jnp.maximum(tile(i) - 1, 0)
    k_tile = lambda pos: pl.BlockSpec((1, t, sb_w), lambda i: (seq(i), pos(i), 0))
    vt_tile = lambda pos: pl.BlockSpec((1, sb_w, t), lambda i: (seq(i), 0, pos(i)))
    in_hbm = pl.BlockSpec(memory_space=pl.ANY)
    full = lambda a: _resident(a.shape)
    weights = (wsb, whg, wout, w1, w2)
    resident_bytes = sum(a.size * a.dtype.itemsize for a in (bg, g2, gf) + weights) \
        + 2 * t * sb_w * 2 + 2 * t * sb_w * 2 + SB_HEADS * t * (V7X_LANES * 2 + SB_HEAD_DIM * 4 + 4)
    moving = t * (5 * sb_w * 2 + d * 4 + ohg.shape[1] * 2 + 2 * d * 4 + d * 4)
    return pl.pallas_call(
        functools.partial(_attn_ffn_kernel, tiles_per_seq=tiles_per_seq, n_tiles=n_tiles),
        grid=(n_tiles + 1,),
        in_specs=[k_tile(tile), k_tile(tile), k_tile(before), vt_tile(tile), vt_tile(before),
                  in_hbm, in_hbm,
                  pl.BlockSpec((t, d), prv),
                  pl.BlockSpec((t, ohg.shape[1]), prv),
                  pl.BlockSpec((t, 2 * d), prv),
                  full(bg), full(g2), full(gf)] + [in_hbm] * len(weights),
        out_specs=pl.BlockSpec((t, d), prv),
        out_shape=jax.ShapeDtypeStruct((n, d), x2.dtype),
        scratch_shapes=[pltpu.VMEM((2, sb_w, t), BF16),
                        pltpu.VMEM((SB_HEADS, t, V7X_LANES), BF16),
                        pltpu.VMEM((SB_HEADS, SB_HEAD_DIM, t), F32),
                        pltpu.VMEM((SB_HEADS, t), F32),
                        pltpu.VMEM((t, sb_w), BF16),
                        pltpu.VMEM((sb_w, t), BF16),
                        pltpu.SemaphoreType.DMA((2,))]
        + [pltpu.VMEM(a.shape, a.dtype) for a in weights]
        + [pltpu.SemaphoreType.DMA((len(weights),))],
        compiler_params=pltpu.CompilerParams(
            dimension_semantics=("arbitrary",),
            vmem_limit_bytes=_vmem_limit(moving, resident_bytes,
                                         t * (dff * 6 + d * 24) + 8 * SB_HEADS * t * t * 4)),
        name="attn_ffn",
    )(q3, k3, k3, vt3, vt3, k3, vt3, x2, ohg, gates, bg, g2, gf, *weights)


def kernel(x, norm1_g, w_in, b_gate, lb_logits, hg_norm_g, w_o_sb, w_o_hg, w_out, norm2_g,
           w_ff1, w_ff2, final_g):
    b, s, d = x.shape
    assert w_in.shape[0] == 1, "single-layer block"
    sb_w = SB_HEADS * SB_HEAD_DIM
    hg_w = HG_HEADS * HG_HEAD_DIM
    x2 = x.reshape(b * s, d)
    later = (w_o_sb[0], w_o_hg[0], w_out[0], w_ff1[0], w_ff2[0])
    q, k, vt, gates, o_hg, wsb, whg, wout, w1, w2 = _proj_hgrn2(
        x2, norm1_g, w_in[0].astype(BF16), lb_logits, hg_norm_g, later, b, sb_w, hg_w)
    out = _attn_ffn(q.reshape(b, s, sb_w), k.reshape(b, s, sb_w), vt, x2, o_hg, gates, b_gate,
                    wsb, whg, wout, norm2_g, w1, w2, final_g.reshape(1, d))
    return out.reshape(b, s, d)
```

```python
import functools

import jax
import jax.numpy as jnp
import numpy as np
from jax import lax
from jax.experimental import pallas as pl
from jax.experimental.pallas import tpu as pltpu

F32 = jnp.float32
BF16 = jnp.bfloat16

SB_HEADS = 8
SB_HEAD_DIM = 64
HG_HEADS = 4
HG_HEAD_DIM = 128
EPS = 1e-6
LOG2E = 1.4426950408889634
SB_DEAD_CARRY = 151.0

V7X_LANES = 128
V7X_MXU_DIM = 256
V7X_VMEM_BYTES = 64 * 1024 * 1024

ATTN_TILE = V7X_MXU_DIM
HG_BLOCK = V7X_MXU_DIM
PROJ_BLOCKS = 2


def _vmem_limit(pipelined_bytes, resident_bytes, temp_bytes):
    need = 2 * pipelined_bytes + resident_bytes + temp_bytes
    return int(min(need + need // 4, V7X_VMEM_BYTES - 8 * 1024 * 1024))


def _resident(shape):
    return pl.BlockSpec(shape, lambda *_: (0,) * len(shape), pipeline_mode=pl.Buffered(1))


def _rms(x, g):
    ms = jnp.mean(x * x, axis=-1, keepdims=True)
    return x * lax.rsqrt(ms + EPS) * g


def _dot(a, b):
    return jnp.dot(a, b, preferred_element_type=F32)


def _dot_nt(a, b):
    return lax.dot_general(a, b, (((1,), (1,)), ((), ())), preferred_element_type=F32)


def _dot_tn(a, b):
    return lax.dot_general(a, b, (((0,), (0,)), ((), ())), preferred_element_type=F32)


def _softplus2(z, mask):
    sp = jnp.maximum(z, 0.0) + jnp.log(1.0 + jnp.exp2(-jnp.abs(z))) * LOG2E
    return sp if mask is None else jnp.where(mask, sp, 0.0)


def _split3(x):
    a = x.astype(BF16)
    r = x - a.astype(F32)
    b = r.astype(BF16)
    c = (r - b.astype(F32)).astype(BF16)
    return a, b, c


def _rows_from_group(b, group, r):
    n, c = b.shape
    if group == n:
        return jnp.broadcast_to(b[r:r + 1, :], (n, c))
    b3 = b.reshape(n // group, group, c)
    return jnp.broadcast_to(b3[:, r:r + 1, :], b3.shape).reshape(n, c)


def _midpoint_rows(b, group, pos):
    n = b.shape[0]
    half = group // 2
    if group >= 16:
        return _rows_from_group(b, group, half - 1)
    if group == 8:
        return _rows_from_group(b, 8, 3)
    up1 = pltpu.roll(b, n - 1, 0)
    dn1 = pltpu.roll(b, 1, 0)
    if group == 2:
        return jnp.where((pos & 1) == 0, b, dn1)
    assert group == 4
    dn2 = pltpu.roll(b, 2, 0)
    r4 = pos & 3
    return jnp.where(r4 == 0, up1, jnp.where(r4 == 1, b, jnp.where(r4 == 2, dn1, dn2)))


def _hgrn2_gates(fr, qr, lb):
    e = jnp.exp(-jnp.abs(fr))
    r = 1.0 / (1.0 + e)
    er = e * r
    sig = jnp.where(fr >= 0, r, er)
    nsig = jnp.where(fr >= 0, er, r)
    logf2 = jnp.log(lb + (1.0 - lb) * sig) * LOG2E
    k = (1.0 - lb) * nsig
    q = qr / (1.0 + jnp.exp(-qr))
    return logf2, k, q


def _hgrn2_head_products(b, q, k, v, lvl, pos, st_ref, h, keep):
    t, dk = b.shape
    n_levels = t.bit_length() - 1
    hb = t // 2
    zeros = jnp.zeros((hb, dk), BF16)
    lvl_d = jnp.concatenate([lvl[0:hb, 0:hb], lvl[hb:t, hb:t]], axis=1)
    diag = jnp.zeros((hb, t), F32)
    for level in range(1, n_levels):
        group = 1 << level
        d = b - _midpoint_rows(b, group, pos)
        later = (pos & (group - 1)) >= (group // 2)
        fac = jnp.exp2(-jnp.abs(d))
        ql = jnp.where(later, q * fac, 0.0).astype(BF16)
        kl = jnp.where(later, 0.0, k * fac).astype(BF16)
        lhs = jnp.concatenate([ql[0:hb], ql[hb:t]], axis=1)
        rhs = jnp.concatenate([jnp.concatenate([kl[0:hb], zeros], axis=1),
                               jnp.concatenate([zeros, kl[hb:t]], axis=1)], axis=0)
        diag = jnp.where(lvl_d == level, _dot_nt(lhs, rhs), diag)
    b_mid = b[hb - 1:hb, :]
    q_top = (q[hb:t] * jnp.exp2(b[hb:t] - b_mid)).astype(BF16)
    k_top = (k[0:hb] * jnp.exp2(b_mid - b[0:hb])).astype(BF16)
    top = _dot_nt(q_top, k_top)
    scores = jnp.concatenate(
        [jnp.concatenate([diag[:, 0:hb], jnp.zeros((hb, hb), F32)], axis=1),
         jnp.concatenate([top, diag[:, hb:t]], axis=1)], axis=0)

    st = st_ref[h] * keep
    b_last = b[t - 1:t, :]
    o = _dot_nt((q * jnp.exp2(b)).astype(BF16), st.astype(BF16))
    o = o + jnp.sum(q * k, axis=1, keepdims=True) * v.astype(F32)
    k_dec = (k * jnp.exp2(b_last - b)).astype(BF16)
    st_ref[h] = jnp.exp2(b_last) * st + _dot_tn(v, k_dec)
    return scores.astype(BF16), o


def _hgrn2_head_output(scores, o, v, gr, ng):
    return _rms(o + _dot(scores, v), ng) * (gr / (1.0 + jnp.exp(-gr)))


def _proj_hgrn2_kernel(*refs, tiles_per_seq, n_cast):
    x0_ref, xnext_ref, g_ref, w_ref, lbl_ref, ng_ref, lvl_ref = refs[:7]
    cast_in = refs[7:7 + n_cast]
    q_ref, k_ref, vt_ref, gate_ref, ohg_ref = refs[7 + n_cast:12 + n_cast]
    cast_out = refs[12 + n_cast:12 + 2 * n_cast]
    xn_buf, f_buf, iqg_buf, st_ref = refs[12 + 2 * n_cast:]
    d = xnext_ref.shape[1]
    t = HG_BLOCK
    n_blocks = xnext_ref.shape[0] // t
    dk = HG_HEAD_DIM
    hg_w = HG_HEADS * dk
    sb_w = q_ref.shape[1]
    i = pl.program_id(0)
    n_steps = pl.num_programs(0)
    slot = i % 2
    prev = 1 - slot

    def step(do_proj, do_hgrn2):
        row = lax.broadcasted_iota(jnp.int32, (t, t), 0)
        col = lax.broadcasted_iota(jnp.int32, (t, t), 1)
        tril = (row >= col).astype(BF16)
        pos = lax.broadcasted_iota(jnp.int32, (t, dk), 0)
        lvl = lvl_ref[...]
        first_keep = jnp.where(i % tiles_per_seq == 1, 0.0, 1.0)

        lbl = lbl_ref[...]
        ex = jnp.exp(lbl - jnp.max(lbl, axis=0, keepdims=True))
        lb_all = ex[0:1, :] / jnp.sum(ex, axis=0, keepdims=True)

        def prepare_block(blk):
            rows = slice(blk * t, (blk + 1) * t)
            logf2, k_all, q_all = _hgrn2_gates(
                f_buf[prev, rows, :], iqg_buf[prev, rows, hg_w:2 * hg_w].astype(F32), lb_all)
            g1, g2, g3 = _split3(logf2)
            return _dot(tril, g1) + _dot(tril, g2) + _dot(tril, g3), q_all, k_all

        def head_products(blk, h, prepared):
            b_all, q_all, k_all = prepared
            rows = slice(blk * t, (blk + 1) * t)
            sl = slice(h * dk, (h + 1) * dk)
            return _hgrn2_head_products(b_all[:, sl], q_all[:, sl], k_all[:, sl],
                                        iqg_buf[prev, rows, sl], lvl, pos, st_ref, h,
                                        first_keep if blk == 0 else 1.0)

        def head_output(blk, h, products):
            rows = slice(blk * t, (blk + 1) * t)
            sl = slice(h * dk, (h + 1) * dk)
            gr = iqg_buf[prev, rows, 2 * hg_w + h * dk:2 * hg_w + (h + 1) * dk].astype(F32)
            y = _hgrn2_head_output(*products, iqg_buf[prev, rows, sl], gr, ng_ref[:, sl])
            ohg_ref[rows, sl] = y.astype(ohg_ref.dtype)

        c0 = 3 * sb_w
        c1 = c0 + hg_w
        c2 = c1 + 3 * hg_w
        slab = d // 2
        slabs = []
        if do_proj:
            xn = xn_buf[slot]

            def project(lo, hi):
                return _dot(xn, w_ref[:, lo:hi])

            def q_slab():
                q_ref[...] = (project(0, sb_w) * (SB_HEAD_DIM ** -0.5 * LOG2E)).astype(q_ref.dtype)

            def k_slab():
                k_ref[...] = project(sb_w, 2 * sb_w).astype(k_ref.dtype)

            def v_slab():
                v = project(2 * sb_w, 3 * sb_w)
                for blk in range(n_blocks):
                    vt_ref[0, :, blk * t:(blk + 1) * t] = \
                        v[blk * t:(blk + 1) * t].T.astype(vt_ref.dtype)

            def f_slab():
                f_buf[slot] = project(c0, c1)

            def iqg_slab(lo):
                def run():
                    iqg_buf[slot, :, lo:lo + slab] = \
                        project(c1 + lo, c1 + lo + slab).astype(iqg_buf.dtype)
                return run

            def gate_slab(lo):
                def run():
                    gate_ref[:, lo:lo + slab] = project(c2 + lo, c2 + lo + slab)
                return run

            for src, dst in zip(cast_in, cast_out):
                dst[...] = src[...].astype(dst.dtype)
            slabs = [v_slab, f_slab] + [iqg_slab(lo) for lo in range(0, 3 * hg_w, slab)] \
                + [gate_slab(lo) for lo in range(0, 2 * d, slab)]
            q_slab()
            k_slab()
        if do_hgrn2:
            prepared = prepare_block(0)
            if slabs:
                slabs.pop(0)()
            for blk in range(n_blocks):
                for h in range(HG_HEADS):
                    if blk > 0 and h == 0:
                        prepared = prepare_block(blk)
                    products = head_products(blk, h, prepared)
                    if slabs:
                        slabs.pop(0)()
                    head_output(blk, h, products)
        for run in slabs:
            run()
        if do_proj:
            xn_buf[prev] = _rms(xnext_ref[...], g_ref[...]).astype(BF16)

    @pl.when(i == 0)
    def _():
        st_ref[...] = jnp.zeros_like(st_ref)
        xn_buf[0] = _rms(x0_ref[...], g_ref[...]).astype(BF16)
        step(True, False)

    @pl.when(jnp.logical_and(i > 0, i < n_steps - 1))
    def _():
        step(True, True)

    @pl.when(i == n_steps - 1)
    def _():
        step(False, True)


def _pair_levels(t):
    idx = np.arange(t)
    x = idx[:, None] ^ idx[None, :]
    lev = np.where(x > 0, np.floor(np.log2(np.maximum(x, 1))).astype(np.int64) + 1, 0)
    return np.where(idx[:, None] > idx[None, :], lev, 0).astype(np.int32)


def _proj_hgrn2(x2, g1, w_in, lb_logits, ng, later_weights, batch, sb_w, hg_w):
    n, d = x2.shape
    t = PROJ_BLOCKS * HG_BLOCK
    cols = w_in.shape[1]
    n_tiles = n // t
    tiles_per_seq = n_tiles // batch
    assert hg_w == HG_HEADS * HG_HEAD_DIM and n_tiles * t == n and tiles_per_seq * batch == n_tiles
    lvl = jnp.asarray(_pair_levels(HG_BLOCK))
    cur = lambda i: jnp.minimum(i, n_tiles - 1)
    rows = lambda w: pl.BlockSpec((t, w), lambda i: (cur(i), 0))
    vt_spec = pl.BlockSpec((1, sb_w, t),
                           lambda i: (cur(i) // tiles_per_seq, 0, cur(i) % tiles_per_seq))
    sds = lambda w, dt: jax.ShapeDtypeStruct((n, w), dt)
    cast_specs = [pl.BlockSpec((w.shape[0] // n_tiles, w.shape[1]), lambda i: (cur(i), 0))
                  for w in later_weights]
    assert all(w.shape[0] % (n_tiles * 16) == 0 for w in later_weights)
    cast_bytes = sum(w.size // n_tiles * 6 for w in later_weights)
    moving = t * (d * 4 + 2 * sb_w * 2 + sb_w * 2 + 2 * d * 4 + hg_w * 2) + cast_bytes
    resident = d * 4 + d * cols * 2 + 3 * hg_w * 4 + HG_BLOCK * HG_BLOCK * 4 + t * d * 4 \
        + 2 * t * d * 2 + 2 * t * hg_w * (4 + 3 * 2) + HG_HEADS * HG_HEAD_DIM * HG_HEAD_DIM * 4
    return pl.pallas_call(
        functools.partial(_proj_hgrn2_kernel, tiles_per_seq=tiles_per_seq,
                          n_cast=len(later_weights)),
        grid=(n_tiles + 1,),
        in_specs=[_resident((t, d)), pl.BlockSpec((t, d), lambda i: (cur(i + 1), 0)),
                  _resident((1, d)), _resident((d, cols)),
                  _resident(lb_logits.shape), _resident(ng.shape),
                  _resident((HG_BLOCK, HG_BLOCK))] + cast_specs,
        out_specs=[rows(sb_w), rows(sb_w), vt_spec, rows(2 * d),
                   pl.BlockSpec((t, hg_w), lambda i: (jnp.maximum(i - 1, 0), 0))] + cast_specs,
        out_shape=[sds(sb_w, BF16), sds(sb_w, BF16),
                   jax.ShapeDtypeStruct((batch, sb_w, n // batch), BF16),
                   sds(2 * d, F32), sds(hg_w, BF16)]
        + [jax.ShapeDtypeStruct(w.shape, BF16) for w in later_weights],
        scratch_shapes=[pltpu.VMEM((2, t, d), BF16),
                        pltpu.VMEM((2, t, hg_w), F32),
                        pltpu.VMEM((2, t, 3 * hg_w), BF16),
                        pltpu.VMEM((HG_HEADS, HG_HEAD_DIM, HG_HEAD_DIM), F32)],
        compiler_params=pltpu.CompilerParams(
            dimension_semantics=("arbitrary",),
            vmem_limit_bytes=_vmem_limit(moving, resident,
                                         t * (d * 6 + cols * 4) + 64 * HG_BLOCK * HG_BLOCK * 4)),
        name="proj_hgrn2",
    )(x2, x2, g1, w_in, lb_logits, ng, lvl, *later_weights)


def _attn_ffn_kernel(q_ref, kd_ref, kp_ref, vtd_ref, vtp_ref, k_hbm, vt_hbm, x_ref, ohg_ref,
                     gate_ref, bg_ref, g2_ref, gf_ref, wsb_hbm, whg_hbm, wout_hbm, w1_hbm, w2_hbm,
                     o_ref, osb_buf, qm_ref, acc_ref, carry_ref, k_buf, vt_buf, sems,
                     wsb_ref, whg_ref, wout_ref, w1_ref, w2_ref, w_sems,
                     *, tiles_per_seq, n_tiles):
    t = q_ref.shape[1]
    d = x_ref.shape[1]
    heads = range(SB_HEADS)
    i = pl.program_id(0)
    slot = i % 2
    qi = jnp.minimum(i, n_tiles - 1) % tiles_per_seq

    row = lax.broadcasted_iota(jnp.int32, (t, t), 0)
    col = lax.broadcasted_iota(jnp.int32, (t, t), 1)
    tri = (col > row).astype(BF16)
    causal = row < col

    def logits(k):
        return [_dot_nt(k[:, (h // 2) * V7X_LANES:(h // 2 + 1) * V7X_LANES], qm_ref[h])
                for h in heads]

    def softplus_phase(z, mask):
        sp = [_softplus2(z[h], mask) for h in heads]
        return sp, [sp[h].astype(BF16) for h in heads]

    def cumsum_phase(spb):
        return [_dot(tri, spb[h]) for h in heads]

    def weight_phase(z, sp, later, mask):
        w = [jnp.exp2(z[h] - sp[h] - later[h]) for h in heads]
        if mask is not None:
            w = [jnp.where(mask, w[h], 0.0) for h in heads]
        return [w[h].astype(BF16) for h in heads]

    def value_phase(vt, w, later, spb):
        pv = [_dot(vt[h * SB_HEAD_DIM:(h + 1) * SB_HEAD_DIM, :], w[h]) for h in heads]
        return pv, [later[h][0:1, :] + spb[h][0:1, :].astype(F32) for h in heads]

    def main_block(do_attn, do_ffn):
        has_prev = qi > 0
        if do_attn:
            lane = lax.broadcasted_iota(jnp.int32, (t, V7X_LANES), 1)
            zero = jnp.zeros((), BF16)
            q = q_ref[0]
            for h in heads:
                grp = q[:, (h // 2) * V7X_LANES:(h // 2 + 1) * V7X_LANES]
                qm_ref[h] = jnp.where((lane // SB_HEAD_DIM) == (h % 2), grp, zero)
        if do_ffn:
            a_sb = _dot_tn(osb_buf[1 - slot], wsb_ref[...])
            a_hg = _dot(ohg_ref[...], whg_ref[...])
        if do_attn:
            z0 = logits(kd_ref[0])
        if do_ffn:
            gates = 1.0 / (1.0 + jnp.exp(-(gate_ref[...] + bg_ref[...])))
            merged = (gates[:, :d] * a_sb + gates[:, d:] * a_hg).astype(BF16)
            hres = x_ref[...] + _dot(merged, wout_ref[...])
        if do_attn:
            z1 = logits(kp_ref[0])
        if do_ffn:
            hn = _rms(hres, g2_ref[...]).astype(BF16)
            half = w1_ref.shape[1] // 2

            def mlp_up(lo):
                act = jnp.maximum(_dot(hn, w1_ref[:, lo:lo + half]), 0.0)
                return (act * act).astype(BF16)

            act_a = mlp_up(0)
        if do_attn:
            sp0, spb0 = softplus_phase(z0, causal)
            lat0 = cumsum_phase(spb0)
        if do_ffn:
            act_b = mlp_up(half)
        if do_attn:
            sp1, spb1 = softplus_phase(z1, None)
            lat1 = cumsum_phase(spb1)
        if do_ffn:
            hres = hres + _dot(act_a, w2_ref[0:half, :])
        if do_attn:
            w0 = weight_phase(z0, sp0, lat0, causal)
            w1 = weight_phase(z1, sp1, lat1, None)
        if do_ffn:
            hres = hres + _dot(act_b, w2_ref[half:2 * half, :])
        if do_attn:
            pv0, tot0 = value_phase(vtd_ref.at[0], w0, lat0, spb0)
            pv1, tot1 = value_phase(vtp_ref.at[0], w1, lat1, spb1)
        if do_ffn:
            o_ref[...] = _rms(hres, gf_ref[...]).astype(o_ref.dtype)
        if do_attn:
            for h in heads:
                scale = jnp.where(has_prev, jnp.exp2(-tot0[h]), 0.0)
                acc = pv0[h] + scale * pv1[h]
                acc_ref[h] = acc
                osb_buf[slot, h * SB_HEAD_DIM:(h + 1) * SB_HEAD_DIM, :] = acc.astype(osb_buf.dtype)
                carry_ref[h:h + 1, :] = tot0[h] + jnp.where(has_prev, tot1[h], 0.0)

    def sweep_rest():
        def more(state):
            n, live = state
            return jnp.logical_and(n < qi, live)

        seq = jnp.minimum(i, n_tiles - 1) // tiles_per_seq

        def tile_copies(j):
            start = pl.multiple_of(j * t, t)
            return (pltpu.make_async_copy(k_hbm.at[seq, pl.ds(start, t), :], k_buf, sems.at[0]),
                    pltpu.make_async_copy(vt_hbm.at[seq, :, pl.ds(start, t)], vt_buf, sems.at[1]))

        def body(state):
            n, _ = state
            copies = tile_copies(qi - 1 - n)
            for c in copies:
                c.start()
            for c in copies:
                c.wait()
            z = logits(k_buf[...])
            sp, spb = softplus_phase(z, None)
            later = cumsum_phase(spb)
            pv, tot = value_phase(vt_buf, weight_phase(z, sp, later, None), later, spb)
            for h in heads:
                c = carry_ref[h:h + 1, :]
                acc_ref[h] += jnp.exp2(-c) * pv[h]
                carry_ref[h:h + 1, :] = c + tot[h]
            return n + 1, jnp.min(carry_ref[...]) < SB_DEAD_CARRY

        swept, _ = lax.while_loop(more, body,
                                  (jnp.int32(1), jnp.min(carry_ref[...]) < SB_DEAD_CARRY))

        @pl.when(swept > 1)
        def _():
            osb_buf[slot] = acc_ref[...].reshape(SB_HEADS * SB_HEAD_DIM, t).astype(osb_buf.dtype)

    weight_copies = [pltpu.make_async_copy(src, dst, w_sems.at[n]) for n, (src, dst) in enumerate(
        ((wsb_hbm, wsb_ref), (whg_hbm, whg_ref), (wout_hbm, wout_ref), (w1_hbm, w1_ref),
         (w2_hbm, w2_ref)))]

    @pl.when(i == 0)
    def _():
        for c in weight_copies:
            c.start()
        main_block(True, False)
        sweep_rest()

    @pl.when(i == 1)
    def _():
        for c in weight_copies:
            c.wait()

    @pl.when(jnp.logical_and(i > 0, i < n_tiles))
    def _():
        main_block(True, True)
        sweep_rest()

    @pl.when(i == n_tiles)
    def _():
        main_block(False, True)


def _attn_ffn(q3, k3, vt3, x2, ohg, gates, bg, wsb, whg, wout, g2, w1, w2, gf):
    b, s, sb_w = q3.shape
    n, d = x2.shape
    t = ATTN_TILE
    dff = w1.shape[1]
    tiles_per_seq = s // t
    n_tiles = n // t
    assert sb_w == SB_HEADS * SB_HEAD_DIM and tiles_per_seq * t == s and n_tiles == b * tiles_per_seq
    cur = lambda i: jnp.minimum(i, n_tiles - 1)
    prv = lambda i: (jnp.maximum(i - 1, 0), 0)
    seq = lambda i: cur(i) // tiles_per_seq
    tile = lambda i: cur(i) % tiles_per_seq
    before = lambda i: jnp.maximum(tile(i) - 1, 0)
    k_tile = lambda pos: pl.BlockSpec((1, t, sb_w), lambda i: (seq(i), pos(i), 0))
    vt_tile = lambda pos: pl.BlockSpec((1, sb_w, t), lambda i: (seq(i), 0, pos(i)))
    in_hbm = pl.BlockSpec(memory_space=pl.ANY)
    full = lambda a: _resident(a.shape)
    weights = (wsb, whg, wout, w1, w2)
    resident_bytes = sum(a.size * a.dtype.itemsize for a in (bg, g2, gf) + weights) \
        + 2 * t * sb_w * 2 + 2 * t * sb_w * 2 + SB_HEADS * t * (V7X_LANES * 2 + SB_HEAD_DIM * 4 + 4)
    moving = t * (5 * sb_w * 2 + d * 4 + ohg.shape[1] * 2 + 2 * d * 4 + d * 4)
    return pl.pallas_call(
        functools.partial(_attn_ffn_kernel, tiles_per_seq=tiles_per_seq, n_tiles=n_tiles),
        grid=(n_tiles + 1,),
        in_specs=[k_tile(tile), k_tile(tile), k_tile(before), vt_tile(tile), vt_tile(before),
                  in_hbm, in_hbm,
                  pl.BlockSpec((t, d), prv),
                  pl.BlockSpec((t, ohg.shape[1]), prv),
                  pl.BlockSpec((t, 2 * d), prv),
                  full(bg), full(g2), full(gf)] + [in_hbm] * len(weights),
        out_specs=pl.BlockSpec((t, d), prv),
        out_shape=jax.ShapeDtypeStruct((n, d), x2.dtype),
        scratch_shapes=[pltpu.VMEM((2, sb_w, t), BF16),
                        pltpu.VMEM((SB_HEADS, t, V7X_LANES), BF16),
                        pltpu.VMEM((SB_HEADS, SB_HEAD_DIM, t), F32),
                        pltpu.VMEM((SB_HEADS, t), F32),
                        pltpu.VMEM((t, sb_w), BF16),
                        pltpu.VMEM((sb_w, t), BF16),
                        pltpu.SemaphoreType.DMA((2,))]
        + [pltpu.VMEM(a.shape, a.dtype) for a in weights]
        + [pltpu.SemaphoreType.DMA((len(weights),))],
        compiler_params=pltpu.CompilerParams(
            dimension_semantics=("arbitrary",),
            vmem_limit_bytes=_vmem_limit(moving, resident_bytes,
                                         t * (dff * 6 + d * 24) + 8 * SB_HEADS * t * t * 4)),
        name="attn_ffn",
    )(q3, k3, k3, vt3, vt3, k3, vt3, x2, ohg, gates, bg, g2, gf, *weights)


def kernel(x, norm1_g, w_in, b_gate, lb_logits, hg_norm_g, w_o_sb, w_o_hg, w_out, norm2_g,
           w_ff1, w_ff2, final_g):
    b, s, d = x.shape
    assert w_in.shape[0] == 1, "single-layer block"
    sb_w = SB_HEADS * SB_HEAD_DIM
    hg_w = HG_HEADS * HG_HEAD_DIM
    x2 = x.reshape(b * s, d)
    later = (w_o_sb[0], w_o_hg[0], w_out[0], w_ff1[0], w_ff2[0])
    q, k, vt, gates, o_hg, wsb, whg, wout, w1, w2 = _proj_hgrn2(
        x2, norm1_g, w_in[0].astype(BF16), lb_logits, hg_norm_g, later, b, sb_w, hg_w)
    out = _attn_ffn(q.reshape(b, s, sb_w), k.reshape(b, s, sb_w), vt, x2, o_hg, gates, b_gate,
                    wsb, whg, wout, norm2_g, w1, w2, final_g.reshape(1, d))
    return out.reshape(b, s, d)
```

```python
import functools

import jax
import jax.numpy as jnp
import numpy as np
from jax import lax
from jax.experimental import pallas as pl
from jax.experimental.pallas import tpu as pltpu

F32 = jnp.float32
BF16 = jnp.bfloat16

SB_HEADS = 8
SB_HEAD_DIM = 64
HG_HEADS = 4
HG_HEAD_DIM = 128
EPS = 1e-6
LOG2E = 1.4426950408889634
SB_DEAD_CARRY = 151.0

V7X_LANES = 128
V7X_MXU_DIM = 256
V7X_VMEM_BYTES = 64 * 1024 * 1024

ATTN_TILE = V7X_MXU_DIM
HG_BLOCK = V7X_MXU_DIM
PROJ_BLOCKS = 2


def _vmem_limit(pipelined_bytes, resident_bytes, temp_bytes):
    need = 2 * pipelined_bytes + resident_bytes + temp_bytes
    return int(min(need + need // 4, V7X_VMEM_BYTES - 8 * 1024 * 1024))


def _resident(shape):
    return pl.BlockSpec(shape, lambda *_: (0,) * len(shape), pipeline_mode=pl.Buffered(1))


def _rms(x, g):
    ms = jnp.mean(x * x, axis=-1, keepdims=True)
    return x * lax.rsqrt(ms + EPS) * g


def _dot(a, b):
    return jnp.dot(a, b, preferred_element_type=F32)


def _dot_nt(a, b):
    return lax.dot_general(a, b, (((1,), (1,)), ((), ())), preferred_element_type=F32)


def _dot_tn(a, b):
    return lax.dot_general(a, b, (((0,), (0,)), ((), ())), preferred_element_type=F32)


def _softplus2(z, mask):
    sp = jnp.maximum(z, 0.0) + jnp.log(1.0 + jnp.exp2(-jnp.abs(z))) * LOG2E
    return sp if mask is None else jnp.where(mask, sp, 0.0)


def _split3(x):
    a = x.astype(BF16)
    r = x - a.astype(F32)
    b = r.astype(BF16)
    c = (r - b.astype(F32)).astype(BF16)
    return a, b, c


def _rows_from_group(b, group, r):
    n, c = b.shape
    if group == n:
        return jnp.broadcast_to(b[r:r + 1, :], (n, c))
    b3 = b.reshape(n // group, group, c)
    return jnp.broadcast_to(b3[:, r:r + 1, :], b3.shape).reshape(n, c)


def _midpoint_rows(b, group, pos):
    n = b.shape[0]
    half = group // 2
    if group >= 16:
        return _rows_from_group(b, group, half - 1)
    if group == 8:
        return _rows_from_group(b, 8, 3)
    up1 = pltpu.roll(b, n - 1, 0)
    dn1 = pltpu.roll(b, 1, 0)
    if group == 2:
        return jnp.where((pos & 1) == 0, b, dn1)
    assert group == 4
    dn2 = pltpu.roll(b, 2, 0)
    r4 = pos & 3
    return jnp.where(r4 == 0, up1, jnp.where(r4 == 1, b, jnp.where(r4 == 2, dn1, dn2)))


def _hgrn2_gates(fr, qr, lb):
    e = jnp.exp(-jnp.abs(fr))
    r = 1.0 / (1.0 + e)
    er = e * r
    sig = jnp.where(fr >= 0, r, er)
    nsig = jnp.where(fr >= 0, er, r)
    logf2 = jnp.log(lb + (1.0 - lb) * sig) * LOG2E
    k = (1.0 - lb) * nsig
    q = qr / (1.0 + jnp.exp(-qr))
    return logf2, k, q


def _hgrn2_head_products(b, q, k, v, lvl, pos, st_ref, h, keep):
    t, dk = b.shape
    n_levels = t.bit_length() - 1
    hb = t // 2
    zeros = jnp.zeros((hb, dk), BF16)
    lvl_d = jnp.concatenate([lvl[0:hb, 0:hb], lvl[hb:t, hb:t]], axis=1)
    diag = jnp.zeros((hb, t), F32)
    for level in range(1, n_levels):
        group = 1 << level
        d = b - _midpoint_rows(b, group, pos)
        later = (pos & (group - 1)) >= (group // 2)
        fac = jnp.exp2(-jnp.abs(d))
        ql = jnp.where(later, q * fac, 0.0).astype(BF16)
        kl = jnp.where(later, 0.0, k * fac).astype(BF16)
        lhs = jnp.concatenate([ql[0:hb], ql[hb:t]], axis=1)
        rhs = jnp.concatenate([jnp.concatenate([kl[0:hb], zeros], axis=1),
                               jnp.concatenate([zeros, kl[hb:t]], axis=1)], axis=0)
        diag = jnp.where(lvl_d == level, _dot_nt(lhs, rhs), diag)
    b_mid = b[hb - 1:hb, :]
    q_top = (q[hb:t] * jnp.exp2(b[hb:t] - b_mid)).astype(BF16)
    k_top = (k[0:hb] * jnp.exp2(b_mid - b[0:hb])).astype(BF16)
    top = _dot_nt(q_top, k_top)
    scores = jnp.concatenate(
        [jnp.concatenate([diag[:, 0:hb], jnp.zeros((hb, hb), F32)], axis=1),
         jnp.concatenate([top, diag[:, hb:t]], axis=1)], axis=0)

    st = st_ref[h] * keep
    b_last = b[t - 1:t, :]
    o = _dot_nt((q * jnp.exp2(b)).astype(BF16), st.astype(BF16))
    o = o + jnp.sum(q * k, axis=1, keepdims=True) * v.astype(F32)
    k_dec = (k * jnp.exp2(b_last - b)).astype(BF16)
    st_ref[h] = jnp.exp2(b_last) * st + _dot_tn(v, k_dec)
    return scores.astype(BF16), o


def _hgrn2_head_output(scores, o, v, gr, ng):
    return _rms(o + _dot(scores, v), ng) * (gr / (1.0 + jnp.exp(-gr)))


def _proj_hgrn2_kernel(*refs, tiles_per_seq, n_cast):
    x0_ref, xnext_ref, g_ref, w_hbm, lbl_ref, ng_ref, lvl_ref = refs[:7]
    cast_in = refs[7:7 + n_cast]
    q_ref, k_ref, vt_ref, gate_ref, ohg_ref = refs[7 + n_cast:12 + n_cast]
    cast_out = refs[12 + n_cast:12 + 2 * n_cast]
    xn_buf, f_buf, iqg_buf, st_ref, w_ref, w_stage, w_sems = refs[12 + 2 * n_cast:]
    d = xnext_ref.shape[1]
    t = HG_BLOCK
    n_blocks = xnext_ref.shape[0] // t
    dk = HG_HEAD_DIM
    hg_w = HG_HEADS * dk
    sb_w = q_ref.shape[1]
    i = pl.program_id(0)
    n_steps = pl.num_programs(0)
    slot = i % 2
    prev = 1 - slot

    def step(do_proj, do_hgrn2):
        row = lax.broadcasted_iota(jnp.int32, (t, t), 0)
        col = lax.broadcasted_iota(jnp.int32, (t, t), 1)
        tril = (row >= col).astype(BF16)
        pos = lax.broadcasted_iota(jnp.int32, (t, dk), 0)
        lvl = lvl_ref[...]
        first_keep = jnp.where(i % tiles_per_seq == 1, 0.0, 1.0)

        lbl = lbl_ref[...]
        ex = jnp.exp(lbl - jnp.max(lbl, axis=0, keepdims=True))
        lb_all = ex[0:1, :] / jnp.sum(ex, axis=0, keepdims=True)

        def prepare_block(blk):
            rows = slice(blk * t, (blk + 1) * t)
            logf2, k_all, q_all = _hgrn2_gates(
                f_buf[prev, rows, :], iqg_buf[prev, rows, hg_w:2 * hg_w].astype(F32), lb_all)
            g1, g2, g3 = _split3(logf2)
            return _dot(tril, g1) + _dot(tril, g2) + _dot(tril, g3), q_all, k_all

        def head_products(blk, h, prepared):
            b_all, q_all, k_all = prepared
            rows = slice(blk * t, (blk + 1) * t)
            sl = slice(h * dk, (h + 1) * dk)
            return _hgrn2_head_products(b_all[:, sl], q_all[:, sl], k_all[:, sl],
                                        iqg_buf[prev, rows, sl], lvl, pos, st_ref, h,
                                        first_keep if blk == 0 else 1.0)

        def head_output(blk, h, products):
            rows = slice(blk * t, (blk + 1) * t)
            sl = slice(h * dk, (h + 1) * dk)
            gr = iqg_buf[prev, rows, 2 * hg_w + h * dk:2 * hg_w + (h + 1) * dk].astype(F32)
            y = _hgrn2_head_output(*products, iqg_buf[prev, rows, sl], gr, ng_ref[:, sl])
            ohg_ref[rows, sl] = y.astype(ohg_ref.dtype)

        c0 = 3 * sb_w
        c1 = c0 + hg_w
        c2 = c1 + 3 * hg_w
        slab = d // 2
        slabs = []
        if do_proj:
            xn = xn_buf[slot]

            def project(lo, hi):
                return _dot(xn, w_ref[:, lo:hi])

            def q_slab():
                q_ref[...] = (project(0, sb_w) * (SB_HEAD_DIM ** -0.5 * LOG2E)).astype(q_ref.dtype)

            def k_slab():
                k_ref[...] = project(sb_w, 2 * sb_w).astype(k_ref.dtype)

            def v_slab():
                v = project(2 * sb_w, 3 * sb_w)
                for blk in range(n_blocks):
                    vt_ref[0, :, blk * t:(blk + 1) * t] = \
                        v[blk * t:(blk + 1) * t].T.astype(vt_ref.dtype)

            def f_slab():
                f_buf[slot] = project(c0, c1)

            def iqg_slab(lo):
                def run():
                    iqg_buf[slot, :, lo:lo + slab] = \
                        project(c1 + lo, c1 + lo + slab).astype(iqg_buf.dtype)
                return run

            def gate_slab(lo):
                def run():
                    gate_ref[:, lo:lo + slab] = project(c2 + lo, c2 + lo + slab)
                return run

            for src, dst in zip(cast_in, cast_out):
                dst[...] = src[...].astype(dst.dtype)
            slabs = [v_slab, f_slab] + [iqg_slab(lo) for lo in range(0, 3 * hg_w, slab)] \
                + [gate_slab(lo) for lo in range(0, 2 * d, slab)]
            q_slab()
            k_slab()
        if do_hgrn2:
            prepared = prepare_block(0)
            if slabs:
                slabs.pop(0)()
            for blk in range(n_blocks):
                for h in range(HG_HEADS):
                    if blk > 0 and h == 0:
                        prepared = prepare_block(blk)
                    products = head_products(blk, h, prepared)
                    if slabs:
                        slabs.pop(0)()
                    head_output(blk, h, products)
        for run in slabs:
            run()
        if do_proj:
            xn_buf[prev] = _rms(xnext_ref[...], g_ref[...]).astype(BF16)

    @pl.when(i == 0)
    def _():
        st_ref[...] = jnp.zeros_like(st_ref)
        xn_buf[0] = _rms(x0_ref[...], g_ref[...]).astype(BF16)
        width = w_stage.shape[2]
        n_slabs = w_ref.shape[1] // width

        def slab_copy(n):
            return pltpu.make_async_copy(w_hbm.at[:, pl.ds(n * width, width)],
                                         w_stage.at[n % 2], w_sems.at[n % 2])

        slab_copy(0).start()
        for n in range(n_slabs):
            if n + 1 < n_slabs:
                slab_copy(n + 1).start()
            slab_copy(n).wait()
            w_ref[:, n * width:(n + 1) * width] = w_stage[n % 2].astype(w_ref.dtype)
        step(True, False)

    @pl.when(jnp.logical_and(i > 0, i < n_steps - 1))
    def _():
        step(True, True)

    @pl.when(i == n_steps - 1)
    def _():
        step(False, True)


def _pair_levels(t):
    idx = np.arange(t)
    x = idx[:, None] ^ idx[None, :]
    lev = np.where(x > 0, np.floor(np.log2(np.maximum(x, 1))).astype(np.int64) + 1, 0)
    return np.where(idx[:, None] > idx[None, :], lev, 0).astype(np.int32)


def _proj_hgrn2(x2, g1, w_in, lb_logits, ng, later_weights, batch, sb_w, hg_w):
    n, d = x2.shape
    t = PROJ_BLOCKS * HG_BLOCK
    cols = w_in.shape[1]
    n_tiles = n // t
    tiles_per_seq = n_tiles // batch
    assert hg_w == HG_HEADS * HG_HEAD_DIM and n_tiles * t == n and tiles_per_seq * batch == n_tiles
    lvl = jnp.asarray(_pair_levels(HG_BLOCK))
    cur = lambda i: jnp.minimum(i, n_tiles - 1)
    rows = lambda w: pl.BlockSpec((t, w), lambda i: (cur(i), 0))
    vt_spec = pl.BlockSpec((1, sb_w, t),
                           lambda i: (cur(i) // tiles_per_seq, 0, cur(i) % tiles_per_seq))
    sds = lambda w, dt: jax.ShapeDtypeStruct((n, w), dt)
    cast_specs = [pl.BlockSpec((w.shape[0] // n_tiles, w.shape[1]), lambda i: (cur(i), 0))
                  for w in later_weights]
    assert all(w.shape[0] % (n_tiles * 16) == 0 for w in later_weights)
    cast_bytes = sum(w.size // n_tiles * 6 for w in later_weights)
    moving = t * (d * 4 + 2 * sb_w * 2 + sb_w * 2 + 2 * d * 4 + hg_w * 2) + cast_bytes
    stage_cols = d // 2
    assert cols % stage_cols == 0
    resident = d * 4 + d * cols * 2 + 2 * d * stage_cols * 4 + 3 * hg_w * 4 \
        + HG_BLOCK * HG_BLOCK * 4 + t * d * 4 \
        + 2 * t * d * 2 + 2 * t * hg_w * (4 + 3 * 2) + HG_HEADS * HG_HEAD_DIM * HG_HEAD_DIM * 4
    return pl.pallas_call(
        functools.partial(_proj_hgrn2_kernel, tiles_per_seq=tiles_per_seq,
                          n_cast=len(later_weights)),
        grid=(n_tiles + 1,),
        in_specs=[_resident((t, d)), pl.BlockSpec((t, d), lambda i: (cur(i + 1), 0)),
                  _resident((1, d)), pl.BlockSpec(memory_space=pl.ANY),
                  _resident(lb_logits.shape), _resident(ng.shape),
                  _resident((HG_BLOCK, HG_BLOCK))] + cast_specs,
        out_specs=[rows(sb_w), rows(sb_w), vt_spec, rows(2 * d),
                   pl.BlockSpec((t, hg_w), lambda i: (jnp.maximum(i - 1, 0), 0))] + cast_specs,
        out_shape=[sds(sb_w, BF16), sds(sb_w, BF16),
                   jax.ShapeDtypeStruct((batch, sb_w, n // batch), BF16),
                   sds(2 * d, F32), sds(hg_w, BF16)]
        + [jax.ShapeDtypeStruct(w.shape, BF16) for w in later_weights],
        scratch_shapes=[pltpu.VMEM((2, t, d), BF16),
                        pltpu.VMEM((2, t, hg_w), F32),
                        pltpu.VMEM((2, t, 3 * hg_w), BF16),
                        pltpu.VMEM((HG_HEADS, HG_HEAD_DIM, HG_HEAD_DIM), F32),
                        pltpu.VMEM((d, cols), BF16),
                        pltpu.VMEM((2, d, stage_cols), F32),
                        pltpu.SemaphoreType.DMA((2,))],
        compiler_params=pltpu.CompilerParams(
            dimension_semantics=("arbitrary",),
            vmem_limit_bytes=_vmem_limit(moving, resident,
                                         t * (d * 6 + cols * 4) + 64 * HG_BLOCK * HG_BLOCK * 4)),
        name="proj_hgrn2",
    )(x2, x2, g1, w_in, lb_logits, ng, lvl, *later_weights)


def _attn_ffn_kernel(q_ref, kd_ref, kp_ref, vtd_ref, vtp_ref, k_hbm, vt_hbm, x_ref, ohg_ref,
                     gate_ref, bg_ref, g2_ref, gf_ref, wsb_hbm, whg_hbm, wout_hbm, w1_hbm, w2_hbm,
                     o_ref, osb_buf, qm_ref, acc_ref, carry_ref, k_buf, vt_buf, sems,
                     wsb_ref, whg_ref, wout_ref, w1_ref, w2_ref, w_sems,
                     *, tiles_per_seq, n_tiles):
    t = q_ref.shape[1]
    d = x_ref.shape[1]
    heads = range(SB_HEADS)
    i = pl.program_id(0)
    slot = i % 2
    qi = jnp.minimum(i, n_tiles - 1) % tiles_per_seq

    row = lax.broadcasted_iota(jnp.int32, (t, t), 0)
    col = lax.broadcasted_iota(jnp.int32, (t, t), 1)
    tri = (col > row).astype(BF16)
    causal = row < col

    def logits(k):
        return [_dot_nt(k[:, (h // 2) * V7X_LANES:(h // 2 + 1) * V7X_LANES], qm_ref[h])
                for h in heads]

    def softplus_phase(z, mask):
        sp = [_softplus2(z[h], mask) for h in heads]
        return sp, [sp[h].astype(BF16) for h in heads]

    def cumsum_phase(spb):
        return [_dot(tri, spb[h]) for h in heads]

    def weight_phase(z, sp, later, mask):
        w = [jnp.exp2(z[h] - sp[h] - later[h]) for h in heads]
        if mask is not None:
            w = [jnp.where(mask, w[h], 0.0) for h in heads]
        return [w[h].astype(BF16) for h in heads]

    def value_phase(vt, w, later, spb):
        pv = [_dot(vt[h * SB_HEAD_DIM:(h + 1) * SB_HEAD_DIM, :], w[h]) for h in heads]
        return pv, [later[h][0:1, :] + spb[h][0:1, :].astype(F32) for h in heads]

    def main_block(do_attn, do_ffn):
        has_prev = qi > 0
        if do_attn:
            lane = lax.broadcasted_iota(jnp.int32, (t, V7X_LANES), 1)
            zero = jnp.zeros((), BF16)
            q = q_ref[0]
            for h in heads:
                grp = q[:, (h // 2) * V7X_LANES:(h // 2 + 1) * V7X_LANES]
                qm_ref[h] = jnp.where((lane // SB_HEAD_DIM) == (h % 2), grp, zero)
        if do_ffn:
            a_sb = _dot_tn(osb_buf[1 - slot], wsb_ref[...])
            a_hg = _dot(ohg_ref[...], whg_ref[...])
        if do_attn:
            z0 = logits(kd_ref[0])
        if do_ffn:
            gates = 1.0 / (1.0 + jnp.exp(-(gate_ref[...] + bg_ref[...])))
            merged = (gates[:, :d] * a_sb + gates[:, d:] * a_hg).astype(BF16)
            hres = x_ref[...] + _dot(merged, wout_ref[...])
        if do_attn:
            z1 = logits(kp_ref[0])
        if do_ffn:
            hn = _rms(hres, g2_ref[...]).astype(BF16)
            half = w1_ref.shape[1] // 2

            def mlp_up(lo):
                act = jnp.maximum(_dot(hn, w1_ref[:, lo:lo + half]), 0.0)
                return (act * act).astype(BF16)

            act_a = mlp_up(0)
        if do_attn:
            sp0, spb0 = softplus_phase(z0, causal)
            lat0 = cumsum_phase(spb0)
        if do_ffn:
            act_b = mlp_up(half)
        if do_attn:
            sp1, spb1 = softplus_phase(z1, None)
            lat1 = cumsum_phase(spb1)
        if do_ffn:
            hres = hres + _dot(act_a, w2_ref[0:half, :])
        if do_attn:
            w0 = weight_phase(z0, sp0, lat0, causal)
            w1 = weight_phase(z1, sp1, lat1, None)
        if do_ffn:
            hres = hres + _dot(act_b, w2_ref[half:2 * half, :])
        if do_attn:
            pv0, tot0 = value_phase(vtd_ref.at[0], w0, lat0, spb0)
            pv1, tot1 = value_phase(vtp_ref.at[0], w1, lat1, spb1)
        if do_ffn:
            o_ref[...] = _rms(hres, gf_ref[...]).astype(o_ref.dtype)
        if do_attn:
            for h in heads:
                scale = jnp.where(has_prev, jnp.exp2(-tot0[h]), 0.0)
                acc_ref[h] = pv0[h] + scale * pv1[h]
                carry_ref[h:h + 1, :] = tot0[h] + jnp.where(has_prev, tot1[h], 0.0)

    def sweep_rest():
        def more(state):
            n, live = state
            return jnp.logical_and(n < qi, live)

        seq = jnp.minimum(i, n_tiles - 1) // tiles_per_seq

        def tile_copies(j):
            start = pl.multiple_of(j * t, t)
            return (pltpu.make_async_copy(k_hbm.at[seq, pl.ds(start, t), :], k_buf, sems.at[0]),
                    pltpu.make_async_copy(vt_hbm.at[seq, :, pl.ds(start, t)], vt_buf, sems.at[1]))

        def body(state):
            n, _ = state
            copies = tile_copies(qi - 1 - n)
            for c in copies:
                c.start()
            for c in copies:
                c.wait()
            z = logits(k_buf[...])
            sp, spb = softplus_phase(z, None)
            later = cumsum_phase(spb)
            pv, tot = value_phase(vt_buf, weight_phase(z, sp, later, None), later, spb)
            for h in heads:
                c = carry_ref[h:h + 1, :]
                acc_ref[h] += jnp.exp2(-c) * pv[h]
                carry_ref[h:h + 1, :] = c + tot[h]
            return n + 1, jnp.min(carry_ref[...]) < SB_DEAD_CARRY

        lax.while_loop(more, body, (jnp.int32(1), jnp.min(carry_ref[...]) < SB_DEAD_CARRY))
        osb_buf[slot] = acc_ref[...].reshape(SB_HEADS * SB_HEAD_DIM, t).astype(osb_buf.dtype)

    weight_copies = [pltpu.make_async_copy(src, dst, w_sems.at[n]) for n, (src, dst) in enumerate(
        ((wsb_hbm, wsb_ref), (whg_hbm, whg_ref), (wout_hbm, wout_ref), (w1_hbm, w1_ref),
         (w2_hbm, w2_ref)))]

    @pl.when(i == 0)
    def _():
        for c in weight_copies:
            c.start()
        main_block(True, False)
        sweep_rest()

    @pl.when(i == 1)
    def _():
        for c in weight_copies:
            c.wait()

    @pl.when(jnp.logical_and(i > 0, i < n_tiles))
    def _():
        main_block(True, True)
        sweep_rest()

    @pl.when(i == n_tiles)
    def _():
        main_block(False, True)


def _attn_ffn(q3, k3, vt3, x2, ohg, gates, bg, wsb, whg, wout, g2, w1, w2, gf):
    b, s, sb_w = q3.shape
    n, d = x2.shape
    t = ATTN_TILE
    dff = w1.shape[1]
    tiles_per_seq = s // t
    n_tiles = n // t
    assert sb_w == SB_HEADS * SB_HEAD_DIM and tiles_per_seq * t == s and n_tiles == b * tiles_per_seq
    cur = lambda i: jnp.minimum(i, n_tiles - 1)
    prv = lambda i: (jnp.maximum(i - 1, 0), 0)
    seq = lambda i: cur(i) // tiles_per_seq
    tile = lambda i: cur(i) % tiles_per_seq
    before = lambda i: jnp.maximum(tile(i) - 1, 0)
    k_tile = lambda pos: pl.BlockSpec((1, t, sb_w), lambda i: (seq(i), pos(i), 0))
    vt_tile = lambda pos: pl.BlockSpec((1, sb_w, t), lambda i: (seq(i), 0, pos(i)))
    in_hbm = pl.BlockSpec(memory_space=pl.ANY)
    full = lambda a: _resident(a.shape)
    weights = (wsb, whg, wout, w1, w2)
    resident_bytes = sum(a.size * a.dtype.itemsize for a in (bg, g2, gf) + weights) \
        + 2 * t * sb_w * 2 + 2 * t * sb_w * 2 + SB_HEADS * t * (V7X_LANES * 2 + SB_HEAD_DIM * 4 + 4)
    moving = t * (5 * sb_w * 2 + d * 4 + ohg.shape[1] * 2 + 2 * d * 4 + d * 4)
    return pl.pallas_call(
        functools.partial(_attn_ffn_kernel, tiles_per_seq=tiles_per_seq, n_tiles=n_tiles),
        grid=(n_tiles + 1,),
        in_specs=[k_tile(tile), k_tile(tile), k_tile(before), vt_tile(tile), vt_tile(before),
                  in_hbm, in_hbm,
                  pl.BlockSpec((t, d), prv),
                  pl.BlockSpec((t, ohg.shape[1]), prv),
                  pl.BlockSpec((t, 2 * d), prv),
                  full(bg), full(g2), full(gf)] + [in_hbm] * len(weights),
        out_specs=pl.BlockSpec((t, d), prv),
        out_shape=jax.ShapeDtypeStruct((n, d), x2.dtype),
        scratch_shapes=[pltpu.VMEM((2, sb_w, t), BF16),
                        pltpu.VMEM((SB_HEADS, t, V7X_LANES), BF16),
                        pltpu.VMEM((SB_HEADS, SB_HEAD_DIM, t), F32),
                        pltpu.VMEM((SB_HEADS, t), F32),
                        pltpu.VMEM((t, sb_w), BF16),
                        pltpu.VMEM((sb_w, t), BF16),
                        pltpu.SemaphoreType.DMA((2,))]
        + [pltpu.VMEM(a.shape, a.dtype) for a in weights]
        + [pltpu.SemaphoreType.DMA((len(weights),))],
        compiler_params=pltpu.CompilerParams(
            dimension_semantics=("arbitrary",),
            vmem_limit_bytes=_vmem_limit(moving, resident_bytes,
                                         t * (dff * 6 + d * 24) + 8 * SB_HEADS * t * t * 4)),
        name="attn_ffn",
    )(q3, k3, k3, vt3, vt3, k3, vt3, x2, ohg, gates, bg, g2, gf, *weights)


def kernel(x, norm1_g, w_in, b_gate, lb_logits, hg_norm_g, w_o_sb, w_o_hg, w_out, norm2_g,
           w_ff1, w_ff2, final_g):
    b, s, d = x.shape
    assert w_in.shape[0] == 1, "single-layer block"
    sb_w = SB_HEADS * SB_HEAD_DIM
    hg_w = HG_HEADS * HG_HEAD_DIM
    x2 = x.reshape(b * s, d)
    later = (w_o_sb[0], w_o_hg[0], w_out[0], w_ff1[0], w_ff2[0])
    q, k, vt, gates, o_hg, wsb, whg, wout, w1, w2 = _proj_hgrn2(
        x2, norm1_g, w_in[0], lb_logits, hg_norm_g, later, b, sb_w, hg_w)
    out = _attn_ffn(q.reshape(b, s, sb_w), k.reshape(b, s, sb_w), vt, x2, o_hg, gates, b_gate,
                    wsb, whg, wout, norm2_g, w1, w2, final_g.reshape(1, d))
    return out.reshape(b, s, d)
```

```python
import functools

import jax
import jax.numpy as jnp
import numpy as np
from jax import lax
from jax.experimental import pallas as pl
from jax.experimental.pallas import tpu as pltpu

F32 = jnp.float32
BF16 = jnp.bfloat16

SB_HEADS = 8
SB_HEAD_DIM = 64
HG_HEADS = 4
HG_HEAD_DIM = 128
EPS = 1e-6
LOG2E = 1.4426950408889634
SB_DEAD_CARRY = 151.0

V7X_LANES = 128
V7X_MXU_DIM = 256
V7X_VMEM_BYTES = 64 * 1024 * 1024

ATTN_TILE = V7X_MXU_DIM
HG_BLOCK = V7X_MXU_DIM
PROJ_BLOCKS = 2


def _vmem_limit(pipelined_bytes, resident_bytes, temp_bytes):
    need = 2 * pipelined_bytes + resident_bytes + temp_bytes
    return int(min(need + need // 4, V7X_VMEM_BYTES - 8 * 1024 * 1024))


def _resident(shape):
    return pl.BlockSpec(shape, lambda *_: (0,) * len(shape), pipeline_mode=pl.Buffered(1))


def _rms(x, g):
    ms = jnp.mean(x * x, axis=-1, keepdims=True)
    return x * lax.rsqrt(ms + EPS) * g


def _dot(a, b):
    return jnp.dot(a, b, preferred_element_type=F32)


def _dot_nt(a, b):
    return lax.dot_general(a, b, (((1,), (1,)), ((), ())), preferred_element_type=F32)


def _dot_tn(a, b):
    return lax.dot_general(a, b, (((0,), (0,)), ((), ())), preferred_element_type=F32)


def _softplus2(z, mask):
    sp = jnp.maximum(z, 0.0) + jnp.log(1.0 + jnp.exp2(-jnp.abs(z))) * LOG2E
    return sp if mask is None else jnp.where(mask, sp, 0.0)


def _split3(x):
    a = x.astype(BF16)
    r = x - a.astype(F32)
    b = r.astype(BF16)
    c = (r - b.astype(F32)).astype(BF16)
    return a, b, c


def _rows_from_group(b, group, r):
    n, c = b.shape
    if group == n:
        return jnp.broadcast_to(b[r:r + 1, :], (n, c))
    b3 = b.reshape(n // group, group, c)
    return jnp.broadcast_to(b3[:, r:r + 1, :], b3.shape).reshape(n, c)


def _midpoint_rows(b, group, pos):
    n = b.shape[0]
    half = group // 2
    if group >= 16:
        return _rows_from_group(b, group, half - 1)
    if group == 8:
        return _rows_from_group(b, 8, 3)
    up1 = pltpu.roll(b, n - 1, 0)
    dn1 = pltpu.roll(b, 1, 0)
    if group == 2:
        return jnp.where((pos & 1) == 0, b, dn1)
    assert group == 4
    dn2 = pltpu.roll(b, 2, 0)
    r4 = pos & 3
    return jnp.where(r4 == 0, up1, jnp.where(r4 == 1, b, jnp.where(r4 == 2, dn1, dn2)))


def _hgrn2_gates(fr, qr, lb):
    e = jnp.exp(-jnp.abs(fr))
    r = 1.0 / (1.0 + e)
    er = e * r
    sig = jnp.where(fr >= 0, r, er)
    nsig = jnp.where(fr >= 0, er, r)
    logf2 = jnp.log(lb + (1.0 - lb) * sig) * LOG2E
    k = (1.0 - lb) * nsig
    q = qr / (1.0 + jnp.exp(-qr))
    return logf2, k, q


def _hgrn2_head_products(b, q, k, v, lvl, pos, st_ref, h, keep):
    t, dk = b.shape
    n_levels = t.bit_length() - 1
    hb = t // 2
    zeros = jnp.zeros((hb, dk), BF16)
    lvl_d = jnp.concatenate([lvl[0:hb, 0:hb], lvl[hb:t, hb:t]], axis=1)
    diag = jnp.zeros((hb, t), F32)
    for level in range(1, n_levels):
        group = 1 << level
        d = b - _midpoint_rows(b, group, pos)
        later = (pos & (group - 1)) >= (group // 2)
        fac = jnp.exp2(-jnp.abs(d))
        ql = jnp.where(later, q * fac, 0.0).astype(BF16)
        kl = jnp.where(later, 0.0, k * fac).astype(BF16)
        lhs = jnp.concatenate([ql[0:hb], ql[hb:t]], axis=1)
        rhs = jnp.concatenate([jnp.concatenate([kl[0:hb], zeros], axis=1),
                               jnp.concatenate([zeros, kl[hb:t]], axis=1)], axis=0)
        diag = jnp.where(lvl_d == level, _dot_nt(lhs, rhs), diag)
    b_mid = b[hb - 1:hb, :]
    q_top = (q[hb:t] * jnp.exp2(b[hb:t] - b_mid)).astype(BF16)
    k_top = (k[0:hb] * jnp.exp2(b_mid - b[0:hb])).astype(BF16)
    top = _dot_nt(q_top, k_top)
    scores = jnp.concatenate(
        [jnp.concatenate([diag[:, 0:hb], jnp.zeros((hb, hb), F32)], axis=1),
         jnp.concatenate([top, diag[:, hb:t]], axis=1)], axis=0)

    st = st_ref[h] * keep
    b_last = b[t - 1:t, :]
    o = _dot_nt((q * jnp.exp2(b)).astype(BF16), st.astype(BF16))
    o = o + jnp.sum(q * k, axis=1, keepdims=True) * v.astype(F32)
    k_dec = (k * jnp.exp2(b_last - b)).astype(BF16)
    st_ref[h] = jnp.exp2(b_last) * st + _dot_tn(v, k_dec)
    return scores.astype(BF16), o


def _hgrn2_head_output(scores, o, v, gr, ng):
    return _rms(o + _dot(scores, v), ng) * (gr / (1.0 + jnp.exp(-gr)))


def _proj_hgrn2_kernel(*refs, tiles_per_seq, n_cast):
    x0_ref, xnext_ref, g_ref, w_hbm, lbl_ref, ng_ref, lvl_ref = refs[:7]
    cast_in = refs[7:7 + n_cast]
    q_ref, k_ref, vt_ref, gate_ref, ohg_ref = refs[7 + n_cast:12 + n_cast]
    cast_out = refs[12 + n_cast:12 + 2 * n_cast]
    xn_buf, f_buf, iqg_buf, st_ref, w_ref, w_stage, w_sems = refs[12 + 2 * n_cast:]
    d = xnext_ref.shape[1]
    t = HG_BLOCK
    n_blocks = xnext_ref.shape[0] // t
    dk = HG_HEAD_DIM
    hg_w = HG_HEADS * dk
    sb_w = q_ref.shape[1]
    i = pl.program_id(0)
    n_steps = pl.num_programs(0)
    slot = i % 2
    prev = 1 - slot

    def step(do_proj, do_hgrn2, fetch=None):
        row = lax.broadcasted_iota(jnp.int32, (t, t), 0)
        col = lax.broadcasted_iota(jnp.int32, (t, t), 1)
        tril = (row >= col).astype(BF16)
        pos = lax.broadcasted_iota(jnp.int32, (t, dk), 0)
        lvl = lvl_ref[...]
        first_keep = jnp.where(i % tiles_per_seq == 1, 0.0, 1.0)

        lbl = lbl_ref[...]
        ex = jnp.exp(lbl - jnp.max(lbl, axis=0, keepdims=True))
        lb_all = ex[0:1, :] / jnp.sum(ex, axis=0, keepdims=True)

        def prepare_block(blk):
            rows = slice(blk * t, (blk + 1) * t)
            logf2, k_all, q_all = _hgrn2_gates(
                f_buf[prev, rows, :], iqg_buf[prev, rows, hg_w:2 * hg_w].astype(F32), lb_all)
            g1, g2, g3 = _split3(logf2)
            return _dot(tril, g1) + _dot(tril, g2) + _dot(tril, g3), q_all, k_all

        def head_products(blk, h, prepared):
            b_all, q_all, k_all = prepared
            rows = slice(blk * t, (blk + 1) * t)
            sl = slice(h * dk, (h + 1) * dk)
            return _hgrn2_head_products(b_all[:, sl], q_all[:, sl], k_all[:, sl],
                                        iqg_buf[prev, rows, sl], lvl, pos, st_ref, h,
                                        first_keep if blk == 0 else 1.0)

        def head_output(blk, h, products):
            rows = slice(blk * t, (blk + 1) * t)
            sl = slice(h * dk, (h + 1) * dk)
            gr = iqg_buf[prev, rows, 2 * hg_w + h * dk:2 * hg_w + (h + 1) * dk].astype(F32)
            y = _hgrn2_head_output(*products, iqg_buf[prev, rows, sl], gr, ng_ref[:, sl])
            ohg_ref[rows, sl] = y.astype(ohg_ref.dtype)

        c0 = 3 * sb_w
        c1 = c0 + hg_w
        c2 = c1 + 3 * hg_w
        slab = d // 2
        slabs = []
        if do_proj:
            xn = xn_buf[slot]

            def project(lo, hi):
                if fetch is not None:
                    fetch(lo, hi)
                return _dot(xn, w_ref[:, lo:hi])

            def q_slab():
                q_ref[...] = (project(0, sb_w) * (SB_HEAD_DIM ** -0.5 * LOG2E)).astype(q_ref.dtype)

            def k_slab():
                k_ref[...] = project(sb_w, 2 * sb_w).astype(k_ref.dtype)

            def v_slab():
                v = project(2 * sb_w, 3 * sb_w)
                for blk in range(n_blocks):
                    vt_ref[0, :, blk * t:(blk + 1) * t] = \
                        v[blk * t:(blk + 1) * t].T.astype(vt_ref.dtype)

            def f_slab():
                f_buf[slot] = project(c0, c1)

            def iqg_slab(lo):
                def run():
                    iqg_buf[slot, :, lo:lo + slab] = \
                        project(c1 + lo, c1 + lo + slab).astype(iqg_buf.dtype)
                return run

            def gate_slab(lo):
                def run():
                    gate_ref[:, lo:lo + slab] = project(c2 + lo, c2 + lo + slab)
                return run

            for src, dst in zip(cast_in, cast_out):
                dst[...] = src[...].astype(dst.dtype)
            slabs = [v_slab, f_slab] + [iqg_slab(lo) for lo in range(0, 3 * hg_w, slab)] \
                + [gate_slab(lo) for lo in range(0, 2 * d, slab)]
            q_slab()
            k_slab()
        if do_hgrn2:
            prepared = prepare_block(0)
            if slabs:
                slabs.pop(0)()
            for blk in range(n_blocks):
                for h in range(HG_HEADS):
                    if blk > 0 and h == 0:
                        prepared = prepare_block(blk)
                    products = head_products(blk, h, prepared)
                    if slabs:
                        slabs.pop(0)()
                    head_output(blk, h, products)
        for run in slabs:
            run()
        if do_proj:
            xn_buf[prev] = _rms(xnext_ref[...], g_ref[...]).astype(BF16)

    @pl.when(i == 0)
    def _():
        st_ref[...] = jnp.zeros_like(st_ref)
        xn_buf[0] = _rms(x0_ref[...], g_ref[...]).astype(BF16)
        width = w_stage.shape[2]
        n_slabs = w_ref.shape[1] // width
        ready = [0]

        def slab_copy(n):
            return pltpu.make_async_copy(w_hbm.at[:, pl.ds(n * width, width)],
                                         w_stage.at[n % 2], w_sems.at[n % 2])

        def fetch(lo, hi):
            assert lo <= ready[0] * width, "projection slabs must be taken in column order"
            while ready[0] * width < hi:
                n = ready[0]
                if n + 1 < n_slabs:
                    slab_copy(n + 1).start()
                slab_copy(n).wait()
                w_ref[:, n * width:(n + 1) * width] = w_stage[n % 2].astype(w_ref.dtype)
                ready[0] = n + 1

        slab_copy(0).start()
        step(True, False, fetch)
        assert ready[0] == n_slabs

    @pl.when(jnp.logical_and(i > 0, i < n_steps - 1))
    def _():
        step(True, True)

    @pl.when(i == n_steps - 1)
    def _():
        step(False, True)


def _pair_levels(t):
    idx = np.arange(t)
    x = idx[:, None] ^ idx[None, :]
    lev = np.where(x > 0, np.floor(np.log2(np.maximum(x, 1))).astype(np.int64) + 1, 0)
    return np.where(idx[:, None] > idx[None, :], lev, 0).astype(np.int32)


def _proj_hgrn2(x2, g1, w_in, lb_logits, ng, later_weights, batch, sb_w, hg_w):
    n, d = x2.shape
    t = PROJ_BLOCKS * HG_BLOCK
    cols = w_in.shape[1]
    n_tiles = n // t
    tiles_per_seq = n_tiles // batch
    assert hg_w == HG_HEADS * HG_HEAD_DIM and n_tiles * t == n and tiles_per_seq * batch == n_tiles
    lvl = jnp.asarray(_pair_levels(HG_BLOCK))
    cur = lambda i: jnp.minimum(i, n_tiles - 1)
    rows = lambda w: pl.BlockSpec((t, w), lambda i: (cur(i), 0))
    vt_spec = pl.BlockSpec((1, sb_w, t),
                           lambda i: (cur(i) // tiles_per_seq, 0, cur(i) % tiles_per_seq))
    sds = lambda w, dt: jax.ShapeDtypeStruct((n, w), dt)
    cast_specs = [pl.BlockSpec((w.shape[0] // n_tiles, w.shape[1]), lambda i: (cur(i), 0))
                  for w in later_weights]
    assert all(w.shape[0] % (n_tiles * 16) == 0 for w in later_weights)
    cast_bytes = sum(w.size // n_tiles * 6 for w in later_weights)
    moving = t * (d * 4 + 2 * sb_w * 2 + sb_w * 2 + 2 * d * 4 + hg_w * 2) + cast_bytes
    stage_cols = d // 2
    assert cols % stage_cols == 0
    resident = d * 4 + d * cols * 2 + 2 * d * stage_cols * 4 + 3 * hg_w * 4 \
        + HG_BLOCK * HG_BLOCK * 4 + t * d * 4 \
        + 2 * t * d * 2 + 2 * t * hg_w * (4 + 3 * 2) + HG_HEADS * HG_HEAD_DIM * HG_HEAD_DIM * 4
    return pl.pallas_call(
        functools.partial(_proj_hgrn2_kernel, tiles_per_seq=tiles_per_seq,
                          n_cast=len(later_weights)),
        grid=(n_tiles + 1,),
        in_specs=[_resident((t, d)), pl.BlockSpec((t, d), lambda i: (cur(i + 1), 0)),
                  _resident((1, d)), pl.BlockSpec(memory_space=pl.ANY),
                  _resident(lb_logits.shape), _resident(ng.shape),
                  _resident((HG_BLOCK, HG_BLOCK))] + cast_specs,
        out_specs=[rows(sb_w), rows(sb_w), vt_spec, rows(2 * d),
                   pl.BlockSpec((t, hg_w), lambda i: (jnp.maximum(i - 1, 0), 0))] + cast_specs,
        out_shape=[sds(sb_w, BF16), sds(sb_w, BF16),
                   jax.ShapeDtypeStruct((batch, sb_w, n // batch), BF16),
                   sds(2 * d, F32), sds(hg_w, BF16)]
        + [jax.ShapeDtypeStruct(w.shape, BF16) for w in later_weights],
        scratch_shapes=[pltpu.VMEM((2, t, d), BF16),
                        pltpu.VMEM((2, t, hg_w), F32),
                        pltpu.VMEM((2, t, 3 * hg_w), BF16),
                        pltpu.VMEM((HG_HEADS, HG_HEAD_DIM, HG_HEAD_DIM), F32),
                        pltpu.VMEM((d, cols), BF16),
                        pltpu.VMEM((2, d, stage_cols), F32),
                        pltpu.SemaphoreType.DMA((2,))],
        compiler_params=pltpu.CompilerParams(
            dimension_semantics=("arbitrary",),
            vmem_limit_bytes=_vmem_limit(moving, resident,
                                         t * (d * 6 + cols * 4) + 64 * HG_BLOCK * HG_BLOCK * 4)),
        name="proj_hgrn2",
    )(x2, x2, g1, w_in, lb_logits, ng, lvl, *later_weights)


def _attn_ffn_kernel(q_ref, kd_ref, kp_ref, vtd_ref, vtp_ref, k_hbm, vt_hbm, x_ref, ohg_ref,
                     gate_ref, bg_ref, g2_ref, gf_ref, wsb_hbm, whg_hbm, wout_hbm, w1_hbm, w2_hbm,
                     o_ref, osb_buf, qm_ref, acc_ref, carry_ref, k_buf, vt_buf, sems,
                     wsb_ref, whg_ref, wout_ref, w1_ref, w2_ref, w_sems,
                     *, tiles_per_seq, n_tiles):
    t = q_ref.shape[1]
    d = x_ref.shape[1]
    heads = range(SB_HEADS)
    i = pl.program_id(0)
    slot = i % 2
    qi = jnp.minimum(i, n_tiles - 1) % tiles_per_seq

    row = lax.broadcasted_iota(jnp.int32, (t, t), 0)
    col = lax.broadcasted_iota(jnp.int32, (t, t), 1)
    tri = (col > row).astype(BF16)
    causal = row < col

    def logits(k):
        return [_dot_nt(k[:, (h // 2) * V7X_LANES:(h // 2 + 1) * V7X_LANES], qm_ref[h])
                for h in heads]

    def softplus_phase(z, mask):
        sp = [_softplus2(z[h], mask) for h in heads]
        return sp, [sp[h].astype(BF16) for h in heads]

    def cumsum_phase(spb):
        return [_dot(tri, spb[h]) for h in heads]

    def weight_phase(z, sp, later, mask):
        w = [jnp.exp2(z[h] - sp[h] - later[h]) for h in heads]
        if mask is not None:
            w = [jnp.where(mask, w[h], 0.0) for h in heads]
        return [w[h].astype(BF16) for h in heads]

    def value_phase(vt, w, later, spb):
        pv = [_dot(vt[h * SB_HEAD_DIM:(h + 1) * SB_HEAD_DIM, :], w[h]) for h in heads]
        return pv, [later[h][0:1, :] + spb[h][0:1, :].astype(F32) for h in heads]

    def main_block(do_attn, do_ffn):
        has_prev = qi > 0
        if do_attn:
            lane = lax.broadcasted_iota(jnp.int32, (t, V7X_LANES), 1)
            zero = jnp.zeros((), BF16)
            q = q_ref[0]
            for h in heads:
                grp = q[:, (h // 2) * V7X_LANES:(h // 2 + 1) * V7X_LANES]
                qm_ref[h] = jnp.where((lane // SB_HEAD_DIM) == (h % 2), grp, zero)
        if do_ffn:
            a_sb = _dot_tn(osb_buf[1 - slot], wsb_ref[...])
            a_hg = _dot(ohg_ref[...], whg_ref[...])
        if do_attn:
            z0 = logits(kd_ref[0])
        if do_ffn:
            gates = 1.0 / (1.0 + jnp.exp(-(gate_ref[...] + bg_ref[...])))
            merged = (gates[:, :d] * a_sb + gates[:, d:] * a_hg).astype(BF16)
            hres = x_ref[...] + _dot(merged, wout_ref[...])
        if do_attn:
            z1 = logits(kp_ref[0])
        if do_ffn:
            hn = _rms(hres, g2_ref[...]).astype(BF16)
            half = w1_ref.shape[1] // 2

            def mlp_up(lo):
                act = jnp.maximum(_dot(hn, w1_ref[:, lo:lo + half]), 0.0)
                return (act * act).astype(BF16)

            act_a = mlp_up(0)
        if do_attn:
            sp0, spb0 = softplus_phase(z0, causal)
            lat0 = cumsum_phase(spb0)
        if do_ffn:
            act_b = mlp_up(half)
        if do_attn:
            sp1, spb1 = softplus_phase(z1, None)
            lat1 = cumsum_phase(spb1)
        if do_ffn:
            hres = hres + _dot(act_a, w2_ref[0:half, :])
        if do_attn:
            w0 = weight_phase(z0, sp0, lat0, causal)
            w1 = weight_phase(z1, sp1, lat1, None)
        if do_ffn:
            hres = hres + _dot(act_b, w2_ref[half:2 * half, :])
        if do_attn:
            pv0, tot0 = value_phase(vtd_ref.at[0], w0, lat0, spb0)
            pv1, tot1 = value_phase(vtp_ref.at[0], w1, lat1, spb1)
        if do_ffn:
            o_ref[...] = _rms(hres, gf_ref[...]).astype(o_ref.dtype)
        if do_attn:
            for h in heads:
                scale = jnp.where(has_prev, jnp.exp2(-tot0[h]), 0.0)
                acc_ref[h] = pv0[h] + scale * pv1[h]
                carry_ref[h:h + 1, :] = tot0[h] + jnp.where(has_prev, tot1[h], 0.0)

    def sweep_rest():
        def more(state):
            n, live = state
            return jnp.logical_and(n < qi, live)

        seq = jnp.minimum(i, n_tiles - 1) // tiles_per_seq

        def tile_copies(j):
            start = pl.multiple_of(j * t, t)
            return (pltpu.make_async_copy(k_hbm.at[seq, pl.ds(start, t), :], k_buf, sems.at[0]),
                    pltpu.make_async_copy(vt_hbm.at[seq, :, pl.ds(start, t)], vt_buf, sems.at[1]))

        def body(state):
            n, _ = state
            copies = tile_copies(qi - 1 - n)
            for c in copies:
                c.start()
            for c in copies:
                c.wait()
            z = logits(k_buf[...])
            sp, spb = softplus_phase(z, None)
            later = cumsum_phase(spb)
            pv, tot = value_phase(vt_buf, weight_phase(z, sp, later, None), later, spb)
            for h in heads:
                c = carry_ref[h:h + 1, :]
                acc_ref[h] += jnp.exp2(-c) * pv[h]
                carry_ref[h:h + 1, :] = c + tot[h]
            return n + 1, jnp.min(carry_ref[...]) < SB_DEAD_CARRY

        lax.while_loop(more, body, (jnp.int32(1), jnp.min(carry_ref[...]) < SB_DEAD_CARRY))
        osb_buf[slot] = acc_ref[...].reshape(SB_HEADS * SB_HEAD_DIM, t).astype(osb_buf.dtype)

    weight_copies = [pltpu.make_async_copy(src, dst, w_sems.at[n]) for n, (src, dst) in enumerate(
        ((wsb_hbm, wsb_ref), (whg_hbm, whg_ref), (wout_hbm, wout_ref), (w1_hbm, w1_ref),
         (w2_hbm, w2_ref)))]

    @pl.when(i == 0)
    def _():
        for c in weight_copies:
            c.start()
        main_block(True, False)
        sweep_rest()

    @pl.when(i == 1)
    def _():
        for c in weight_copies:
            c.wait()

    @pl.when(jnp.logical_and(i > 0, i < n_tiles))
    def _():
        main_block(True, True)
        sweep_rest()

    @pl.when(i == n_tiles)
    def _():
        main_block(False, True)


def _attn_ffn(q3, k3, vt3, x2, ohg, gates, bg, wsb, whg, wout, g2, w1, w2, gf):
    b, s, sb_w = q3.shape
    n, d = x2.shape
    t = ATTN_TILE
    dff = w1.shape[1]
    tiles_per_seq = s // t
    n_tiles = n // t
    assert sb_w == SB_HEADS * SB_HEAD_DIM and tiles_per_seq * t == s and n_tiles == b * tiles_per_seq
    cur = lambda i: jnp.minimum(i, n_tiles - 1)
    prv = lambda i: (jnp.maximum(i - 1, 0), 0)
    seq = lambda i: cur(i) // tiles_per_seq
    tile = lambda i: cur(i) % tiles_per_seq
    before = lambda i: jnp.maximum(tile(i) - 1, 0)
    k_tile = lambda pos: pl.BlockSpec((1, t, sb_w), lambda i: (seq(i), pos(i), 0))
    vt_tile = lambda pos: pl.BlockSpec((1, sb_w, t), lambda i: (seq(i), 0, pos(i)))
    in_hbm = pl.BlockSpec(memory_space=pl.ANY)
    full = lambda a: _resident(a.shape)
    weights = (wsb, whg, wout, w1, w2)
    resident_bytes = sum(a.size * a.dtype.itemsize for a in (bg, g2, gf) + weights) \
        + 2 * t * sb_w * 2 + 2 * t * sb_w * 2 + SB_HEADS * t * (V7X_LANES * 2 + SB_HEAD_DIM * 4 + 4)
    moving = t * (5 * sb_w * 2 + d * 4 + ohg.shape[1] * 2 + 2 * d * 4 + d * 4)
    return pl.pallas_call(
        functools.partial(_attn_ffn_kernel, tiles_per_seq=tiles_per_seq, n_tiles=n_tiles),
        grid=(n_tiles + 1,),
        in_specs=[k_tile(tile), k_tile(tile), k_tile(before), vt_tile(tile), vt_tile(before),
                  in_hbm, in_hbm,
                  pl.BlockSpec((t, d), prv),
                  pl.BlockSpec((t, ohg.shape[1]), prv),
                  pl.BlockSpec((t, 2 * d), prv),
                  full(bg), full(g2), full(gf)] + [in_hbm] * len(weights),
        out_specs=pl.BlockSpec((t, d), prv),
        out_shape=jax.ShapeDtypeStruct((n, d), x2.dtype),
        scratch_shapes=[pltpu.VMEM((2, sb_w, t), BF16),
                        pltpu.VMEM((SB_HEADS, t, V7X_LANES), BF16),
                        pltpu.VMEM((SB_HEADS, SB_HEAD_DIM, t), F32),
                        pltpu.VMEM((SB_HEADS, t), F32),
                        pltpu.VMEM((t, sb_w), BF16),
                        pltpu.VMEM((sb_w, t), BF16),
                        pltpu.SemaphoreType.DMA((2,))]
        + [pltpu.VMEM(a.shape, a.dtype) for a in weights]
        + [pltpu.SemaphoreType.DMA((len(weights),))],
        compiler_params=pltpu.CompilerParams(
            dimension_semantics=("arbitrary",),
            vmem_limit_bytes=_vmem_limit(moving, resident_bytes,
                                         t * (dff * 6 + d * 24) + 8 * SB_HEADS * t * t * 4)),
        name="attn_ffn",
    )(q3, k3, k3, vt3, vt3, k3, vt3, x2, ohg, gates, bg, g2, gf, *weights)


def kernel(x, norm1_g, w_in, b_gate, lb_logits, hg_norm_g, w_o_sb, w_o_hg, w_out, norm2_g,
           w_ff1, w_ff2, final_g):
    b, s, d = x.shape
    assert w_in.shape[0] == 1, "single-layer block"
    sb_w = SB_HEADS * SB_HEAD_DIM
    hg_w = HG_HEADS * HG_HEAD_DIM
    x2 = x.reshape(b * s, d)
    later = (w_o_sb[0], w_o_hg[0], w_out[0], w_ff1[0], w_ff2[0])
    q, k, vt, gates, o_hg, wsb, whg, wout, w1, w2 = _proj_hgrn2(
        x2, norm1_g, w_in[0], lb_logits, hg_norm_g, later, b, sb_w, hg_w)
    out = _attn_ffn(q.reshape(b, s, sb_w), k.reshape(b, s, sb_w), vt, x2, o_hg, gates, b_gate,
                    wsb, whg, wout, norm2_g, w1, w2, final_g.reshape(1, d))
    return out.reshape(b, s, d)
```

```python
import functools

import jax
import jax.numpy as jnp
import numpy as np
from jax import lax
from jax.experimental import pallas as pl
from jax.experimental.pallas import tpu as pltpu

F32 = jnp.float32
BF16 = jnp.bfloat16

SB_HEADS = 8
SB_HEAD_DIM = 64
HG_HEADS = 4
HG_HEAD_DIM = 128
EPS = 1e-6
LOG2E = 1.4426950408889634
SB_DEAD_CARRY = 151.0

V7X_LANES = 128
V7X_MXU_DIM = 256
V7X_VMEM_BYTES = 64 * 1024 * 1024

ATTN_TILE = V7X_MXU_DIM
HG_BLOCK = V7X_MXU_DIM
PROJ_BLOCKS = 2


def _vmem_limit(pipelined_bytes, resident_bytes, temp_bytes):
    need = 2 * pipelined_bytes + resident_bytes + temp_bytes
    return int(min(need + need // 4, V7X_VMEM_BYTES - 8 * 1024 * 1024))


def _resident(shape):
    return pl.BlockSpec(shape, lambda *_: (0,) * len(shape), pipeline_mode=pl.Buffered(1))


def _rms(x, g):
    ms = jnp.mean(x * x, axis=-1, keepdims=True)
    return x * lax.rsqrt(ms + EPS) * g


def _dot(a, b):
    return jnp.dot(a, b, preferred_element_type=F32)


def _dot_nt(a, b):
    return lax.dot_general(a, b, (((1,), (1,)), ((), ())), preferred_element_type=F32)


def _dot_tn(a, b):
    return lax.dot_general(a, b, (((0,), (0,)), ((), ())), preferred_element_type=F32)


def _softplus2(z, mask):
    sp = jnp.maximum(z, 0.0) + jnp.log(1.0 + jnp.exp2(-jnp.abs(z))) * LOG2E
    return sp if mask is None else jnp.where(mask, sp, 0.0)


def _split3(x):
    a = x.astype(BF16)
    r = x - a.astype(F32)
    b = r.astype(BF16)
    c = (r - b.astype(F32)).astype(BF16)
    return a, b, c


def _rows_from_group(b, group, r):
    n, c = b.shape
    if group == n:
        return jnp.broadcast_to(b[r:r + 1, :], (n, c))
    b3 = b.reshape(n // group, group, c)
    return jnp.broadcast_to(b3[:, r:r + 1, :], b3.shape).reshape(n, c)


def _midpoint_rows(b, group, pos):
    n = b.shape[0]
    half = group // 2
    if group >= 16:
        return _rows_from_group(b, group, half - 1)
    if group == 8:
        return _rows_from_group(b, 8, 3)
    up1 = pltpu.roll(b, n - 1, 0)
    dn1 = pltpu.roll(b, 1, 0)
    if group == 2:
        return jnp.where((pos & 1) == 0, b, dn1)
    assert group == 4
    dn2 = pltpu.roll(b, 2, 0)
    r4 = pos & 3
    return jnp.where(r4 == 0, up1, jnp.where(r4 == 1, b, jnp.where(r4 == 2, dn1, dn2)))


def _hgrn2_gates(fr, qr, lb):
    e = jnp.exp(-jnp.abs(fr))
    r = 1.0 / (1.0 + e)
    er = e * r
    sig = jnp.where(fr >= 0, r, er)
    nsig = jnp.where(fr >= 0, er, r)
    logf2 = jnp.log(lb + (1.0 - lb) * sig) * LOG2E
    k = (1.0 - lb) * nsig
    q = qr / (1.0 + jnp.exp(-qr))
    return logf2, k, q


def _hgrn2_head_products(b, q, k, v, lvl, pos, st_ref, h, keep):
    t, dk = b.shape
    n_levels = t.bit_length() - 1
    hb = t // 2
    zeros = jnp.zeros((hb, dk), BF16)
    lvl_d = jnp.concatenate([lvl[0:hb, 0:hb], lvl[hb:t, hb:t]], axis=1)
    diag = jnp.zeros((hb, t), F32)
    for level in range(1, n_levels):
        group = 1 << level
        d = b - _midpoint_rows(b, group, pos)
        later = (pos & (group - 1)) >= (group // 2)
        fac = jnp.exp2(-jnp.abs(d))
        ql = jnp.where(later, q * fac, 0.0).astype(BF16)
        kl = jnp.where(later, 0.0, k * fac).astype(BF16)
        lhs = jnp.concatenate([ql[0:hb], ql[hb:t]], axis=1)
        rhs = jnp.concatenate([jnp.concatenate([kl[0:hb], zeros], axis=1),
                               jnp.concatenate([zeros, kl[hb:t]], axis=1)], axis=0)
        diag = jnp.where(lvl_d == level, _dot_nt(lhs, rhs), diag)
    b_mid = b[hb - 1:hb, :]
    q_top = (q[hb:t] * jnp.exp2(b[hb:t] - b_mid)).astype(BF16)
    k_top = (k[0:hb] * jnp.exp2(b_mid - b[0:hb])).astype(BF16)
    top = _dot_nt(q_top, k_top)
    scores = jnp.concatenate(
        [jnp.concatenate([diag[:, 0:hb], jnp.zeros((hb, hb), F32)], axis=1),
         jnp.concatenate([top, diag[:, hb:t]], axis=1)], axis=0)

    st = st_ref[h] * keep
    b_last = b[t - 1:t, :]
    o = _dot_nt((q * jnp.exp2(b)).astype(BF16), st.astype(BF16))
    o = o + jnp.sum(q * k, axis=1, keepdims=True) * v.astype(F32)
    k_dec = (k * jnp.exp2(b_last - b)).astype(BF16)
    st_ref[h] = jnp.exp2(b_last) * st + _dot_tn(v, k_dec)
    return scores.astype(BF16), o


def _hgrn2_head_output(scores, o, v, gr, ng):
    return _rms(o + _dot(scores, v), ng) * (gr / (1.0 + jnp.exp(-gr)))


def _proj_hgrn2_kernel(*refs, tiles_per_seq, n_cast):
    x0_ref, xnext_ref, g_ref, w_hbm, lbl_ref, ng_ref, lvl_ref = refs[:7]
    cast_in = refs[7:7 + n_cast]
    q_ref, k_ref, vt_ref, gate_ref, ohg_ref = refs[7 + n_cast:12 + n_cast]
    cast_out = refs[12 + n_cast:12 + 2 * n_cast]
    xn_buf, f_buf, iqg_buf, st_ref, w_ref, w_stage, w_sems = refs[12 + 2 * n_cast:]
    d = xnext_ref.shape[1]
    t = HG_BLOCK
    n_blocks = xnext_ref.shape[0] // t
    dk = HG_HEAD_DIM
    hg_w = HG_HEADS * dk
    sb_w = q_ref.shape[1]
    i = pl.program_id(0)
    n_steps = pl.num_programs(0)
    slot = i % 2
    prev = 1 - slot

    def step(do_proj, do_hgrn2, fetch=None):
        row = lax.broadcasted_iota(jnp.int32, (t, t), 0)
        col = lax.broadcasted_iota(jnp.int32, (t, t), 1)
        tril = (row >= col).astype(BF16)
        pos = lax.broadcasted_iota(jnp.int32, (t, dk), 0)
        lvl = lvl_ref[...]
        first_keep = jnp.where(i % tiles_per_seq == 1, 0.0, 1.0)

        lbl = lbl_ref[...]
        ex = jnp.exp(lbl - jnp.max(lbl, axis=0, keepdims=True))
        lb_all = ex[0:1, :] / jnp.sum(ex, axis=0, keepdims=True)

        def prepare_block(blk):
            rows = slice(blk * t, (blk + 1) * t)
            logf2, k_all, q_all = _hgrn2_gates(
                f_buf[prev, rows, :], iqg_buf[prev, rows, hg_w:2 * hg_w].astype(F32), lb_all)
            g1, g2, g3 = _split3(logf2)
            return _dot(tril, g1) + _dot(tril, g2) + _dot(tril, g3), q_all, k_all

        def head_products(blk, h, prepared):
            b_all, q_all, k_all = prepared
            rows = slice(blk * t, (blk + 1) * t)
            sl = slice(h * dk, (h + 1) * dk)
            return _hgrn2_head_products(b_all[:, sl], q_all[:, sl], k_all[:, sl],
                                        iqg_buf[prev, rows, sl], lvl, pos, st_ref, h,
                                        first_keep if blk == 0 else 1.0)

        def head_output(blk, h, products):
            rows = slice(blk * t, (blk + 1) * t)
            sl = slice(h * dk, (h + 1) * dk)
            gr = iqg_buf[prev, rows, 2 * hg_w + h * dk:2 * hg_w + (h + 1) * dk].astype(F32)
            y = _hgrn2_head_output(*products, iqg_buf[prev, rows, sl], gr, ng_ref[:, sl])
            ohg_ref[rows, sl] = y.astype(ohg_ref.dtype)

        c0 = 3 * sb_w
        c1 = c0 + hg_w
        c2 = c1 + 3 * hg_w
        slab = d // 2
        slabs = []
        if do_proj:
            xn = xn_buf[slot]

            def project(lo, hi):
                if fetch is not None:
                    fetch(lo, hi)
                return _dot(xn, w_ref[:, lo:hi])

            def q_slab():
                q_ref[...] = (project(0, sb_w) * (SB_HEAD_DIM ** -0.5 * LOG2E)).astype(q_ref.dtype)

            def k_slab():
                k_ref[...] = project(sb_w, 2 * sb_w).astype(k_ref.dtype)

            def v_slab():
                v = project(2 * sb_w, 3 * sb_w)
                for blk in range(n_blocks):
                    vt_ref[0, :, blk * t:(blk + 1) * t] = \
                        v[blk * t:(blk + 1) * t].T.astype(vt_ref.dtype)

            def f_slab():
                f_buf[slot] = project(c0, c1)

            def iqg_slab(lo):
                def run():
                    iqg_buf[slot, :, lo:lo + slab] = \
                        project(c1 + lo, c1 + lo + slab).astype(iqg_buf.dtype)
                return run

            def gate_slab(lo):
                def run():
                    gate_ref[:, lo:lo + slab] = project(c2 + lo, c2 + lo + slab)
                return run

            for src, dst in zip(cast_in, cast_out):
                dst[...] = src[...].astype(dst.dtype)
            slabs = [v_slab, f_slab] + [iqg_slab(lo) for lo in range(0, 3 * hg_w, slab)] \
                + [gate_slab(lo) for lo in range(0, 2 * d, slab)]
            q_slab()
            k_slab()
        if do_hgrn2:
            prepared = prepare_block(0)
            if slabs:
                slabs.pop(0)()
            for blk in range(n_blocks):
                for h in range(HG_HEADS):
                    if blk > 0 and h == 0:
                        prepared = prepare_block(blk)
                    products = head_products(blk, h, prepared)
                    if slabs:
                        slabs.pop(0)()
                    head_output(blk, h, products)
        for run in slabs:
            run()
        if do_proj:
            xn_buf[prev] = _rms(xnext_ref[...], g_ref[...]).astype(BF16)

    @pl.when(i == 0)
    def _():
        st_ref[...] = jnp.zeros_like(st_ref)
        xn_buf[0] = _rms(x0_ref[...], g_ref[...]).astype(BF16)
        width = w_stage.shape[2]
        n_slabs = w_ref.shape[1] // width
        ready = [0]

        def slab_copy(n):
            return pltpu.make_async_copy(w_hbm.at[:, pl.ds(n * width, width)],
                                         w_stage.at[n % 2], w_sems.at[n % 2])

        def fetch(lo, hi):
            assert lo <= ready[0] * width, "projection slabs must be taken in column order"
            while ready[0] * width < hi:
                n = ready[0]
                if n + 1 < n_slabs:
                    slab_copy(n + 1).start()
                slab_copy(n).wait()
                w_ref[:, n * width:(n + 1) * width] = w_stage[n % 2].astype(w_ref.dtype)
                ready[0] = n + 1

        slab_copy(0).start()
        step(True, False, fetch)
        assert ready[0] == n_slabs

    @pl.when(jnp.logical_and(i > 0, i < n_steps - 1))
    def _():
        step(True, True)

    @pl.when(i == n_steps - 1)
    def _():
        step(False, True)


def _pair_levels(t):
    idx = np.arange(t)
    x = idx[:, None] ^ idx[None, :]
    lev = np.where(x > 0, np.floor(np.log2(np.maximum(x, 1))).astype(np.int64) + 1, 0)
    return np.where(idx[:, None] > idx[None, :], lev, 0).astype(np.int32)


def _proj_hgrn2(x2, g1, w_in, lb_logits, ng, later_weights, batch, sb_w, hg_w):
    n, d = x2.shape
    t = PROJ_BLOCKS * HG_BLOCK
    cols = w_in.shape[1]
    n_tiles = n // t
    tiles_per_seq = n_tiles // batch
    assert hg_w == HG_HEADS * HG_HEAD_DIM and n_tiles * t == n and tiles_per_seq * batch == n_tiles
    lvl = jnp.asarray(_pair_levels(HG_BLOCK))
    cur = lambda i: jnp.minimum(i, n_tiles - 1)
    rows = lambda w: pl.BlockSpec((t, w), lambda i: (cur(i), 0))
    vt_spec = pl.BlockSpec((1, sb_w, t),
                           lambda i: (cur(i) // tiles_per_seq, 0, cur(i) % tiles_per_seq))
    sds = lambda w, dt: jax.ShapeDtypeStruct((n, w), dt)
    cast_specs = [pl.BlockSpec((w.shape[0] // n_tiles, w.shape[1]), lambda i: (cur(i), 0))
                  for w in later_weights]
    assert all(w.shape[0] % (n_tiles * 16) == 0 for w in later_weights)
    cast_bytes = sum(w.size // n_tiles * 6 for w in later_weights)
    moving = t * (d * 4 + 2 * sb_w * 2 + sb_w * 2 + 2 * d * 4 + hg_w * 2) + cast_bytes
    stage_cols = d // 2
    assert cols % stage_cols == 0
    resident = d * 4 + d * cols * 2 + 2 * d * stage_cols * 4 + 3 * hg_w * 4 \
        + HG_BLOCK * HG_BLOCK * 4 + t * d * 4 \
        + 2 * t * d * 2 + 2 * t * hg_w * (4 + 3 * 2) + HG_HEADS * HG_HEAD_DIM * HG_HEAD_DIM * 4
    return pl.pallas_call(
        functools.partial(_proj_hgrn2_kernel, tiles_per_seq=tiles_per_seq,
                          n_cast=len(later_weights)),
        grid=(n_tiles + 1,),
        in_specs=[_resident((t, d)), pl.BlockSpec((t, d), lambda i: (cur(i + 1), 0)),
                  _resident((1, d)), pl.BlockSpec(memory_space=pl.ANY),
                  _resident(lb_logits.shape), _resident(ng.shape),
                  _resident((HG_BLOCK, HG_BLOCK))] + cast_specs,
        out_specs=[rows(sb_w), rows(sb_w), vt_spec, rows(2 * d),
                   pl.BlockSpec((t, hg_w), lambda i: (jnp.maximum(i - 1, 0), 0))] + cast_specs,
        out_shape=[sds(sb_w, BF16), sds(sb_w, BF16),
                   jax.ShapeDtypeStruct((batch, sb_w, n // batch), BF16),
                   sds(2 * d, F32), sds(hg_w, BF16)]
        + [jax.ShapeDtypeStruct(w.shape, BF16) for w in later_weights],
        scratch_shapes=[pltpu.VMEM((2, t, d), BF16),
                        pltpu.VMEM((2, t, hg_w), F32),
                        pltpu.VMEM((2, t, 3 * hg_w), BF16),
                        pltpu.VMEM((HG_HEADS, HG_HEAD_DIM, HG_HEAD_DIM), F32),
                        pltpu.VMEM((d, cols), BF16),
                        pltpu.VMEM((2, d, stage_cols), F32),
                        pltpu.SemaphoreType.DMA((2,))],
        compiler_params=pltpu.CompilerParams(
            dimension_semantics=("arbitrary",),
            vmem_limit_bytes=_vmem_limit(moving, resident,
                                         t * (d * 6 + cols * 4) + 64 * HG_BLOCK * HG_BLOCK * 4)),
        name="proj_hgrn2",
    )(x2, x2, g1, w_in, lb_logits, ng, lvl, *later_weights)


def _attn_ffn_kernel(q_ref, kd_ref, kp_ref, vtd_ref, vtp_ref, k_hbm, vt_hbm, x_ref, ohg_ref,
                     gate_ref, bg_ref, g2_ref, gf_ref, wsb_hbm, whg_hbm, wout_hbm, w1_hbm, w2_hbm,
                     o_ref, osb_buf, qm_ref, acc_ref, carry_ref, k_buf, vt_buf, sems,
                     wsb_ref, whg_ref, wout_ref, w1_ref, w2_ref, w_sems,
                     *, tiles_per_seq, n_tiles):
    t = q_ref.shape[1]
    d = x_ref.shape[1]
    heads = range(SB_HEADS)
    i = pl.program_id(0)
    slot = i % 2
    qi = jnp.minimum(i, n_tiles - 1) % tiles_per_seq

    row = lax.broadcasted_iota(jnp.int32, (t, t), 0)
    col = lax.broadcasted_iota(jnp.int32, (t, t), 1)
    tri = (col > row).astype(BF16)
    causal = row < col

    def logits(k):
        return [_dot_nt(k[:, (h // 2) * V7X_LANES:(h // 2 + 1) * V7X_LANES], qm_ref[h])
                for h in heads]

    def softplus_phase(z, mask):
        sp = [_softplus2(z[h], mask) for h in heads]
        return sp, [sp[h].astype(BF16) for h in heads]

    def cumsum_phase(spb):
        return [_dot(tri, spb[h]) for h in heads]

    def weight_phase(z, sp, later, mask):
        w = [jnp.exp2(z[h] - sp[h] - later[h]) for h in heads]
        if mask is not None:
            w = [jnp.where(mask, w[h], 0.0) for h in heads]
        return [w[h].astype(BF16) for h in heads]

    def value_phase(vt, w, later, spb):
        pv = [_dot(vt[h * SB_HEAD_DIM:(h + 1) * SB_HEAD_DIM, :], w[h]) for h in heads]
        return pv, [later[h][0:1, :] + spb[h][0:1, :].astype(F32) for h in heads]

    def main_block(do_attn, do_ffn, with_prev=True):
        if do_attn:
            lane = lax.broadcasted_iota(jnp.int32, (t, V7X_LANES), 1)
            zero = jnp.zeros((), BF16)
            q = q_ref[0]
            for h in heads:
                grp = q[:, (h // 2) * V7X_LANES:(h // 2 + 1) * V7X_LANES]
                qm_ref[h] = jnp.where((lane // SB_HEAD_DIM) == (h % 2), grp, zero)
        if do_ffn:
            a_sb = _dot_tn(osb_buf[1 - slot], wsb_ref[...])
            a_hg = _dot(ohg_ref[...], whg_ref[...])
        if do_attn:
            z0 = logits(kd_ref[0])
        if do_ffn:
            gates = 1.0 / (1.0 + jnp.exp(-(gate_ref[...] + bg_ref[...])))
            merged = (gates[:, :d] * a_sb + gates[:, d:] * a_hg).astype(BF16)
            hres = x_ref[...] + _dot(merged, wout_ref[...])
        if do_attn and with_prev:
            z1 = logits(kp_ref[0])
        if do_ffn:
            hn = _rms(hres, g2_ref[...]).astype(BF16)
            half = w1_ref.shape[1] // 2

            def mlp_up(lo):
                act = jnp.maximum(_dot(hn, w1_ref[:, lo:lo + half]), 0.0)
                return (act * act).astype(BF16)

            act_a = mlp_up(0)
        if do_attn:
            sp0, spb0 = softplus_phase(z0, causal)
            lat0 = cumsum_phase(spb0)
        if do_ffn:
            act_b = mlp_up(half)
        if do_attn and with_prev:
            sp1, spb1 = softplus_phase(z1, None)
            lat1 = cumsum_phase(spb1)
        if do_ffn:
            hres = hres + _dot(act_a, w2_ref[0:half, :])
        if do_attn:
            w0 = weight_phase(z0, sp0, lat0, causal)
            if with_prev:
                w1 = weight_phase(z1, sp1, lat1, None)
        if do_ffn:
            hres = hres + _dot(act_b, w2_ref[half:2 * half, :])
        if do_attn:
            pv0, tot0 = value_phase(vtd_ref.at[0], w0, lat0, spb0)
            if with_prev:
                pv1, tot1 = value_phase(vtp_ref.at[0], w1, lat1, spb1)
        if do_ffn:
            o_ref[...] = _rms(hres, gf_ref[...]).astype(o_ref.dtype)
        if do_attn:
            for h in heads:
                if with_prev:
                    acc_ref[h] = pv0[h] + jnp.exp2(-tot0[h]) * pv1[h]
                    carry_ref[h:h + 1, :] = tot0[h] + tot1[h]
                else:
                    acc_ref[h] = pv0[h]
                    carry_ref[h:h + 1, :] = tot0[h]

    def sweep_rest():
        def more(state):
            n, live = state
            return jnp.logical_and(n < qi, live)

        seq = jnp.minimum(i, n_tiles - 1) // tiles_per_seq

        def tile_copies(j):
            start = pl.multiple_of(j * t, t)
            return (pltpu.make_async_copy(k_hbm.at[seq, pl.ds(start, t), :], k_buf, sems.at[0]),
                    pltpu.make_async_copy(vt_hbm.at[seq, :, pl.ds(start, t)], vt_buf, sems.at[1]))

        def body(state):
            n, _ = state
            copies = tile_copies(qi - 1 - n)
            for c in copies:
                c.start()
            for c in copies:
                c.wait()
            z = logits(k_buf[...])
            sp, spb = softplus_phase(z, None)
            later = cumsum_phase(spb)
            pv, tot = value_phase(vt_buf, weight_phase(z, sp, later, None), later, spb)
            for h in heads:
                c = carry_ref[h:h + 1, :]
                acc_ref[h] += jnp.exp2(-c) * pv[h]
                carry_ref[h:h + 1, :] = c + tot[h]
            return n + 1, jnp.min(carry_ref[...]) < SB_DEAD_CARRY

        lax.while_loop(more, body, (jnp.int32(1), jnp.min(carry_ref[...]) < SB_DEAD_CARRY))
        osb_buf[slot] = acc_ref[...].reshape(SB_HEADS * SB_HEAD_DIM, t).astype(osb_buf.dtype)

    weight_copies = [pltpu.make_async_copy(src, dst, w_sems.at[n]) for n, (src, dst) in enumerate(
        ((wsb_hbm, wsb_ref), (whg_hbm, whg_ref), (wout_hbm, wout_ref), (w1_hbm, w1_ref),
         (w2_hbm, w2_ref)))]

    @pl.when(i == 0)
    def _():
        for c in weight_copies:
            c.start()
        main_block(True, False, with_prev=False)
        sweep_rest()

    @pl.when(i == 1)
    def _():
        for c in weight_copies:
            c.wait()

    @pl.when(jnp.logical_and(jnp.logical_and(i > 0, i < n_tiles), qi > 0))
    def _():
        main_block(True, True)
        sweep_rest()

    @pl.when(jnp.logical_and(jnp.logical_and(i > 0, i < n_tiles), qi == 0))
    def _():
        main_block(True, True, with_prev=False)
        sweep_rest()

    @pl.when(i == n_tiles)
    def _():
        main_block(False, True)


def _attn_ffn(q3, k3, vt3, x2, ohg, gates, bg, wsb, whg, wout, g2, w1, w2, gf):
    b, s, sb_w = q3.shape
    n, d = x2.shape
    t = ATTN_TILE
    dff = w1.shape[1]
    tiles_per_seq = s // t
    n_tiles = n // t
    assert sb_w == SB_HEADS * SB_HEAD_DIM and tiles_per_seq * t == s and n_tiles == b * tiles_per_seq
    cur = lambda i: jnp.minimum(i, n_tiles - 1)
    prv = lambda i: (jnp.maximum(i - 1, 0), 0)
    seq = lambda i: cur(i) // tiles_per_seq
    tile = lambda i: cur(i) % tiles_per_seq
    before = lambda i: jnp.maximum(tile(i) - 1, 0)
    k_tile = lambda pos: pl.BlockSpec((1, t, sb_w), lambda i: (seq(i), pos(i), 0))
    vt_tile = lambda pos: pl.BlockSpec((1, sb_w, t), lambda i: (seq(i), 0, pos(i)))
    in_hbm = pl.BlockSpec(memory_space=pl.ANY)
    full = lambda a: _resident(a.shape)
    weights = (wsb, whg, wout, w1, w2)
    resident_bytes = sum(a.size * a.dtype.itemsize for a in (bg, g2, gf) + weights) \
        + 2 * t * sb_w * 2 + 2 * t * sb_w * 2 + SB_HEADS * t * (V7X_LANES * 2 + SB_HEAD_DIM * 4 + 4)
    moving = t * (5 * sb_w * 2 + d * 4 + ohg.shape[1] * 2 + 2 * d * 4 + d * 4)
    return pl.pallas_call(
        functools.partial(_attn_ffn_kernel, tiles_per_seq=tiles_per_seq, n_tiles=n_tiles),
        grid=(n_tiles + 1,),
        in_specs=[k_tile(tile), k_tile(tile), k_tile(before), vt_tile(tile), vt_tile(before),
                  in_hbm, in_hbm,
                  pl.BlockSpec((t, d), prv),
                  pl.BlockSpec((t, ohg.shape[1]), prv),
                  pl.BlockSpec((t, 2 * d), prv),
                  full(bg), full(g2), full(gf)] + [in_hbm] * len(weights),
        out_specs=pl.BlockSpec((t, d), prv),
        out_shape=jax.ShapeDtypeStruct((n, d), x2.dtype),
        scratch_shapes=[pltpu.VMEM((2, sb_w, t), BF16),
                        pltpu.VMEM((SB_HEADS, t, V7X_LANES), BF16),
                        pltpu.VMEM((SB_HEADS, SB_HEAD_DIM, t), F32),
                        pltpu.VMEM((SB_HEADS, t), F32),
                        pltpu.VMEM((t, sb_w), BF16),
                        pltpu.VMEM((sb_w, t), BF16),
                        pltpu.SemaphoreType.DMA((2,))]
        + [pltpu.VMEM(a.shape, a.dtype) for a in weights]
        + [pltpu.SemaphoreType.DMA((len(weights),))],
        compiler_params=pltpu.CompilerParams(
            dimension_semantics=("arbitrary",),
            vmem_limit_bytes=_vmem_limit(moving, resident_bytes,
                                         t * (dff * 6 + d * 24) + 8 * SB_HEADS * t * t * 4)),
        name="attn_ffn",
    )(q3, k3, k3, vt3, vt3, k3, vt3, x2, ohg, gates, bg, g2, gf, *weights)


def kernel(x, norm1_g, w_in, b_gate, lb_logits, hg_norm_g, w_o_sb, w_o_hg, w_out, norm2_g,
           w_ff1, w_ff2, final_g):
    b, s, d = x.shape
    assert w_in.shape[0] == 1, "single-layer block"
    sb_w = SB_HEADS * SB_HEAD_DIM
    hg_w = HG_HEADS * HG_HEAD_DIM
    x2 = x.reshape(b * s, d)
    later = (w_o_sb[0], w_o_hg[0], w_out[0], w_ff1[0], w_ff2[0])
    q, k, vt, gates, o_hg, wsb, whg, wout, w1, w2 = _proj_hgrn2(
        x2, norm1_g, w_in[0], lb_logits, hg_norm_g, later, b, sb_w, hg_w)
    out = _attn_ffn(q.reshape(b, s, sb_w), k.reshape(b, s, sb_w), vt, x2, o_hg, gates, b_gate,
                    wsb, whg, wout, norm2_g, w1, w2, final_g.reshape(1, d))
    return out.reshape(b, s, d)
```

```python
import functools

import jax
import jax.numpy as jnp
import numpy as np
from jax import lax
from jax.experimental import pallas as pl
from jax.experimental.pallas import tpu as pltpu

F32 = jnp.float32
BF16 = jnp.bfloat16

SB_HEADS = 8
SB_HEAD_DIM = 64
HG_HEADS = 4
HG_HEAD_DIM = 128
EPS = 1e-6
LOG2E = 1.4426950408889634
SB_DEAD_CARRY = 151.0

V7X_LANES = 128
V7X_MXU_DIM = 256
V7X_VMEM_BYTES = 64 * 1024 * 1024

ATTN_TILE = V7X_MXU_DIM
HG_BLOCK = V7X_MXU_DIM
PROJ_BLOCKS = 2


def _vmem_limit(pipelined_bytes, resident_bytes, temp_bytes):
    need = 2 * pipelined_bytes + resident_bytes + temp_bytes
    return int(min(need + need // 4, V7X_VMEM_BYTES - 8 * 1024 * 1024))


def _resident(shape):
    return pl.BlockSpec(shape, lambda *_: (0,) * len(shape), pipeline_mode=pl.Buffered(1))


def _rms(x, g):
    ms = jnp.mean(x * x, axis=-1, keepdims=True)
    return x * lax.rsqrt(ms + EPS) * g


def _dot(a, b):
    return jnp.dot(a, b, preferred_element_type=F32)


def _dot_nt(a, b):
    return lax.dot_general(a, b, (((1,), (1,)), ((), ())), preferred_element_type=F32)


def _dot_tn(a, b):
    return lax.dot_general(a, b, (((0,), (0,)), ((), ())), preferred_element_type=F32)


def _softplus2(z, mask):
    sp = jnp.maximum(z, 0.0) + jnp.log(1.0 + jnp.exp2(-jnp.abs(z))) * LOG2E
    return sp if mask is None else jnp.where(mask, sp, 0.0)


def _split3(x):
    a = x.astype(BF16)
    r = x - a.astype(F32)
    b = r.astype(BF16)
    c = (r - b.astype(F32)).astype(BF16)
    return a, b, c


def _rows_from_group(b, group, r):
    n, c = b.shape
    if group == n:
        return jnp.broadcast_to(b[r:r + 1, :], (n, c))
    b3 = b.reshape(n // group, group, c)
    return jnp.broadcast_to(b3[:, r:r + 1, :], b3.shape).reshape(n, c)


def _midpoint_rows(b, group, pos):
    n = b.shape[0]
    half = group // 2
    if group >= 16:
        return _rows_from_group(b, group, half - 1)
    if group == 8:
        return _rows_from_group(b, 8, 3)
    up1 = pltpu.roll(b, n - 1, 0)
    dn1 = pltpu.roll(b, 1, 0)
    if group == 2:
        return jnp.where((pos & 1) == 0, b, dn1)
    assert group == 4
    dn2 = pltpu.roll(b, 2, 0)
    r4 = pos & 3
    return jnp.where(r4 == 0, up1, jnp.where(r4 == 1, b, jnp.where(r4 == 2, dn1, dn2)))


def _hgrn2_gates(fr, qr, lb):
    e = jnp.exp(-jnp.abs(fr))
    r = 1.0 / (1.0 + e)
    er = e * r
    sig = jnp.where(fr >= 0, r, er)
    nsig = jnp.where(fr >= 0, er, r)
    logf2 = jnp.log(lb + (1.0 - lb) * sig) * LOG2E
    k = (1.0 - lb) * nsig
    q = qr / (1.0 + jnp.exp(-qr))
    return logf2, k, q


def _hgrn2_head_products(b, q, k, v, lvl, pos, st_ref, h, keep):
    t, dk = b.shape
    n_levels = t.bit_length() - 1
    hb = t // 2
    zeros = jnp.zeros((hb, dk), BF16)
    lvl_d = jnp.concatenate([lvl[0:hb, 0:hb], lvl[hb:t, hb:t]], axis=1)
    diag = jnp.zeros((hb, t), F32)
    for level in range(1, n_levels):
        group = 1 << level
        d = b - _midpoint_rows(b, group, pos)
        later = (pos & (group - 1)) >= (group // 2)
        fac = jnp.exp2(-jnp.abs(d))
        ql = jnp.where(later, q * fac, 0.0).astype(BF16)
        kl = jnp.where(later, 0.0, k * fac).astype(BF16)
        lhs = jnp.concatenate([ql[0:hb], ql[hb:t]], axis=1)
        rhs = jnp.concatenate([jnp.concatenate([kl[0:hb], zeros], axis=1),
                               jnp.concatenate([zeros, kl[hb:t]], axis=1)], axis=0)
        diag = jnp.where(lvl_d == level, _dot_nt(lhs, rhs), diag)
    b_mid = b[hb - 1:hb, :]
    q_top = (q[hb:t] * jnp.exp2(b[hb:t] - b_mid)).astype(BF16)
    k_top = (k[0:hb] * jnp.exp2(b_mid - b[0:hb])).astype(BF16)
    top = _dot_nt(q_top, k_top)
    scores = jnp.concatenate(
        [jnp.concatenate([diag[:, 0:hb], jnp.zeros((hb, hb), F32)], axis=1),
         jnp.concatenate([top, diag[:, hb:t]], axis=1)], axis=0)

    st = st_ref[h] * keep
    b_last = b[t - 1:t, :]
    o = _dot_nt((q * jnp.exp2(b)).astype(BF16), st.astype(BF16))
    o = o + jnp.sum(q * k, axis=1, keepdims=True) * v.astype(F32)
    k_dec = (k * jnp.exp2(b_last - b)).astype(BF16)
    st_ref[h] = jnp.exp2(b_last) * st + _dot_tn(v, k_dec)
    return scores.astype(BF16), o


def _hgrn2_head_output(scores, o, v, gr, ng):
    return _rms(o + _dot(scores, v), ng) * (gr / (1.0 + jnp.exp(-gr)))


def _proj_hgrn2_kernel(*refs, tiles_per_seq, n_cast):
    x0_ref, xnext_ref, g_ref, w_hbm, lbl_ref, ng_ref, lvl_ref = refs[:7]
    cast_in = refs[7:7 + n_cast]
    q_ref, k_ref, vt_ref, gate_ref, ohg_ref = refs[7 + n_cast:12 + n_cast]
    cast_out = refs[12 + n_cast:12 + 2 * n_cast]
    xn_buf, f_buf, iqg_buf, st_ref, w_ref, w_stage, w_sems = refs[12 + 2 * n_cast:]
    d = xnext_ref.shape[1]
    t = HG_BLOCK
    n_blocks = xnext_ref.shape[0] // t
    dk = HG_HEAD_DIM
    hg_w = HG_HEADS * dk
    sb_w = q_ref.shape[1]
    i = pl.program_id(0)
    n_steps = pl.num_programs(0)
    slot = i % 2
    prev = 1 - slot

    def step(do_proj, do_hgrn2, fetch=None):
        row = lax.broadcasted_iota(jnp.int32, (t, t), 0)
        col = lax.broadcasted_iota(jnp.int32, (t, t), 1)
        tril = (row >= col).astype(BF16)
        pos = lax.broadcasted_iota(jnp.int32, (t, dk), 0)
        lvl = lvl_ref[...]
        first_keep = jnp.where(i % tiles_per_seq == 1, 0.0, 1.0)

        lbl = lbl_ref[...]
        ex = jnp.exp(lbl - jnp.max(lbl, axis=0, keepdims=True))
        lb_all = ex[0:1, :] / jnp.sum(ex, axis=0, keepdims=True)

        def prepare_block(blk):
            rows = slice(blk * t, (blk + 1) * t)
            logf2, k_all, q_all = _hgrn2_gates(
                f_buf[prev, rows, :], iqg_buf[prev, rows, hg_w:2 * hg_w].astype(F32), lb_all)
            g1, g2, g3 = _split3(logf2)
            return _dot(tril, g1) + _dot(tril, g2) + _dot(tril, g3), q_all, k_all

        def head_products(blk, h, prepared):
            b_all, q_all, k_all = prepared
            rows = slice(blk * t, (blk + 1) * t)
            sl = slice(h * dk, (h + 1) * dk)
            return _hgrn2_head_products(b_all[:, sl], q_all[:, sl], k_all[:, sl],
                                        iqg_buf[prev, rows, sl], lvl, pos, st_ref, h,
                                        first_keep if blk == 0 else 1.0)

        def head_output(blk, h, products):
            rows = slice(blk * t, (blk + 1) * t)
            sl = slice(h * dk, (h + 1) * dk)
            gr = iqg_buf[prev, rows, 2 * hg_w + h * dk:2 * hg_w + (h + 1) * dk].astype(F32)
            y = _hgrn2_head_output(*products, iqg_buf[prev, rows, sl], gr, ng_ref[:, sl])
            ohg_ref[rows, sl] = y.astype(ohg_ref.dtype)

        c0 = 3 * sb_w
        c1 = c0 + hg_w
        c2 = c1 + 3 * hg_w
        slab = d // 2
        slabs = []
        if do_proj:
            xn = xn_buf[slot]

            def project(lo, hi):
                if fetch is not None:
                    fetch(lo, hi)
                return _dot(xn, w_ref[:, lo:hi])

            def q_slab():
                q_ref[...] = (project(0, sb_w) * (SB_HEAD_DIM ** -0.5 * LOG2E)).astype(q_ref.dtype)

            def k_slab():
                k_ref[...] = project(sb_w, 2 * sb_w).astype(k_ref.dtype)

            def v_slab():
                v = project(2 * sb_w, 3 * sb_w)
                for blk in range(n_blocks):
                    vt_ref[0, :, blk * t:(blk + 1) * t] = \
                        v[blk * t:(blk + 1) * t].T.astype(vt_ref.dtype)

            def f_slab():
                f_buf[slot] = project(c0, c1)

            def iqg_slab(lo):
                def run():
                    iqg_buf[slot, :, lo:lo + slab] = \
                        project(c1 + lo, c1 + lo + slab).astype(iqg_buf.dtype)
                return run

            def gate_slab(lo):
                def run():
                    gate_ref[:, lo:lo + slab] = project(c2 + lo, c2 + lo + slab)
                return run

            for src, dst in zip(cast_in, cast_out):
                dst[...] = src[...].astype(dst.dtype)
            slabs = [v_slab, f_slab] + [iqg_slab(lo) for lo in range(0, 3 * hg_w, slab)] \
                + [gate_slab(lo) for lo in range(0, 2 * d, slab)]
            q_slab()
            k_slab()
        if do_hgrn2:
            prepared = prepare_block(0)
            if slabs:
                slabs.pop(0)()
            for blk in range(n_blocks):
                for h in range(HG_HEADS):
                    if blk > 0 and h == 0:
                        prepared = prepare_block(blk)
                    products = head_products(blk, h, prepared)
                    if slabs:
                        slabs.pop(0)()
                    head_output(blk, h, products)
        for run in slabs:
            run()
        if do_proj:
            xn_buf[prev] = _rms(xnext_ref[...], g_ref[...]).astype(BF16)

    @pl.when(i == 0)
    def _():
        st_ref[...] = jnp.zeros_like(st_ref)
        xn_buf[0] = _rms(x0_ref[...], g_ref[...]).astype(BF16)
        width = w_stage.shape[2]
        n_slabs = w_ref.shape[1] // width
        ready = [0]

        def slab_copy(n):
            return pltpu.make_async_copy(w_hbm.at[:, pl.ds(n * width, width)],
                                         w_stage.at[n % 2], w_sems.at[n % 2])

        def fetch(lo, hi):
            assert lo <= ready[0] * width, "projection slabs must be taken in column order"
            while ready[0] * width < hi:
                n = ready[0]
                if n + 1 < n_slabs:
                    slab_copy(n + 1).start()
                slab_copy(n).wait()
                w_ref[:, n * width:(n + 1) * width] = w_stage[n % 2].astype(w_ref.dtype)
                ready[0] = n + 1

        slab_copy(0).start()
        step(True, False, fetch)
        assert ready[0] == n_slabs

    @pl.when(jnp.logical_and(i > 0, i < n_steps - 1))
    def _():
        step(True, True)

    @pl.when(i == n_steps - 1)
    def _():
        step(False, True)


def _pair_levels(t):
    idx = np.arange(t)
    x = idx[:, None] ^ idx[None, :]
    lev = np.where(x > 0, np.floor(np.log2(np.maximum(x, 1))).astype(np.int64) + 1, 0)
    return np.where(idx[:, None] > idx[None, :], lev, 0).astype(np.int32)


def _proj_hgrn2(x2, g1, w_in, lb_logits, ng, later_weights, batch, sb_w, hg_w):
    n, d = x2.shape
    t = PROJ_BLOCKS * HG_BLOCK
    cols = w_in.shape[1]
    n_tiles = n // t
    tiles_per_seq = n_tiles // batch
    assert hg_w == HG_HEADS * HG_HEAD_DIM and n_tiles * t == n and tiles_per_seq * batch == n_tiles
    lvl = jnp.asarray(_pair_levels(HG_BLOCK))
    cur = lambda i: jnp.minimum(i, n_tiles - 1)
    rows = lambda w: pl.BlockSpec((t, w), lambda i: (cur(i), 0))
    vt_spec = pl.BlockSpec((1, sb_w, t),
                           lambda i: (cur(i) // tiles_per_seq, 0, cur(i) % tiles_per_seq))
    sds = lambda w, dt: jax.ShapeDtypeStruct((n, w), dt)
    cast_specs = [pl.BlockSpec((w.shape[0] // n_tiles, w.shape[1]), lambda i: (cur(i), 0))
                  for w in later_weights]
    assert all(w.shape[0] % (n_tiles * 16) == 0 for w in later_weights)
    cast_bytes = sum(w.size // n_tiles * 6 for w in later_weights)
    moving = t * (d * 4 + 2 * sb_w * 2 + sb_w * 2 + 2 * d * 4 + hg_w * 2) + cast_bytes
    stage_cols = d // 2
    assert cols % stage_cols == 0
    resident = d * 4 + d * cols * 2 + 2 * d * stage_cols * 4 + 3 * hg_w * 4 \
        + HG_BLOCK * HG_BLOCK * 4 + t * d * 4 \
        + 2 * t * d * 2 + 2 * t * hg_w * (4 + 3 * 2) + HG_HEADS * HG_HEAD_DIM * HG_HEAD_DIM * 4
    return pl.pallas_call(
        functools.partial(_proj_hgrn2_kernel, tiles_per_seq=tiles_per_seq,
                          n_cast=len(later_weights)),
        grid=(n_tiles + 1,),
        in_specs=[_resident((t, d)), pl.BlockSpec((t, d), lambda i: (cur(i + 1), 0)),
                  _resident((1, d)), pl.BlockSpec(memory_space=pl.ANY),
                  _resident(lb_logits.shape), _resident(ng.shape),
                  _resident((HG_BLOCK, HG_BLOCK))] + cast_specs,
        out_specs=[rows(sb_w), rows(sb_w), vt_spec, rows(2 * d),
                   pl.BlockSpec((t, hg_w), lambda i: (jnp.maximum(i - 1, 0), 0))] + cast_specs,
        out_shape=[sds(sb_w, BF16), sds(sb_w, BF16),
                   jax.ShapeDtypeStruct((batch, sb_w, n // batch), BF16),
                   sds(2 * d, F32), sds(hg_w, BF16)]
        + [jax.ShapeDtypeStruct(w.shape, BF16) for w in later_weights],
        scratch_shapes=[pltpu.VMEM((2, t, d), BF16),
                        pltpu.VMEM((2, t, hg_w), F32),
                        pltpu.VMEM((2, t, 3 * hg_w), BF16),
                        pltpu.VMEM((HG_HEADS, HG_HEAD_DIM, HG_HEAD_DIM), F32),
                        pltpu.VMEM((d, cols), BF16),
                        pltpu.VMEM((2, d, stage_cols), F32),
                        pltpu.SemaphoreType.DMA((2,))],
        compiler_params=pltpu.CompilerParams(
            dimension_semantics=("arbitrary",),
            vmem_limit_bytes=_vmem_limit(moving, resident,
                                         t * (d * 6 + cols * 4) + 64 * HG_BLOCK * HG_BLOCK * 4)),
        name="proj_hgrn2",
    )(x2, x2, g1, w_in, lb_logits, ng, lvl, *later_weights)


def _attn_ffn_kernel(q_ref, kd_ref, kp_ref, vtd_ref, vtp_ref, k_hbm, vt_hbm, x_ref, ohg_ref,
                     gate_ref, bg_ref, g2_ref, gf_ref, wsb_hbm, whg_hbm, wout_hbm, w1_hbm, w2_hbm,
                     o_ref, osb_buf, qm_ref, acc_ref, carry_ref, k_buf, vt_buf, sems,
                     wsb_ref, whg_ref, wout_ref, w1_ref, w2_ref, w_sems,
                     *, tiles_per_seq, n_tiles):
    t = q_ref.shape[1]
    d = x_ref.shape[1]
    heads = range(SB_HEADS)
    i = pl.program_id(0)
    slot = i % 2
    qi = jnp.minimum(i, n_tiles - 1) % tiles_per_seq

    row = lax.broadcasted_iota(jnp.int32, (t, t), 0)
    col = lax.broadcasted_iota(jnp.int32, (t, t), 1)
    tri = (col > row).astype(BF16)
    causal = row < col

    def logits(k):
        return [_dot_nt(k[:, (h // 2) * V7X_LANES:(h // 2 + 1) * V7X_LANES], qm_ref[h])
                for h in heads]

    def softplus_phase(z, mask):
        sp = [_softplus2(z[h], mask) for h in heads]
        return sp, [sp[h].astype(BF16) for h in heads]

    def cumsum_phase(spb):
        return [_dot(tri, spb[h]) for h in heads]

    def weight_phase(z, sp, later, mask):
        w = [jnp.exp2(z[h] - sp[h] - later[h]) for h in heads]
        if mask is not None:
            w = [jnp.where(mask, w[h], 0.0) for h in heads]
        return [w[h].astype(BF16) for h in heads]

    def value_phase(vt, w, later, spb):
        pv = [_dot(vt[h * SB_HEAD_DIM:(h + 1) * SB_HEAD_DIM, :], w[h]) for h in heads]
        return pv, [later[h][0:1, :] + spb[h][0:1, :].astype(F32) for h in heads]

    def main_block(do_attn, do_ffn):
        has_prev = qi > 0
        if do_attn:
            lane = lax.broadcasted_iota(jnp.int32, (t, V7X_LANES), 1)
            zero = jnp.zeros((), BF16)
            q = q_ref[0]
            for h in heads:
                grp = q[:, (h // 2) * V7X_LANES:(h // 2 + 1) * V7X_LANES]
                qm_ref[h] = jnp.where((lane // SB_HEAD_DIM) == (h % 2), grp, zero)
        if do_ffn:
            a_sb = _dot_tn(osb_buf[1 - slot], wsb_ref[...])
            a_hg = _dot(ohg_ref[...], whg_ref[...])
        if do_attn:
            z0 = logits(kd_ref[0])
        if do_ffn:
            gates = 1.0 / (1.0 + jnp.exp(-(gate_ref[...] + bg_ref[...])))
            merged = (gates[:, :d] * a_sb + gates[:, d:] * a_hg).astype(BF16)
            hres = x_ref[...] + _dot(merged, wout_ref[...])
        if do_attn:
            z1 = logits(kp_ref[0])
        if do_ffn:
            hn = _rms(hres, g2_ref[...]).astype(BF16)
            half = w1_ref.shape[1] // 2

            def mlp_up(lo):
                act = jnp.maximum(_dot(hn, w1_ref[:, lo:lo + half]), 0.0)
                return (act * act).astype(BF16)

            act_a = mlp_up(0)
        if do_attn:
            sp0, spb0 = softplus_phase(z0, causal)
            lat0 = cumsum_phase(spb0)
        if do_ffn:
            act_b = mlp_up(half)
        if do_attn:
            sp1, spb1 = softplus_phase(z1, None)
            lat1 = cumsum_phase(spb1)
        if do_ffn:
            hres = hres + _dot(act_a, w2_ref[0:half, :])
        if do_attn:
            w0 = weight_phase(z0, sp0, lat0, causal)
            w1 = weight_phase(z1, sp1, lat1, None)
        if do_ffn:
            hres = hres + _dot(act_b, w2_ref[half:2 * half, :])
        if do_attn:
            pv0, tot0 = value_phase(vtd_ref.at[0], w0, lat0, spb0)
            pv1, tot1 = value_phase(vtp_ref.at[0], w1, lat1, spb1)
        if do_ffn:
            o_ref[...] = _rms(hres, gf_ref[...]).astype(o_ref.dtype)
        if do_attn:
            for h in heads:
                scale = jnp.where(has_prev, jnp.exp2(-tot0[h]), 0.0)
                acc_ref[h] = pv0[h] + scale * pv1[h]
                carry_ref[h:h + 1, :] = tot0[h] + jnp.where(has_prev, tot1[h], 0.0)

    def sweep_rest():
        def more(state):
            n, live = state
            return jnp.logical_and(n < qi, live)

        seq = jnp.minimum(i, n_tiles - 1) // tiles_per_seq

        def tile_copies(j):
            start = pl.multiple_of(j * t, t)
            return (pltpu.make_async_copy(k_hbm.at[seq, pl.ds(start, t), :], k_buf, sems.at[0]),
                    pltpu.make_async_copy(vt_hbm.at[seq, :, pl.ds(start, t)], vt_buf, sems.at[1]))

        def body(state):
            n, _ = state
            copies = tile_copies(qi - 1 - n)
            for c in copies:
                c.start()
            for c in copies:
                c.wait()
            z = logits(k_buf[...])
            sp, spb = softplus_phase(z, None)
            later = cumsum_phase(spb)
            pv, tot = value_phase(vt_buf, weight_phase(z, sp, later, None), later, spb)
            for h in heads:
                c = carry_ref[h:h + 1, :]
                acc_ref[h] += jnp.exp2(-c) * pv[h]
                carry_ref[h:h + 1, :] = c + tot[h]
            return n + 1, jnp.min(carry_ref[...]) < SB_DEAD_CARRY

        lax.while_loop(more, body, (jnp.int32(1), jnp.min(carry_ref[...]) < SB_DEAD_CARRY))
        osb_buf[slot] = acc_ref[...].reshape(SB_HEADS * SB_HEAD_DIM, t).astype(osb_buf.dtype)

    weight_copies = [pltpu.make_async_copy(src, dst, w_sems.at[n]) for n, (src, dst) in enumerate(
        ((wsb_hbm, wsb_ref), (whg_hbm, whg_ref), (wout_hbm, wout_ref), (w1_hbm, w1_ref),
         (w2_hbm, w2_ref)))]

    @pl.when(i == 0)
    def _():
        for c in weight_copies:
            c.start()
        main_block(True, False)

    @pl.when(i == 1)
    def _():
        for c in weight_copies:
            c.wait()

    @pl.when(jnp.logical_and(i > 0, i < n_tiles))
    def _():
        main_block(True, True)

    @pl.when(i < n_tiles)
    def _():
        sweep_rest()

    @pl.when(i == n_tiles)
    def _():
        main_block(False, True)


def _attn_ffn(q3, k3, vt3, x2, ohg, gates, bg, wsb, whg, wout, g2, w1, w2, gf):
    b, s, sb_w = q3.shape
    n, d = x2.shape
    t = ATTN_TILE
    dff = w1.shape[1]
    tiles_per_seq = s // t
    n_tiles = n // t
    assert sb_w == SB_HEADS * SB_HEAD_DIM and tiles_per_seq * t == s and n_tiles == b * tiles_per_seq
    cur = lambda i: jnp.minimum(i, n_tiles - 1)
    prv = lambda i: (jnp.maximum(i - 1, 0), 0)
    seq = lambda i: cur(i) // tiles_per_seq
    tile = lambda i: cur(i) % tiles_per_seq
    before = lambda i: jnp.maximum(tile(i) - 1, 0)
    k_tile = lambda pos: pl.BlockSpec((1, t, sb_w), lambda i: (seq(i), pos(i), 0))
    vt_tile = lambda pos: pl.BlockSpec((1, sb_w, t), lambda i: (seq(i), 0, pos(i)))
    in_hbm = pl.BlockSpec(memory_space=pl.ANY)
    full = lambda a: _resident(a.shape)
    weights = (wsb, whg, wout, w1, w2)
    resident_bytes = sum(a.size * a.dtype.itemsize for a in (bg, g2, gf) + weights) \
        + 2 * t * sb_w * 2 + 2 * t * sb_w * 2 + SB_HEADS * t * (V7X_LANES * 2 + SB_HEAD_DIM * 4 + 4)
    moving = t * (5 * sb_w * 2 + d * 4 + ohg.shape[1] * 2 + 2 * d * 4 + d * 4)
    return pl.pallas_call(
        functools.partial(_attn_ffn_kernel, tiles_per_seq=tiles_per_seq, n_tiles=n_tiles),
        grid=(n_tiles + 1,),
        in_specs=[k_tile(tile), k_tile(tile), k_tile(before), vt_tile(tile), vt_tile(before),
                  in_hbm, in_hbm,
                  pl.BlockSpec((t, d), prv),
                  pl.BlockSpec((t, ohg.shape[1]), prv),
                  pl.BlockSpec((t, 2 * d), prv),
                  full(bg), full(g2), full(gf)] + [in_hbm] * len(weights),
        out_specs=pl.BlockSpec((t, d), prv),
        out_shape=jax.ShapeDtypeStruct((n, d), x2.dtype),
        scratch_shapes=[pltpu.VMEM((2, sb_w, t), BF16),
                        pltpu.VMEM((SB_HEADS, t, V7X_LANES), BF16),
                        pltpu.VMEM((SB_HEADS, SB_HEAD_DIM, t), F32),
                        pltpu.VMEM((SB_HEADS, t), F32),
                        pltpu.VMEM((t, sb_w), BF16),
                        pltpu.VMEM((sb_w, t), BF16),
                        pltpu.SemaphoreType.DMA((2,))]
        + [pltpu.VMEM(a.shape, a.dtype) for a in weights]
        + [pltpu.SemaphoreType.DMA((len(weights),))],
        compiler_params=pltpu.CompilerParams(
            dimension_semantics=("arbitrary",),
            vmem_limit_bytes=_vmem_limit(moving, resident_bytes,
                                         t * (dff * 6 + d * 24) + 8 * SB_HEADS * t * t * 4)),
        name="attn_ffn",
    )(q3, k3, k3, vt3, vt3, k3, vt3, x2, ohg, gates, bg, g2, gf, *weights)


def kernel(x, norm1_g, w_in, b_gate, lb_logits, hg_norm_g, w_o_sb, w_o_hg, w_out, norm2_g,
           w_ff1, w_ff2, final_g):
    b, s, d = x.shape
    assert w_in.shape[0] == 1, "single-layer block"
    sb_w = SB_HEADS * SB_HEAD_DIM
    hg_w = HG_HEADS * HG_HEAD_DIM
    x2 = x.reshape(b * s, d)
    later = (w_o_sb[0], w_o_hg[0], w_out[0], w_ff1[0], w_ff2[0])
    q, k, vt, gates, o_hg, wsb, whg, wout, w1, w2 = _proj_hgrn2(
        x2, norm1_g, w_in[0], lb_logits, hg_norm_g, later, b, sb_w, hg_w)
    out = _attn_ffn(q.reshape(b, s, sb_w), k.reshape(b, s, sb_w), vt, x2, o_hg, gates, b_gate,
                    wsb, whg, wout, norm2_g, w1, w2, final_g.reshape(1, d))
    return out.reshape(b, s, d)
```

```python
import functools

import jax
import jax.numpy as jnp
import numpy as np
from jax import lax
from jax.experimental import pallas as pl
from jax.experimental.pallas import tpu as pltpu

F32 = jnp.float32
BF16 = jnp.bfloat16

SB_HEADS = 8
SB_HEAD_DIM = 64
HG_HEADS = 4
HG_HEAD_DIM = 128
EPS = 1e-6
LOG2E = 1.4426950408889634
SB_DEAD_CARRY = 151.0

V7X_LANES = 128
V7X_MXU_DIM = 256
V7X_VMEM_BYTES = 64 * 1024 * 1024

ATTN_TILE = V7X_MXU_DIM
HG_BLOCK = V7X_MXU_DIM
PROJ_BLOCKS = 2


def _vmem_limit(pipelined_bytes, resident_bytes, temp_bytes):
    need = 2 * pipelined_bytes + resident_bytes + temp_bytes
    return int(min(need + need // 4, V7X_VMEM_BYTES - 8 * 1024 * 1024))


def _resident(shape):
    return pl.BlockSpec(shape, lambda *_: (0,) * len(shape), pipeline_mode=pl.Buffered(1))


def _rms(x, g):
    ms = jnp.mean(x * x, axis=-1, keepdims=True)
    return x * lax.rsqrt(ms + EPS) * g


def _dot(a, b):
    return jnp.dot(a, b, preferred_element_type=F32)


def _dot_nt(a, b):
    return lax.dot_general(a, b, (((1,), (1,)), ((), ())), preferred_element_type=F32)


def _dot_tn(a, b):
    return lax.dot_general(a, b, (((0,), (0,)), ((), ())), preferred_element_type=F32)


def _softplus2(z, mask):
    sp = jnp.maximum(z, 0.0) + jnp.log(1.0 + jnp.exp2(-jnp.abs(z))) * LOG2E
    return sp if mask is None else jnp.where(mask, sp, 0.0)


def _split3(x):
    a = x.astype(BF16)
    r = x - a.astype(F32)
    b = r.astype(BF16)
    c = (r - b.astype(F32)).astype(BF16)
    return a, b, c


def _rows_from_group(b, group, r):
    n, c = b.shape
    if group == n:
        return jnp.broadcast_to(b[r:r + 1, :], (n, c))
    b3 = b.reshape(n // group, group, c)
    return jnp.broadcast_to(b3[:, r:r + 1, :], b3.shape).reshape(n, c)


def _midpoint_rows(b, group, pos):
    n = b.shape[0]
    half = group // 2
    if group >= 16:
        return _rows_from_group(b, group, half - 1)
    if group == 8:
        return _rows_from_group(b, 8, 3)
    up1 = pltpu.roll(b, n - 1, 0)
    dn1 = pltpu.roll(b, 1, 0)
    if group == 2:
        return jnp.where((pos & 1) == 0, b, dn1)
    assert group == 4
    dn2 = pltpu.roll(b, 2, 0)
    r4 = pos & 3
    return jnp.where(r4 == 0, up1, jnp.where(r4 == 1, b, jnp.where(r4 == 2, dn1, dn2)))


def _hgrn2_gates(fr, qr, lb):
    e = jnp.exp(-jnp.abs(fr))
    r = 1.0 / (1.0 + e)
    er = e * r
    sig = jnp.where(fr >= 0, r, er)
    nsig = jnp.where(fr >= 0, er, r)
    logf2 = jnp.log(lb + (1.0 - lb) * sig) * LOG2E
    k = (1.0 - lb) * nsig
    q = qr / (1.0 + jnp.exp(-qr))
    return logf2, k, q


def _hgrn2_head_products(b, q, k, v, lvl, pos, st_ref, h, keep):
    t, dk = b.shape
    n_levels = t.bit_length() - 1
    hb = t // 2
    zeros = jnp.zeros((hb, dk), BF16)
    lvl_d = jnp.concatenate([lvl[0:hb, 0:hb], lvl[hb:t, hb:t]], axis=1)
    diag = jnp.zeros((hb, t), F32)
    for level in range(1, n_levels):
        group = 1 << level
        d = b - _midpoint_rows(b, group, pos)
        later = (pos & (group - 1)) >= (group // 2)
        fac = jnp.exp2(-jnp.abs(d))
        ql = jnp.where(later, q * fac, 0.0).astype(BF16)
        kl = jnp.where(later, 0.0, k * fac).astype(BF16)
        lhs = jnp.concatenate([ql[0:hb], ql[hb:t]], axis=1)
        rhs = jnp.concatenate([jnp.concatenate([kl[0:hb], zeros], axis=1),
                               jnp.concatenate([zeros, kl[hb:t]], axis=1)], axis=0)
        diag = jnp.where(lvl_d == level, _dot_nt(lhs, rhs), diag)
    b_mid = b[hb - 1:hb, :]
    q_top = (q[hb:t] * jnp.exp2(b[hb:t] - b_mid)).astype(BF16)
    k_top = (k[0:hb] * jnp.exp2(b_mid - b[0:hb])).astype(BF16)
    top = _dot_nt(q_top, k_top)
    scores = jnp.concatenate(
        [jnp.concatenate([diag[:, 0:hb], jnp.zeros((hb, hb), F32)], axis=1),
         jnp.concatenate([top, diag[:, hb:t]], axis=1)], axis=0)

    st = st_ref[h] * keep
    b_last = b[t - 1:t, :]
    o = _dot_nt((q * jnp.exp2(b)).astype(BF16), st.astype(BF16))
    o = o + jnp.sum(q * k, axis=1, keepdims=True) * v.astype(F32)
    k_dec = (k * jnp.exp2(b_last - b)).astype(BF16)
    st_ref[h] = jnp.exp2(b_last) * st + _dot_tn(v, k_dec)
    return scores.astype(BF16), o


def _hgrn2_head_output(scores, o, v, gr, ng):
    return _rms(o + _dot(scores, v), ng) * (gr / (1.0 + jnp.exp(-gr)))


def _proj_hgrn2_kernel(*refs, tiles_per_seq, n_cast):
    x0_ref, xnext_ref, g_ref, w_hbm, lbl_ref, ng_ref, lvl_ref = refs[:7]
    cast_in = refs[7:7 + n_cast]
    q_ref, k_ref, vt_ref, gate_ref, ohg_ref = refs[7 + n_cast:12 + n_cast]
    cast_out = refs[12 + n_cast:12 + 2 * n_cast]
    xn_buf, f_buf, iqg_buf, st_ref, w_ref, w_stage, w_sems = refs[12 + 2 * n_cast:]
    d = xnext_ref.shape[1]
    t = HG_BLOCK
    n_blocks = xnext_ref.shape[0] // t
    dk = HG_HEAD_DIM
    hg_w = HG_HEADS * dk
    sb_w = q_ref.shape[1]
    i = pl.program_id(0)
    n_steps = pl.num_programs(0)
    slot = i % 2
    prev = 1 - slot

    def step(do_proj, do_hgrn2, fetch=None):
        row = lax.broadcasted_iota(jnp.int32, (t, t), 0)
        col = lax.broadcasted_iota(jnp.int32, (t, t), 1)
        tril = (row >= col).astype(BF16)
        pos = lax.broadcasted_iota(jnp.int32, (t, dk), 0)
        lvl = lvl_ref[...]
        first_keep = jnp.where(i % tiles_per_seq == 1, 0.0, 1.0)

        lbl = lbl_ref[...]
        ex = jnp.exp(lbl - jnp.max(lbl, axis=0, keepdims=True))
        lb_all = ex[0:1, :] / jnp.sum(ex, axis=0, keepdims=True)

        def prepare_block(blk):
            rows = slice(blk * t, (blk + 1) * t)
            logf2, k_all, q_all = _hgrn2_gates(
                f_buf[prev, rows, :], iqg_buf[prev, rows, hg_w:2 * hg_w].astype(F32), lb_all)
            g1, g2, g3 = _split3(logf2)
            return _dot(tril, g1) + _dot(tril, g2) + _dot(tril, g3), q_all, k_all

        def head_products(blk, h, prepared):
            b_all, q_all, k_all = prepared
            rows = slice(blk * t, (blk + 1) * t)
            sl = slice(h * dk, (h + 1) * dk)
            return _hgrn2_head_products(b_all[:, sl], q_all[:, sl], k_all[:, sl],
                                        iqg_buf[prev, rows, sl], lvl, pos, st_ref, h,
                                        first_keep if blk == 0 else 1.0)

        def head_output(blk, h, products):
            rows = slice(blk * t, (blk + 1) * t)
            sl = slice(h * dk, (h + 1) * dk)
            gr = iqg_buf[prev, rows, 2 * hg_w + h * dk:2 * hg_w + (h + 1) * dk].astype(F32)
            y = _hgrn2_head_output(*products, iqg_buf[prev, rows, sl], gr, ng_ref[:, sl])
            ohg_ref[rows, sl] = y.astype(ohg_ref.dtype)

        c0 = 3 * sb_w
        c1 = c0 + hg_w
        c2 = c1 + 3 * hg_w
        slab = d // 2
        slabs = []
        if do_proj:
            xn = xn_buf[slot]

            def project(lo, hi):
                if fetch is not None:
                    fetch(lo, hi)
                return _dot(xn, w_ref[:, lo:hi])

            def q_slab():
                q_ref[...] = (project(0, sb_w) * (SB_HEAD_DIM ** -0.5 * LOG2E)).astype(q_ref.dtype)

            def k_slab():
                k_ref[...] = project(sb_w, 2 * sb_w).astype(k_ref.dtype)

            def v_slab():
                v = project(2 * sb_w, 3 * sb_w)
                for blk in range(n_blocks):
                    vt_ref[0, :, blk * t:(blk + 1) * t] = \
                        v[blk * t:(blk + 1) * t].T.astype(vt_ref.dtype)

            def f_slab():
                f_buf[slot] = project(c0, c1)

            def iqg_slab(lo):
                def run():
                    iqg_buf[slot, :, lo:lo + slab] = \
                        project(c1 + lo, c1 + lo + slab).astype(iqg_buf.dtype)
                return run

            def gate_slab(lo):
                def run():
                    gate_ref[:, lo:lo + slab] = project(c2 + lo, c2 + lo + slab)
                return run

            for src, dst in zip(cast_in, cast_out):
                dst[...] = src[...].astype(dst.dtype)
            slabs = [v_slab, f_slab] + [iqg_slab(lo) for lo in range(0, 3 * hg_w, slab)] \
                + [gate_slab(lo) for lo in range(0, 2 * d, slab)]
            q_slab()
            k_slab()
        if do_hgrn2:
            prepared = prepare_block(0)
            if slabs:
                slabs.pop(0)()
            for blk in range(n_blocks):
                for h in range(HG_HEADS):
                    if blk > 0 and h == 0:
                        prepared = prepare_block(blk)
                    products = head_products(blk, h, prepared)
                    if slabs:
                        slabs.pop(0)()
                    head_output(blk, h, products)
        for run in slabs:
            run()
        if do_proj:
            xn_buf[prev] = _rms(xnext_ref[...], g_ref[...]).astype(BF16)

    @pl.when(i == 0)
    def _():
        st_ref[...] = jnp.zeros_like(st_ref)
        xn_buf[0] = _rms(x0_ref[...], g_ref[...]).astype(BF16)
        width = w_stage.shape[2]
        n_slabs = w_ref.shape[1] // width
        ready = [0]

        def slab_copy(n):
            return pltpu.make_async_copy(w_hbm.at[:, pl.ds(n * width, width)],
                                         w_stage.at[n % 2], w_sems.at[n % 2])

        def fetch(lo, hi):
            assert lo <= ready[0] * width, "projection slabs must be taken in column order"
            while ready[0] * width < hi:
                n = ready[0]
                if n + 1 < n_slabs:
                    slab_copy(n + 1).start()
                slab_copy(n).wait()
                w_ref[:, n * width:(n + 1) * width] = w_stage[n % 2].astype(w_ref.dtype)
                ready[0] = n + 1

        slab_copy(0).start()
        step(True, False, fetch)
        assert ready[0] == n_slabs

    @pl.when(jnp.logical_and(i > 0, i < n_steps - 1))
    def _():
        step(True, True)

    @pl.when(i == n_steps - 1)
    def _():
        step(False, True)


def _pair_levels(t):
    idx = np.arange(t)
    x = idx[:, None] ^ idx[None, :]
    lev = np.where(x > 0, np.floor(np.log2(np.maximum(x, 1))).astype(np.int64) + 1, 0)
    return np.where(idx[:, None] > idx[None, :], lev, 0).astype(np.int32)


def _proj_hgrn2(x2, g1, w_in, lb_logits, ng, later_weights, batch, sb_w, hg_w):
    n, d = x2.shape
    t = PROJ_BLOCKS * HG_BLOCK
    cols = w_in.shape[1]
    n_tiles = n // t
    tiles_per_seq = n_tiles // batch
    assert hg_w == HG_HEADS * HG_HEAD_DIM and n_tiles * t == n and tiles_per_seq * batch == n_tiles
    lvl = jnp.asarray(_pair_levels(HG_BLOCK))
    cur = lambda i: jnp.minimum(i, n_tiles - 1)
    rows = lambda w: pl.BlockSpec((t, w), lambda i: (cur(i), 0))
    vt_spec = pl.BlockSpec((1, sb_w, t),
                           lambda i: (cur(i) // tiles_per_seq, 0, cur(i) % tiles_per_seq))
    sds = lambda w, dt: jax.ShapeDtypeStruct((n, w), dt)
    cast_specs = [pl.BlockSpec((w.shape[0] // n_tiles, w.shape[1]), lambda i: (cur(i), 0))
                  for w in later_weights]
    assert all(w.shape[0] % (n_tiles * 16) == 0 for w in later_weights)
    cast_bytes = sum(w.size // n_tiles * 6 for w in later_weights)
    moving = t * (d * 4 + 2 * sb_w * 2 + sb_w * 2 + 2 * d * 4 + hg_w * 2) + cast_bytes
    stage_cols = d // 2
    assert cols % stage_cols == 0
    resident = d * 4 + d * cols * 2 + 2 * d * stage_cols * 4 + 3 * hg_w * 4 \
        + HG_BLOCK * HG_BLOCK * 4 + t * d * 4 \
        + 2 * t * d * 2 + 2 * t * hg_w * (4 + 3 * 2) + HG_HEADS * HG_HEAD_DIM * HG_HEAD_DIM * 4
    return pl.pallas_call(
        functools.partial(_proj_hgrn2_kernel, tiles_per_seq=tiles_per_seq,
                          n_cast=len(later_weights)),
        grid=(n_tiles + 1,),
        in_specs=[_resident((t, d)), pl.BlockSpec((t, d), lambda i: (cur(i + 1), 0)),
                  _resident((1, d)), pl.BlockSpec(memory_space=pl.ANY),
                  _resident(lb_logits.shape), _resident(ng.shape),
                  _resident((HG_BLOCK, HG_BLOCK))] + cast_specs,
        out_specs=[rows(sb_w), rows(sb_w), vt_spec, rows(2 * d),
                   pl.BlockSpec((t, hg_w), lambda i: (jnp.maximum(i - 1, 0), 0))] + cast_specs,
        out_shape=[sds(sb_w, BF16), sds(sb_w, BF16),
                   jax.ShapeDtypeStruct((batch, sb_w, n // batch), BF16),
                   sds(2 * d, F32), sds(hg_w, BF16)]
        + [jax.ShapeDtypeStruct(w.shape, BF16) for w in later_weights],
        scratch_shapes=[pltpu.VMEM((2, t, d), BF16),
                        pltpu.VMEM((2, t, hg_w), F32),
                        pltpu.VMEM((2, t, 3 * hg_w), BF16),
                        pltpu.VMEM((HG_HEADS, HG_HEAD_DIM, HG_HEAD_DIM), F32),
                        pltpu.VMEM((d, cols), BF16),
                        pltpu.VMEM((2, d, stage_cols), F32),
                        pltpu.SemaphoreType.DMA((2,))],
        compiler_params=pltpu.CompilerParams(
            dimension_semantics=("arbitrary",),
            vmem_limit_bytes=_vmem_limit(moving, resident,
                                         t * (d * 6 + cols * 4) + 64 * HG_BLOCK * HG_BLOCK * 4)),
        name="proj_hgrn2",
    )(x2, x2, g1, w_in, lb_logits, ng, lvl, *later_weights)


def _attn_ffn_kernel(q_ref, kd_ref, kp_ref, vtd_ref, vtp_ref, k_hbm, vt_hbm, x_ref, ohg_ref,
                     gate_ref, bg_ref, g2_ref, gf_ref, wsb_hbm, whg_hbm, wout_hbm, w1_hbm, w2_hbm,
                     o_ref, osb_buf, qm_ref, acc_ref, carry_ref, k_buf, vt_buf, sems,
                     wsb_ref, whg_ref, wout_ref, w1_ref, w2_ref, w_sems,
                     *, tiles_per_seq, n_tiles):
    t = q_ref.shape[1]
    d = x_ref.shape[1]
    heads = range(SB_HEADS)
    i = pl.program_id(0)
    slot = i % 2
    qi = jnp.minimum(i, n_tiles - 1) % tiles_per_seq

    row = lax.broadcasted_iota(jnp.int32, (t, t), 0)
    col = lax.broadcasted_iota(jnp.int32, (t, t), 1)
    tri = (col > row).astype(BF16)
    causal = row < col

    def logits(k):
        return [_dot_nt(k[:, (h // 2) * V7X_LANES:(h // 2 + 1) * V7X_LANES], qm_ref[h])
                for h in heads]

    def by_key_halves(fn, mask, *tiles):
        hk = t // 2
        parts = [fn(*[x[lo:lo + hk] for x in tiles], None if mask is None else mask[lo:lo + hk])
                 for lo in (0, hk)]
        return jnp.concatenate(parts, axis=0)

    def softplus_phase(z, mask):
        sp = [by_key_halves(_softplus2, mask, z[h]) for h in heads]
        return sp, [sp[h].astype(BF16) for h in heads]

    def cumsum_phase(spb):
        return [_dot(tri, spb[h]) for h in heads]

    def weight_phase(z, sp, later, mask):
        def weights(zp, spp, lp, m):
            w = jnp.exp2(zp - spp - lp)
            return (w if m is None else jnp.where(m, w, 0.0)).astype(BF16)
        return [by_key_halves(weights, mask, z[h], sp[h], later[h]) for h in heads]

    def value_phase(vt, w, later, spb):
        pv = [_dot(vt[h * SB_HEAD_DIM:(h + 1) * SB_HEAD_DIM, :], w[h]) for h in heads]
        return pv, [later[h][0:1, :] + spb[h][0:1, :].astype(F32) for h in heads]

    def main_block(do_attn, do_ffn):
        has_prev = qi > 0
        if do_attn:
            lane = lax.broadcasted_iota(jnp.int32, (t, V7X_LANES), 1)
            zero = jnp.zeros((), BF16)
            q = q_ref[0]
            for h in heads:
                grp = q[:, (h // 2) * V7X_LANES:(h // 2 + 1) * V7X_LANES]
                qm_ref[h] = jnp.where((lane // SB_HEAD_DIM) == (h % 2), grp, zero)
        if do_ffn:
            a_sb = _dot_tn(osb_buf[1 - slot], wsb_ref[...])
            a_hg = _dot(ohg_ref[...], whg_ref[...])
        if do_attn:
            z0 = logits(kd_ref[0])
        if do_ffn:
            gates = 1.0 / (1.0 + jnp.exp(-(gate_ref[...] + bg_ref[...])))
            merged = (gates[:, :d] * a_sb + gates[:, d:] * a_hg).astype(BF16)
            hres = x_ref[...] + _dot(merged, wout_ref[...])
        if do_attn:
            z1 = logits(kp_ref[0])
        if do_ffn:
            hn = _rms(hres, g2_ref[...]).astype(BF16)
            half = w1_ref.shape[1] // 2

            def mlp_up(lo):
                act = jnp.maximum(_dot(hn, w1_ref[:, lo:lo + half]), 0.0)
                return (act * act).astype(BF16)

            act_a = mlp_up(0)
        if do_attn:
            sp0, spb0 = softplus_phase(z0, causal)
            lat0 = cumsum_phase(spb0)
        if do_ffn:
            act_b = mlp_up(half)
        if do_attn:
            sp1, spb1 = softplus_phase(z1, None)
            lat1 = cumsum_phase(spb1)
        if do_ffn:
            hres = hres + _dot(act_a, w2_ref[0:half, :])
        if do_attn:
            w0 = weight_phase(z0, sp0, lat0, causal)
            w1 = weight_phase(z1, sp1, lat1, None)
        if do_ffn:
            hres = hres + _dot(act_b, w2_ref[half:2 * half, :])
        if do_attn:
            pv0, tot0 = value_phase(vtd_ref.at[0], w0, lat0, spb0)
            pv1, tot1 = value_phase(vtp_ref.at[0], w1, lat1, spb1)
        if do_ffn:
            o_ref[...] = _rms(hres, gf_ref[...]).astype(o_ref.dtype)
        if do_attn:
            for h in heads:
                scale = jnp.where(has_prev, jnp.exp2(-tot0[h]), 0.0)
                acc_ref[h] = pv0[h] + scale * pv1[h]
                carry_ref[h:h + 1, :] = tot0[h] + jnp.where(has_prev, tot1[h], 0.0)

    def sweep_rest():
        def more(state):
            n, live = state
            return jnp.logical_and(n < qi, live)

        seq = jnp.minimum(i, n_tiles - 1) // tiles_per_seq

        def tile_copies(j):
            start = pl.multiple_of(j * t, t)
            return (pltpu.make_async_copy(k_hbm.at[seq, pl.ds(start, t), :], k_buf, sems.at[0]),
                    pltpu.make_async_copy(vt_hbm.at[seq, :, pl.ds(start, t)], vt_buf, sems.at[1]))

        def body(state):
            n, _ = state
            copies = tile_copies(qi - 1 - n)
            for c in copies:
                c.start()
            for c in copies:
                c.wait()
            z = logits(k_buf[...])
            sp, spb = softplus_phase(z, None)
            later = cumsum_phase(spb)
            pv, tot = value_phase(vt_buf, weight_phase(z, sp, later, None), later, spb)
            for h in heads:
                c = carry_ref[h:h + 1, :]
                acc_ref[h] += jnp.exp2(-c) * pv[h]
                carry_ref[h:h + 1, :] = c + tot[h]
            return n + 1, jnp.min(carry_ref[...]) < SB_DEAD_CARRY

        lax.while_loop(more, body, (jnp.int32(1), jnp.min(carry_ref[...]) < SB_DEAD_CARRY))
        osb_buf[slot] = acc_ref[...].reshape(SB_HEADS * SB_HEAD_DIM, t).astype(osb_buf.dtype)

    weight_copies = [pltpu.make_async_copy(src, dst, w_sems.at[n]) for n, (src, dst) in enumerate(
        ((wsb_hbm, wsb_ref), (whg_hbm, whg_ref), (wout_hbm, wout_ref), (w1_hbm, w1_ref),
         (w2_hbm, w2_ref)))]

    @pl.when(i == 0)
    def _():
        for c in weight_copies:
            c.start()
        main_block(True, False)

    @pl.when(i == 1)
    def _():
        for c in weight_copies:
            c.wait()

    @pl.when(jnp.logical_and(i > 0, i < n_tiles))
    def _():
        main_block(True, True)

    @pl.when(i < n_tiles)
    def _():
        sweep_rest()

    @pl.when(i == n_tiles)
    def _():
        main_block(False, True)


def _attn_ffn(q3, k3, vt3, x2, ohg, gates, bg, wsb, whg, wout, g2, w1, w2, gf):
    b, s, sb_w = q3.shape
    n, d = x2.shape
    t = ATTN_TILE
    dff = w1.shape[1]
    tiles_per_seq = s // t
    n_tiles = n // t
    assert sb_w == SB_HEADS * SB_HEAD_DIM and tiles_per_seq * t == s and n_tiles == b * tiles_per_seq
    cur = lambda i: jnp.minimum(i, n_tiles - 1)
    prv = lambda i: (jnp.maximum(i - 1, 0), 0)
    seq = lambda i: cur(i) // tiles_per_seq
    tile = lambda i: cur(i) % tiles_per_seq
    before = lambda i: jnp.maximum(tile(i) - 1, 0)
    k_tile = lambda pos: pl.BlockSpec((1, t, sb_w), lambda i: (seq(i), pos(i), 0))
    vt_tile = lambda pos: pl.BlockSpec((1, sb_w, t), lambda i: (seq(i), 0, pos(i)))
    in_hbm = pl.BlockSpec(memory_space=pl.ANY)
    full = lambda a: _resident(a.shape)
    weights = (wsb, whg, wout, w1, w2)
    resident_bytes = sum(a.size * a.dtype.itemsize for a in (bg, g2, gf) + weights) \
        + 2 * t * sb_w * 2 + 2 * t * sb_w * 2 + SB_HEADS * t * (V7X_LANES * 2 + SB_HEAD_DIM * 4 + 4)
    moving = t * (5 * sb_w * 2 + d * 4 + ohg.shape[1] * 2 + 2 * d * 4 + d * 4)
    return pl.pallas_call(
        functools.partial(_attn_ffn_kernel, tiles_per_seq=tiles_per_seq, n_tiles=n_tiles),
        grid=(n_tiles + 1,),
        in_specs=[k_tile(tile), k_tile(tile), k_tile(before), vt_tile(tile), vt_tile(before),
                  in_hbm, in_hbm,
                  pl.BlockSpec((t, d), prv),
                  pl.BlockSpec((t, ohg.shape[1]), prv),
                  pl.BlockSpec((t, 2 * d), prv),
                  full(bg), full(g2), full(gf)] + [in_hbm] * len(weights),
        out_specs=pl.BlockSpec((t, d), prv),
        out_shape=jax.ShapeDtypeStruct((n, d), x2.dtype),
        scratch_shapes=[pltpu.VMEM((2, sb_w, t), BF16),
                        pltpu.VMEM((SB_HEADS, t, V7X_LANES), BF16),
                        pltpu.VMEM((SB_HEADS, SB_HEAD_DIM, t), F32),
                        pltpu.VMEM((SB_HEADS, t), F32),
                        pltpu.VMEM((t, sb_w), BF16),
                        pltpu.VMEM((sb_w, t), BF16),
                        pltpu.SemaphoreType.DMA((2,))]
        + [pltpu.VMEM(a.shape, a.dtype) for a in weights]
        + [pltpu.SemaphoreType.DMA((len(weights),))],
        compiler_params=pltpu.CompilerParams(
            dimension_semantics=("arbitrary",),
            vmem_limit_bytes=_vmem_limit(moving, resident_bytes,
                                         t * (dff * 6 + d * 24) + 8 * SB_HEADS * t * t * 4)),
        name="attn_ffn",
    )(q3, k3, k3, vt3, vt3, k3, vt3, x2, ohg, gates, bg, g2, gf, *weights)


def kernel(x, norm1_g, w_in, b_gate, lb_logits, hg_norm_g, w_o_sb, w_o_hg, w_out, norm2_g,
           w_ff1, w_ff2, final_g):
    b, s, d = x.shape
    assert w_in.shape[0] == 1, "single-layer block"
    sb_w = SB_HEADS * SB_HEAD_DIM
    hg_w = HG_HEADS * HG_HEAD_DIM
    x2 = x.reshape(b * s, d)
    later = (w_o_sb[0], w_o_hg[0], w_out[0], w_ff1[0], w_ff2[0])
    q, k, vt, gates, o_hg, wsb, whg, wout, w1, w2 = _proj_hgrn2(
        x2, norm1_g, w_in[0], lb_logits, hg_norm_g, later, b, sb_w, hg_w)
    out = _attn_ffn(q.reshape(b, s, sb_w), k.reshape(b, s, sb_w), vt, x2, o_hg, gates, b_gate,
                    wsb, whg, wout, norm2_g, w1, w2, final_g.reshape(1, d))
    return out.reshape(b, s, d)
```
